```python
import math
import jax, jax.numpy as jnp
from jax import lax
import numpy as np

D_MODEL = 1024
BATCH = 2
SEQ = 8192
DEPTH = 4

HEAD_DIM = 64
D_ATTN = 512
N_ATTN_HEADS = D_ATTN // HEAD_DIM
D_SGU = 512
N_SGU_GROUPS = 4
SGU_GROUP_DIM = D_SGU // N_SGU_GROUPS
CHUNK = 128
D_MIX = D_ATTN + D_SGU
D_IN = 3 * D_ATTN + 2 * D_SGU
D_FF = 2816
DILATED_PATTERNS = ((128, 1), (512, 4), (2048, 16))
ROPE_THETA = 500000.0
ROPE_DIM = HEAD_DIM // 4
EPS = 1e-6
NEG_INF = -1e30

kernel_name = "hybrid_dilated_attn_sgu_macaron"


def rms_norm(x, g):
    xf = x.astype(jnp.float32)
    y = xf * lax.rsqrt(jnp.mean(xf * xf, axis=-1, keepdims=True) + EPS)
    return (y * g.astype(jnp.float32)).astype(x.dtype)


def layer_norm(x, g, b):
    xf = x.astype(jnp.float32)
    mu = jnp.mean(xf, axis=-1, keepdims=True)
    xc = xf - mu
    y = xc * lax.rsqrt(jnp.mean(xc * xc, axis=-1, keepdims=True) + EPS)
    return (y * g.astype(jnp.float32) + b.astype(jnp.float32)).astype(x.dtype)


def swiglu(h, w_gate, w_up, w_down):
    return (jax.nn.silu(h @ w_gate) * (h @ w_up)) @ w_down


def partial_rotary(x, positions):
    half = ROPE_DIM // 2
    inv_freq = ROPE_THETA ** (-jnp.arange(0, ROPE_DIM, 2, dtype=jnp.float32) / ROPE_DIM)
    ang = positions.astype(jnp.float32)[:, None] * inv_freq[None, :]
    cos, sin = jnp.cos(ang), jnp.sin(ang)
    xr = x[..., :ROPE_DIM].astype(jnp.float32)
    x1, x2 = xr[..., :half], xr[..., half:]
    rot = jnp.concatenate([x1 * cos - x2 * sin, x1 * sin + x2 * cos], axis=-1)
    return jnp.concatenate([rot.astype(x.dtype), x[..., ROPE_DIM:]], axis=-1)


def banded_attention(q, k, v, half):
    *lead, L, Dh = q.shape
    blk = half
    nb = -(-L // blk)
    pad = nb * blk - L
    lead_pad = [(0, 0)] * len(lead)

    def blocks(t):
        t = jnp.pad(t, lead_pad + [(0, pad), (0, 0)])
        return t.reshape(*lead, nb, blk, Dh)

    def windows(t):
        tp = jnp.pad(t, lead_pad + [(1, 1), (0, 0), (0, 0)])
        return jnp.concatenate([tp[..., :-2, :, :], tp[..., 1:-1, :, :], tp[..., 2:, :, :]], axis=-2)

    qb = blocks(q)
    kw = windows(blocks(k))
    vw = windows(blocks(v))
    s = jnp.einsum("...nqd,...nkd->...nqk", qb, kw).astype(jnp.float32) * (1.0 / math.sqrt(Dh))
    n = jnp.arange(nb)[:, None, None]
    qi = n * blk + jnp.arange(blk)[None, :, None]
    ki = (n - 1) * blk + jnp.arange(3 * blk)[None, None, :]
    mask = (jnp.abs(ki - qi) <= half) & (ki >= 0) & (ki < L)
    s = jnp.where(mask, s, NEG_INF)
    lse = jax.nn.logsumexp(s, axis=-1)
    p = jnp.exp(s - lse[..., None])
    o = jnp.einsum("...nqk,...nkd->...nqd", p.astype(v.dtype), vw)
    o = o.reshape(*lead, nb * blk, Dh)[..., :L, :]
    lse = lse.reshape(*lead, nb * blk)[..., :L]
    return o, lse


def dilated_attention(q, k, v):
    B, H, S, Dh = q.shape
    outs, lses = [], []
    for window, dil in DILATED_PATTERNS:
        half = window // 2 // dil

        def to_strided(t):
            return t.reshape(B, H, S // dil, dil, Dh).swapaxes(2, 3)

        o, l = banded_attention(to_strided(q), to_strided(k), to_strided(v), half)
        outs.append(o.swapaxes(2, 3).reshape(B, H, S, Dh))
        lses.append(l.swapaxes(2, 3).reshape(B, H, S))
    w = jax.nn.softmax(jnp.stack(lses, axis=0), axis=0)
    out = jnp.einsum("pbhs,pbhsd->bhsd", w, jnp.stack(outs, axis=0).astype(jnp.float32))
    return out.astype(q.dtype)


def spatial_gating(u, v, ln_g, ln_b, w_s, b_s):
    B, S, _ = v.shape
    u = jax.nn.gelu(u, approximate=False)
    v = layer_norm(jax.nn.gelu(v, approximate=False), ln_g, ln_b)
    vc = v.reshape(B, S // CHUNK, CHUNK, N_SGU_GROUPS, SGU_GROUP_DIM)
    mixed = jnp.einsum("gts,bcsge->bctge", w_s, vc) + b_s.T[None, None, :, :, None]
    return u * mixed.reshape(B, S, D_SGU)


def setup_inputs(seed: int = 0) -> dict:
    key = jax.random.key(seed)
    ks = jax.random.split(key, 20)
    f32 = jnp.float32

    def nrm(k, shape, scale):
        return jax.random.normal(k, shape, f32) * scale

    def gain(k, shape):
        return 1.0 + 0.02 * jax.random.normal(k, shape, f32)

    L = DEPTH
    return {
        "x": jax.random.normal(ks[0], (BATCH, SEQ, D_MODEL), f32),
        "norm_ffn1": gain(ks[1], (L, D_MODEL)),
        "ffn1_w_gate": nrm(ks[2], (L, D_MODEL, D_FF), D_MODEL ** -0.5),
        "ffn1_w_up": nrm(ks[3], (L, D_MODEL, D_FF), D_MODEL ** -0.5),
        "ffn1_w_down": nrm(ks[4], (L, D_FF, D_MODEL), D_FF ** -0.5),
        "norm_mix": gain(ks[5], (L, D_MODEL)),
        "w_in": nrm(ks[6], (L, D_MODEL, D_IN), D_MODEL ** -0.5),
        "sgu_ln_g": gain(ks[7], (L, D_SGU)),
        "sgu_ln_b": nrm(ks[8], (L, D_SGU), 0.02),
        "sgu_w": nrm(ks[9], (L, N_SGU_GROUPS, CHUNK, CHUNK), 0.5 * CHUNK ** -0.5),
        "sgu_b": gain(ks[10], (L, N_SGU_GROUPS, CHUNK)),
        "out_norm_attn": gain(ks[11], (L, D_ATTN)),
        "out_norm_sgu": gain(ks[12], (L, D_SGU)),
        "w_out": nrm(ks[13], (L, D_MIX, D_MODEL), D_MIX ** -0.5),
        "norm_ffn2": gain(ks[14], (L, D_MODEL)),
        "ffn2_w_gate": nrm(ks[15], (L, D_MODEL, D_FF), D_MODEL ** -0.5),
        "ffn2_w_up": nrm(ks[16], (L, D_MODEL, D_FF), D_MODEL ** -0.5),
        "ffn2_w_down": nrm(ks[17], (L, D_FF, D_MODEL), D_FF ** -0.5),
        "final_norm": gain(ks[18], (D_MODEL,)),
    }


def reference(x, norm_ffn1, ffn1_w_gate, ffn1_w_up, ffn1_w_down, norm_mix, w_in,
              sgu_ln_g, sgu_ln_b, sgu_w, sgu_b, out_norm_attn, out_norm_sgu, w_out,
              norm_ffn2, ffn2_w_gate, ffn2_w_up, ffn2_w_down, final_norm):
    B, S, _ = x.shape
    positions = jnp.arange(S, dtype=jnp.int32)
    splits = [D_ATTN, 2 * D_ATTN, 3 * D_ATTN, 3 * D_ATTN + D_SGU]

    def heads(t):
        return t.reshape(B, S, N_ATTN_HEADS, HEAD_DIM).transpose(0, 2, 1, 3)

    for l in range(DEPTH):
        x = x + 0.5 * swiglu(rms_norm(x, norm_ffn1[l]), ffn1_w_gate[l], ffn1_w_up[l], ffn1_w_down[l])

        h = rms_norm(x, norm_mix[l])
        proj = h @ w_in[l]
        q, k, v, u, g = jnp.split(proj, splits, axis=-1)
        q = partial_rotary(heads(q), positions)
        k = partial_rotary(heads(k), positions)
        a = dilated_attention(q, k, heads(v))
        a = a.transpose(0, 2, 1, 3).reshape(B, S, D_ATTN)
        sg = spatial_gating(u, g, sgu_ln_g[l], sgu_ln_b[l], sgu_w[l], sgu_b[l])
        mixed = jnp.concatenate([rms_norm(a, out_norm_attn[l]), rms_norm(sg, out_norm_sgu[l])], axis=-1)
        x = x + mixed @ w_out[l]

        x = x + 0.5 * swiglu(rms_norm(x, norm_ffn2[l]), ffn2_w_gate[l], ffn2_w_up[l], ffn2_w_down[l])

    return rms_norm(x, final_norm)
```

```python
import functools
import math

import numpy as np
import jax
import jax.numpy as jnp
from jax import lax
from jax.experimental import pallas as pl
from jax.experimental.pallas import tpu as pltpu

F32 = jnp.float32
BF16 = jnp.bfloat16

D_MODEL = 1024
DEPTH = 4
HEAD_DIM = 64
D_ATTN = 512
D_SGU = 512
N_SGU_GROUPS = 4
CHUNK = 128
D_IN = 3 * D_ATTN + 2 * D_SGU
D_FF = 2816
DILATIONS = (1, 4, 16)
BAND = 64
ROPE_THETA = 500000.0
ROPE_DIM = HEAD_DIM // 4
EPS = 1e-6
NEG_INF = -1e30

LANES = 128
V7X_VMEM_BYTES = 64 * 1024 * 1024
VMEM_LIMIT = V7X_VMEM_BYTES * 7 // 8

TM = 512
FF_CHUNK = 256
ATT_ROWS = 512
QB = 2 * BAND
KW = QB + 2 * BAND


def _rms(x, gain):
    return x * lax.rsqrt(jnp.mean(x * x, axis=-1, keepdims=True) + EPS) * gain


def _params(n_axes):
    return pltpu.CompilerParams(dimension_semantics=("arbitrary",) * n_axes,
                                vmem_limit_bytes=VMEM_LIMIT)


def _resident(shape, layer):
    nd = len(shape)
    return pl.BlockSpec((None,) + shape, lambda *_: (layer,) + (0,) * nd,
                        pipeline_mode=pl.Buffered(1))


def _rows(width):
    return pl.BlockSpec((TM, width), lambda i: (i, 0))


def _ffn_kernel(x_ref, gain_ref, wg_ref, wu_ref, wd_ref, fgain_ref, o_ref, act_ref, *, final):
    x = x_ref[...]
    h = _rms(x, gain_ref[...]).astype(BF16)
    for c in range(D_FF // FF_CHUNK):
        cols = slice(c * FF_CHUNK, (c + 1) * FF_CHUNK)
        g = jnp.dot(h, wg_ref[:, cols], preferred_element_type=F32)
        u = jnp.dot(h, wu_ref[:, cols], preferred_element_type=F32)
        act_ref[:, cols] = (g * jax.nn.sigmoid(g) * u).astype(BF16)
    y = x + 0.5 * jnp.dot(act_ref[...], wd_ref[...], preferred_element_type=F32)
    o_ref[...] = _rms(y, fgain_ref[...]) if final else y


def _ffn(x, gain, wg, wu, wd, fgain, layer, final):
    tokens = x.shape[0]
    return pl.pallas_call(
        functools.partial(_ffn_kernel, final=final),
        grid=(tokens // TM,),
        in_specs=[_rows(D_MODEL), _resident((1, D_MODEL), layer),
                  _resident((D_MODEL, D_FF), layer), _resident((D_MODEL, D_FF), layer),
                  _resident((D_FF, D_MODEL), layer),
                  pl.BlockSpec((1, D_MODEL), lambda i: (0, 0))],
        out_specs=_rows(D_MODEL),
        out_shape=jax.ShapeDtypeStruct((tokens, D_MODEL), F32),
        scratch_shapes=[pltpu.VMEM((TM, D_FF), BF16)],
        compiler_params=_params(1),
        name="ffn",
    )(x, gain, wg, wu, wd, fgain)


def _gelu(x):
    return 0.5 * x * (1.0 + lax.erf(x * (1.0 / math.sqrt(2.0))))


def _inproj_kernel(x_ref, gain_ref, w_ref, cos_ref, sina_ref, sinb_ref, lng_ref, lnb_ref,
                   sw_ref, sb_ref, q_ref, k_ref, v_ref, sg_ref):
    h = _rms(x_ref[...], gain_ref[...]).astype(BF16)
    cos, sina, sinb = cos_ref[...], sina_ref[...], sinb_ref[...]

    def proj(col0, width):
        return jnp.dot(h, w_ref[:, col0:col0 + width], preferred_element_type=F32)

    def rope_store(dst_ref, col0):
        t = proj(col0, D_ATTN)
        for s in range(D_ATTN // LANES):
            ts = t[:, s * LANES:(s + 1) * LANES]
            half = ROPE_DIM // 2
            ts = (ts * cos + pltpu.roll(ts, half, 1) * sina
                  + pltpu.roll(ts, LANES - half, 1) * sinb)
            dst_ref[:, s * LANES:(s + 1) * LANES] = ts.astype(BF16)

    rope_store(q_ref, 0)
    rope_store(k_ref, D_ATTN)
    v_ref[...] = proj(2 * D_ATTN, D_ATTN).astype(BF16)

    u = _gelu(proj(3 * D_ATTN, D_SGU))
    g = _gelu(proj(3 * D_ATTN + D_SGU, D_SGU))
    gc = g - jnp.mean(g, axis=-1, keepdims=True)
    gn = gc * lax.rsqrt(jnp.mean(gc * gc, axis=-1, keepdims=True) + EPS) * lng_ref[...] + lnb_ref[...]
    gn = gn.astype(BF16)
    group = D_SGU // N_SGU_GROUPS
    for c in range(TM // CHUNK):
        rows = slice(c * CHUNK, (c + 1) * CHUNK)
        for gi in range(N_SGU_GROUPS):
            cols = slice(gi * group, (gi + 1) * group)
            mixed = jnp.dot(sw_ref[gi], gn[rows, cols], preferred_element_type=F32) + sb_ref[gi]
            sg_ref[rows, cols] = (u[rows, cols] * mixed).astype(BF16)


def _inproj(x, gain, w_in, rope, lng, lnb, sw, sb, layer, seq):
    tokens = x.shape[0]
    table = pl.BlockSpec((TM, LANES), lambda i: (i % (seq // TM), 0))
    out = jax.ShapeDtypeStruct((tokens, D_ATTN), BF16)
    return pl.pallas_call(
        _inproj_kernel,
        grid=(tokens // TM,),
        in_specs=[_rows(D_MODEL), _resident((1, D_MODEL), layer), _resident((D_MODEL, D_IN), layer),
                  table, table, table,
                  _resident((1, D_SGU), layer), _resident((1, D_SGU), layer),
                  _resident((N_SGU_GROUPS, CHUNK, CHUNK), layer),
                  _resident((N_SGU_GROUPS, CHUNK, 1), layer)],
        out_specs=[_rows(D_ATTN)] * 4,
        out_shape=[out] * 4,
        compiler_params=_params(1),
        name="inproj_sgu",
    )(x, gain, w_in, *rope, lng, lnb, sw, sb)


def _rope_tables(seq):
    half = ROPE_DIM // 2
    inv_freq = ROPE_THETA ** (-np.arange(0, ROPE_DIM, 2, dtype=np.float64) / ROPE_DIM)
    ang = np.arange(seq, dtype=np.float64)[:, None] * inv_freq[None, :]
    cos = np.ones((seq, LANES))
    sina = np.zeros((seq, LANES))
    sinb = np.zeros((seq, LANES))
    for head0 in range(0, LANES, HEAD_DIM):
        cos[:, head0:head0 + half] = np.cos(ang)
        cos[:, head0 + half:head0 + 2 * half] = np.cos(ang)
        sinb[:, head0:head0 + half] = -np.sin(ang)
        sina[:, head0 + half:head0 + 2 * half] = np.sin(ang)
    return tuple(jnp.asarray(t, F32) for t in (cos, sina, sinb))


def _attn_kernel(q_ref, kc_ref, kp_ref, kn_ref, vc_ref, vp_ref, vn_ref, o_ref, lse_ref,
                 kbuf, vbuf, *, sub_len):
    for buf, prev, cur, nxt in ((kbuf, kp_ref, kc_ref, kn_ref), (vbuf, vp_ref, vc_ref, vn_ref)):
        buf[0:BAND] = prev[...]
        buf[BAND:BAND + ATT_ROWS] = cur[...]
        buf[BAND + ATT_ROWS:] = nxt[...]

    first_row = (pl.program_id(0) % (sub_len // ATT_ROWS)) * ATT_ROWS
    delta = (lax.broadcasted_iota(jnp.int32, (QB, KW), 1)
             - lax.broadcasted_iota(jnp.int32, (QB, KW), 0))
    band_bias = jnp.where((delta >= 0) & (delta <= 2 * BAND), 0.0, NEG_INF)
    key_col = lax.broadcasted_iota(jnp.int32, (1, KW), 1)
    low_head = lax.broadcasted_iota(jnp.int32, (1, LANES), 1) < HEAD_DIM
    scale = 1.0 / math.sqrt(HEAD_DIM)

    def tile(j, carry):
        row0 = pl.multiple_of(j * QB, QB)
        key_pos = key_col + (first_row + row0 - BAND)
        bias = band_bias + jnp.where((key_pos >= 0) & (key_pos < sub_len), 0.0, NEG_INF)
        for pair in range(D_ATTN // LANES):
            cols = slice(pair * LANES, (pair + 1) * LANES)
            q2 = q_ref[pl.ds(row0, QB), cols]
            k2 = kbuf[pl.ds(row0, KW), cols]
            v2 = vbuf[pl.ds(row0, KW), cols]
            outs, lses = [], []
            for head_mask in (low_head, ~low_head):
                qh = jnp.where(head_mask, q2, jnp.zeros_like(q2))
                s = lax.dot_general(qh, k2, (((1,), (1,)), ((), ())),
                                    preferred_element_type=F32) * scale + bias
                m = jnp.max(s, axis=-1, keepdims=True)
                p = jnp.exp(s - m)
                denom = jnp.sum(p, axis=-1, keepdims=True)
                pv = jnp.dot(p.astype(BF16), v2, preferred_element_type=F32)
                outs.append(pv / denom)
                lses.append(m + jnp.log(denom))
            o_ref[pl.ds(row0, QB), cols] = jnp.where(low_head, outs[0], outs[1]).astype(BF16)
            lse_ref[pl.ds(row0, QB), cols] = jnp.where(low_head, lses[0], lses[1])
        return carry

    lax.fori_loop(0, ATT_ROWS // QB, tile, 0)


def _attention(q, k, v, dil, seq):
    tokens = q.shape[0]
    rows = tokens // dil
    view = lambda t: t.reshape(rows, dil * D_ATTN)
    halos_per_step = ATT_ROWS // BAND
    last_halo = rows // BAND - 1
    cur = pl.BlockSpec((ATT_ROWS, D_ATTN), lambda i, r: (i, r))
    prev = pl.BlockSpec((BAND, D_ATTN), lambda i, r: (jnp.maximum(i * halos_per_step - 1, 0), r))
    nxt = pl.BlockSpec((BAND, D_ATTN),
                       lambda i, r: (jnp.minimum((i + 1) * halos_per_step, last_halo), r))
    o, lse = pl.pallas_call(
        functools.partial(_attn_kernel, sub_len=seq // dil),
        grid=(rows // ATT_ROWS, dil),
        in_specs=[cur, cur, prev, nxt, cur, prev, nxt],
        out_specs=[cur, cur],
        out_shape=[jax.ShapeDtypeStruct((rows, dil * D_ATTN), BF16),
                   jax.ShapeDtypeStruct((rows, dil * D_ATTN), F32)],
        scratch_shapes=[pltpu.VMEM((ATT_ROWS + 2 * BAND, D_ATTN), BF16)] * 2,
        compiler_params=_params(2),
        name=f"attn_d{dil}",
    )(view(q), view(k), view(k), view(k), view(v), view(v), view(v))
    return o.reshape(tokens, D_ATTN), lse.reshape(tokens, D_ATTN)


def _outproj_kernel(x_ref, o1_ref, l1_ref, o2_ref, l2_ref, o3_ref, l3_ref, sg_ref,
                    ga_ref, gs_ref, wo_ref, out_ref):
    l1, l2, l3 = l1_ref[...], l2_ref[...], l3_ref[...]
    m = jnp.maximum(jnp.maximum(l1, l2), l3)
    e1, e2, e3 = jnp.exp(l1 - m), jnp.exp(l2 - m), jnp.exp(l3 - m)
    a = (e1 * o1_ref[...].astype(F32) + e2 * o2_ref[...].astype(F32)
         + e3 * o3_ref[...].astype(F32)) / (e1 + e2 + e3)
    na = _rms(a, ga_ref[...]).astype(BF16)
    ns = _rms(sg_ref[...].astype(F32), gs_ref[...]).astype(BF16)
    y = (jnp.dot(na, wo_ref[0:D_ATTN, :], preferred_element_type=F32)
         + jnp.dot(ns, wo_ref[D_ATTN:, :], preferred_element_type=F32))
    out_ref[...] = x_ref[...] + y


def _outproj(x, attn, sg, ga, gs, wo, layer):
    tokens = x.shape[0]
    return pl.pallas_call(
        _outproj_kernel,
        grid=(tokens // TM,),
        in_specs=[_rows(D_MODEL)] + [_rows(D_ATTN)] * 7
                 + [_resident((1, D_ATTN), layer), _resident((1, D_SGU), layer),
                    _resident((D_ATTN + D_SGU, D_MODEL), layer)],
        out_specs=_rows(D_MODEL),
        out_shape=jax.ShapeDtypeStruct((tokens, D_MODEL), F32),
        compiler_params=_params(1),
        name="merge_outproj",
    )(x, *attn, sg, ga, gs, wo)


def kernel(x, norm_ffn1, ffn1_w_gate, ffn1_w_up, ffn1_w_down, norm_mix, w_in, sgu_ln_g, sgu_ln_b,
           sgu_w, sgu_b, out_norm_attn, out_norm_sgu, w_out, norm_ffn2, ffn2_w_gate, ffn2_w_up,
           ffn2_w_down, final_norm):
    batch, seq, _ = x.shape
    tokens = batch * seq
    assert x.shape[2] == D_MODEL and tokens % TM == 0 and seq % TM == 0 and TM % CHUNK == 0
    assert all(seq // d % ATT_ROWS == 0 for d in DILATIONS)

    row = lambda p: p[:, None, :]
    bf = lambda w: w.astype(BF16)
    w1 = (bf(ffn1_w_gate), bf(ffn1_w_up), bf(ffn1_w_down))
    w2 = (bf(ffn2_w_gate), bf(ffn2_w_up), bf(ffn2_w_down))
    w_in_b, w_out_b, sgu_w_b = bf(w_in), bf(w_out), bf(sgu_w)
    n1, n2, nm = row(norm_ffn1), row(norm_ffn2), row(norm_mix)
    lng, lnb = row(sgu_ln_g), row(sgu_ln_b)
    ga, gs = row(out_norm_attn), row(out_norm_sgu)
    sgu_b_col = sgu_b[..., None]
    fgain = final_norm[None, :]
    rope = _rope_tables(seq)

    xt = x.reshape(tokens, D_MODEL)
    for layer in range(DEPTH):
        xt = _ffn(xt, n1, *w1, fgain, layer, False)
        q, k, v, sg = _inproj(xt, nm, w_in_b, rope, lng, lnb, sgu_w_b, sgu_b_col, layer, seq)
        attn = []
        for dil in DILATIONS:
            attn.extend(_attention(q, k, v, dil, seq))
        xt = _outproj(xt, attn, sg, ga, gs, w_out_b, layer)
        xt = _ffn(xt, n2, *w2, fgain, layer, layer == DEPTH - 1)
    return xt.reshape(batch, seq, D_MODEL)
```

```python
import functools
import math

import numpy as np
import jax
import jax.numpy as jnp
from jax import lax
from jax.experimental import pallas as pl
from jax.experimental.pallas import tpu as pltpu

F32 = jnp.float32
BF16 = jnp.bfloat16

D_MODEL = 1024
DEPTH = 4
HEAD_DIM = 64
D_ATTN = 512
D_SGU = 512
N_SGU_GROUPS = 4
CHUNK = 128
D_IN = 3 * D_ATTN + 2 * D_SGU
D_FF = 2816
DILATIONS = (1, 4, 16)
BAND = 64
ROPE_THETA = 500000.0
ROPE_DIM = HEAD_DIM // 4
EPS = 1e-6
NEG_INF = -1e30

LANES = 128
V7X_VMEM_BYTES = 64 * 1024 * 1024
VMEM_LIMIT = V7X_VMEM_BYTES * 7 // 8

TM = 512
FF_CHUNK = 256
ATT_ROWS = 512
QB = 2 * BAND
KW = QB + 2 * BAND


def _rms(x, gain):
    return x * lax.rsqrt(jnp.mean(x * x, axis=-1, keepdims=True) + EPS) * gain


def _params(n_axes):
    return pltpu.CompilerParams(dimension_semantics=("arbitrary",) * n_axes,
                                vmem_limit_bytes=VMEM_LIMIT)


def _resident(shape, layer):
    nd = len(shape)
    return pl.BlockSpec((None,) + shape, lambda *_: (layer,) + (0,) * nd,
                        pipeline_mode=pl.Buffered(1))


def _rows(width):
    return pl.BlockSpec((TM, width), lambda i: (i, 0))


def _view_spec(dil):
    return pl.BlockSpec((TM // dil, dil * D_ATTN), lambda i: (i, 0))


def _view_shape(tokens, dil, dtype):
    return jax.ShapeDtypeStruct((tokens // dil, dil * D_ATTN), dtype)


def _ffn_kernel(x_ref, gain_ref, wg_ref, wu_ref, wd_ref, fgain_ref, o_ref, act_ref, *, final):
    x = x_ref[...]
    h = _rms(x, gain_ref[...]).astype(BF16)
    for c in range(D_FF // FF_CHUNK):
        cols = slice(c * FF_CHUNK, (c + 1) * FF_CHUNK)
        g = jnp.dot(h, wg_ref[:, cols], preferred_element_type=F32)
        u = jnp.dot(h, wu_ref[:, cols], preferred_element_type=F32)
        act_ref[:, cols] = (g * jax.nn.sigmoid(g) * u).astype(BF16)
    y = x + 0.5 * jnp.dot(act_ref[...], wd_ref[...], preferred_element_type=F32)
    o_ref[...] = _rms(y, fgain_ref[...]) if final else y


def _ffn(x, gain, wg, wu, wd, fgain, layer, final):
    tokens = x.shape[0]
    return pl.pallas_call(
        functools.partial(_ffn_kernel, final=final),
        grid=(tokens // TM,),
        in_specs=[_rows(D_MODEL), _resident((1, D_MODEL), layer),
                  _resident((D_MODEL, D_FF), layer), _resident((D_MODEL, D_FF), layer),
                  _resident((D_FF, D_MODEL), layer),
                  pl.BlockSpec((1, D_MODEL), lambda i: (0, 0))],
        out_specs=_rows(D_MODEL),
        out_shape=jax.ShapeDtypeStruct((tokens, D_MODEL), F32),
        scratch_shapes=[pltpu.VMEM((TM, D_FF), BF16)],
        compiler_params=_params(1),
        name="ffn",
    )(x, gain, wg, wu, wd, fgain)


def _gelu(x):
    return 0.5 * x * (1.0 + lax.erf(x * (1.0 / math.sqrt(2.0))))


def _store_views(t, dst_refs, stage_ref):
    dst_refs[0][...] = t.astype(BF16)
    for s in range(D_ATTN // LANES):
        stage_ref[s] = t[:, s * LANES:(s + 1) * LANES]
    for dil, dst in zip(DILATIONS[1:], dst_refs[1:]):
        for r in range(dil):
            for s in range(D_ATTN // LANES):
                col0 = r * D_ATTN + s * LANES
                dst[:, col0:col0 + LANES] = (
                    stage_ref[s, pl.ds(r, TM // dil, stride=dil), :].astype(BF16))


def _inproj_kernel(x_ref, gain_ref, w_ref, cos_ref, sina_ref, sinb_ref, lng_ref, lnb_ref,
                   sw_ref, sb_ref, *out_and_scratch):
    n_pat = len(DILATIONS)
    q_refs, k_refs, v_refs = (out_and_scratch[i * n_pat:(i + 1) * n_pat] for i in range(3))
    sg_ref, stage_ref = out_and_scratch[3 * n_pat:]
    h = _rms(x_ref[...], gain_ref[...]).astype(BF16)
    cos, sina, sinb = cos_ref[...], sina_ref[...], sinb_ref[...]

    def proj(col0, width):
        return jnp.dot(h, w_ref[:, col0:col0 + width], preferred_element_type=F32)

    def rope(t):
        half = ROPE_DIM // 2
        slabs = []
        for s in range(D_ATTN // LANES):
            ts = t[:, s * LANES:(s + 1) * LANES]
            slabs.append(ts * cos + pltpu.roll(ts, half, 1) * sina
                         + pltpu.roll(ts, LANES - half, 1) * sinb)
        return jnp.concatenate(slabs, axis=1)

    _store_views(rope(proj(0, D_ATTN)), q_refs, stage_ref)
    _store_views(rope(proj(D_ATTN, D_ATTN)), k_refs, stage_ref)
    _store_views(proj(2 * D_ATTN, D_ATTN), v_refs, stage_ref)

    u = _gelu(proj(3 * D_ATTN, D_SGU))
    g = _gelu(proj(3 * D_ATTN + D_SGU, D_SGU))
    gc = g - jnp.mean(g, axis=-1, keepdims=True)
    gn = gc * lax.rsqrt(jnp.mean(gc * gc, axis=-1, keepdims=True) + EPS) * lng_ref[...] + lnb_ref[...]
    gn = gn.astype(BF16)
    group = D_SGU // N_SGU_GROUPS
    for c in range(TM // CHUNK):
        rows = slice(c * CHUNK, (c + 1) * CHUNK)
        for gi in range(N_SGU_GROUPS):
            cols = slice(gi * group, (gi + 1) * group)
            mixed = jnp.dot(sw_ref[gi], gn[rows, cols], preferred_element_type=F32) + sb_ref[gi]
            sg_ref[rows, cols] = (u[rows, cols] * mixed).astype(BF16)


def _inproj(x, gain, w_in, rope, lng, lnb, sw, sb, layer, seq):
    tokens = x.shape[0]
    table = pl.BlockSpec((TM, LANES), lambda i: (i % (seq // TM), 0))
    views = [_view_spec(d) for d in DILATIONS]
    view_shapes = [_view_shape(tokens, d, BF16) for d in DILATIONS]
    outs = pl.pallas_call(
        _inproj_kernel,
        grid=(tokens // TM,),
        in_specs=[_rows(D_MODEL), _resident((1, D_MODEL), layer), _resident((D_MODEL, D_IN), layer),
                  table, table, table,
                  _resident((1, D_SGU), layer), _resident((1, D_SGU), layer),
                  _resident((N_SGU_GROUPS, CHUNK, CHUNK), layer),
                  _resident((N_SGU_GROUPS, CHUNK, 1), layer)],
        out_specs=views * 3 + [_rows(D_SGU)],
        out_shape=view_shapes * 3 + [jax.ShapeDtypeStruct((tokens, D_SGU), BF16)],
        scratch_shapes=[pltpu.VMEM((D_ATTN // LANES, TM, LANES), F32)],
        compiler_params=_params(1),
        name="inproj_sgu",
    )(x, gain, w_in, *rope, lng, lnb, sw, sb)
    n_pat = len(DILATIONS)
    return outs[:n_pat], outs[n_pat:2 * n_pat], outs[2 * n_pat:3 * n_pat], outs[-1]


def _rope_tables(seq):
    half = ROPE_DIM // 2
    inv_freq = ROPE_THETA ** (-np.arange(0, ROPE_DIM, 2, dtype=np.float64) / ROPE_DIM)
    ang = np.arange(seq, dtype=np.float64)[:, None] * inv_freq[None, :]
    cos = np.ones((seq, LANES))
    sina = np.zeros((seq, LANES))
    sinb = np.zeros((seq, LANES))
    for head0 in range(0, LANES, HEAD_DIM):
        cos[:, head0:head0 + half] = np.cos(ang)
        cos[:, head0 + half:head0 + 2 * half] = np.cos(ang)
        sinb[:, head0:head0 + half] = -np.sin(ang)
        sina[:, head0 + half:head0 + 2 * half] = np.sin(ang)
    return tuple(jnp.asarray(t, F32) for t in (cos, sina, sinb))


def _attn_kernel(q_ref, kc_ref, kp_ref, kn_ref, vc_ref, vp_ref, vn_ref, o_ref, lse_ref,
                 kbuf, vbuf, *, sub_len):
    for buf, prev, cur, nxt in ((kbuf, kp_ref, kc_ref, kn_ref), (vbuf, vp_ref, vc_ref, vn_ref)):
        buf[0:BAND] = prev[...]
        buf[BAND:BAND + ATT_ROWS] = cur[...]
        buf[BAND + ATT_ROWS:] = nxt[...]

    first_row = (pl.program_id(0) % (sub_len // ATT_ROWS)) * ATT_ROWS
    delta = (lax.broadcasted_iota(jnp.int32, (QB, KW), 1)
             - lax.broadcasted_iota(jnp.int32, (QB, KW), 0))
    band_bias = jnp.where((delta >= 0) & (delta <= 2 * BAND), 0.0, NEG_INF)
    key_col = lax.broadcasted_iota(jnp.int32, (1, KW), 1)
    low_head = lax.broadcasted_iota(jnp.int32, (1, LANES), 1) < HEAD_DIM
    scale = 1.0 / math.sqrt(HEAD_DIM)

    def tile(j, carry):
        row0 = pl.multiple_of(j * QB, QB)
        key_pos = key_col + (first_row + row0 - BAND)
        bias = band_bias + jnp.where((key_pos >= 0) & (key_pos < sub_len), 0.0, NEG_INF)
        for pair in range(D_ATTN // LANES):
            cols = slice(pair * LANES, (pair + 1) * LANES)
            q2 = q_ref[pl.ds(row0, QB), cols]
            k2 = kbuf[pl.ds(row0, KW), cols]
            v2 = vbuf[pl.ds(row0, KW), cols]
            outs, lses = [], []
            for head_mask in (low_head, ~low_head):
                qh = jnp.where(head_mask, q2, jnp.zeros_like(q2))
                s = lax.dot_general(qh, k2, (((1,), (1,)), ((), ())),
                                    preferred_element_type=F32) * scale + bias
                m = jnp.max(s, axis=-1, keepdims=True)
                p = jnp.exp(s - m)
                denom = jnp.sum(p, axis=-1, keepdims=True)
                pv = jnp.dot(p.astype(BF16), v2, preferred_element_type=F32)
                outs.append(pv / denom)
                lses.append(m + jnp.log(denom))
            o_ref[pl.ds(row0, QB), cols] = jnp.where(low_head, outs[0], outs[1]).astype(BF16)
            lse_ref[pl.ds(row0, QB), cols] = jnp.where(low_head, lses[0], lses[1])
        return carry

    lax.fori_loop(0, ATT_ROWS // QB, tile, 0)


def _attention(q, k, v, dil, seq):
    rows = q.shape[0]
    halos_per_step = ATT_ROWS // BAND
    last_halo = rows // BAND - 1
    cur = pl.BlockSpec((ATT_ROWS, D_ATTN), lambda i, r: (i, r))
    prev = pl.BlockSpec((BAND, D_ATTN), lambda i, r: (jnp.maximum(i * halos_per_step - 1, 0), r))
    nxt = pl.BlockSpec((BAND, D_ATTN),
                       lambda i, r: (jnp.minimum((i + 1) * halos_per_step, last_halo), r))
    return pl.pallas_call(
        functools.partial(_attn_kernel, sub_len=seq // dil),
        grid=(rows // ATT_ROWS, dil),
        in_specs=[cur, cur, prev, nxt, cur, prev, nxt],
        out_specs=[cur, cur],
        out_shape=[jax.ShapeDtypeStruct(q.shape, BF16), jax.ShapeDtypeStruct(q.shape, F32)],
        scratch_shapes=[pltpu.VMEM((ATT_ROWS + 2 * BAND, D_ATTN), BF16)] * 2,
        compiler_params=_params(2),
        name=f"attn_d{dil}",
    )(q, k, k, k, v, v, v)


def _outproj_kernel(x_ref, *refs):
    n_pat = len(DILATIONS)
    o_refs, l_refs = refs[0:2 * n_pat:2], refs[1:2 * n_pat:2]
    sg_ref, ga_ref, gs_ref, wo_ref, out_ref = refs[2 * n_pat:2 * n_pat + 5]
    stage_refs = refs[2 * n_pat + 5:]

    outs, lses = [o_refs[0][...].astype(F32)], [l_refs[0][...]]
    for p, dil in enumerate(DILATIONS[1:]):
        o_stage, l_stage = stage_refs[2 * p], stage_refs[2 * p + 1]
        for r in range(dil):
            for s in range(D_ATTN // LANES):
                cols = slice(r * D_ATTN + s * LANES, r * D_ATTN + (s + 1) * LANES)
                rows = pl.ds(r, TM // dil, stride=dil)
                o_stage[s, rows, :] = o_refs[p + 1][:, cols].astype(F32)
                l_stage[s, rows, :] = l_refs[p + 1][:, cols]
        outs.append(jnp.concatenate([o_stage[s] for s in range(D_ATTN // LANES)], axis=1))
        lses.append(jnp.concatenate([l_stage[s] for s in range(D_ATTN // LANES)], axis=1))

    m = functools.reduce(jnp.maximum, lses)
    es = [jnp.exp(l - m) for l in lses]
    a = sum(e * o for e, o in zip(es, outs)) / sum(es)
    na = _rms(a, ga_ref[...]).astype(BF16)
    ns = _rms(sg_ref[...].astype(F32), gs_ref[...]).astype(BF16)
    y = (jnp.dot(na, wo_ref[0:D_ATTN, :], preferred_element_type=F32)
         + jnp.dot(ns, wo_ref[D_ATTN:, :], preferred_element_type=F32))
    out_ref[...] = x_ref[...] + y


def _outproj(x, attn, sg, ga, gs, wo, layer):
    tokens = x.shape[0]
    views = [_view_spec(d) for d in DILATIONS for _ in range(2)]
    return pl.pallas_call(
        _outproj_kernel,
        grid=(tokens // TM,),
        in_specs=[_rows(D_MODEL)] + views + [_rows(D_SGU)]
                 + [_resident((1, D_ATTN), layer), _resident((1, D_SGU), layer),
                    _resident((D_ATTN + D_SGU, D_MODEL), layer)],
        out_specs=_rows(D_MODEL),
        out_shape=jax.ShapeDtypeStruct((tokens, D_MODEL), F32),
        scratch_shapes=[pltpu.VMEM((D_ATTN // LANES, TM, LANES), F32)] * (2 * (len(DILATIONS) - 1)),
        compiler_params=_params(1),
        name="merge_outproj",
    )(x, *attn, sg, ga, gs, wo)


def kernel(x, norm_ffn1, ffn1_w_gate, ffn1_w_up, ffn1_w_down, norm_mix, w_in, sgu_ln_g, sgu_ln_b,
           sgu_w, sgu_b, out_norm_attn, out_norm_sgu, w_out, norm_ffn2, ffn2_w_gate, ffn2_w_up,
           ffn2_w_down, final_norm):
    batch, seq, _ = x.shape
    tokens = batch * seq
    assert x.shape[2] == D_MODEL and tokens % TM == 0 and seq % TM == 0 and TM % CHUNK == 0
    assert all(seq // d % ATT_ROWS == 0 for d in DILATIONS)

    row = lambda p: p[:, None, :]
    bf = lambda w: w.astype(BF16)
    w1 = (bf(ffn1_w_gate), bf(ffn1_w_up), bf(ffn1_w_down))
    w2 = (bf(ffn2_w_gate), bf(ffn2_w_up), bf(ffn2_w_down))
    w_in_b, w_out_b, sgu_w_b = bf(w_in), bf(w_out), bf(sgu_w)
    n1, n2, nm = row(norm_ffn1), row(norm_ffn2), row(norm_mix)
    lng, lnb = row(sgu_ln_g), row(sgu_ln_b)
    ga, gs = row(out_norm_attn), row(out_norm_sgu)
    sgu_b_col = sgu_b[..., None]
    fgain = final_norm[None, :]
    rope = _rope_tables(seq)

    xt = x.reshape(tokens, D_MODEL)
    for layer in range(DEPTH):
        xt = _ffn(xt, n1, *w1, fgain, layer, False)
        qs, ks, vs, sg = _inproj(xt, nm, w_in_b, rope, lng, lnb, sgu_w_b, sgu_b_col, layer, seq)
        attn = []
        for q, k, v, dil in zip(qs, ks, vs, DILATIONS):
            attn.extend(_attention(q, k, v, dil, seq))
        xt = _outproj(xt, attn, sg, ga, gs, w_out_b, layer)
        xt = _ffn(xt, n2, *w2, fgain, layer, layer == DEPTH - 1)
    return xt.reshape(batch, seq, D_MODEL)
```

```python
import functools
import math

import numpy as np
import jax
import jax.numpy as jnp
from jax import lax
from jax.experimental import pallas as pl
from jax.experimental.pallas import tpu as pltpu

F32 = jnp.float32
BF16 = jnp.bfloat16

D_MODEL = 1024
DEPTH = 4
HEAD_DIM = 64
D_ATTN = 512
D_SGU = 512
N_SGU_GROUPS = 4
CHUNK = 128
D_IN = 3 * D_ATTN + 2 * D_SGU
D_FF = 2816
DILATIONS = (1, 4, 16)
N_PAT = len(DILATIONS)
BAND = 64
ROPE_THETA = 500000.0
ROPE_DIM = HEAD_DIM // 4
EPS = 1e-6
NEG_INF = -1e30

LANES = 128
V7X_VMEM_BYTES = 64 * 1024 * 1024
VMEM_LIMIT = V7X_VMEM_BYTES * 7 // 8

TM = 512
FF_CHUNK = 256
ATT_ROWS = 512
QB = 2 * BAND
KW = QB + 2 * BAND


def _rms(x, gain):
    return x * lax.rsqrt(jnp.mean(x * x, axis=-1, keepdims=True) + EPS) * gain


def _params(n_axes):
    return pltpu.CompilerParams(dimension_semantics=("arbitrary",) * n_axes,
                                vmem_limit_bytes=VMEM_LIMIT)


def _resident(shape, layer):
    nd = len(shape)
    return pl.BlockSpec((None,) + shape, lambda *_: (layer,) + (0,) * nd,
                        pipeline_mode=pl.Buffered(1))


def _rows(width):
    return pl.BlockSpec((TM, width), lambda i: (i, 0))


def _view_spec(dil):
    return pl.BlockSpec((TM // dil, dil * D_ATTN), lambda i: (i, 0))


def _view_shape(tokens, dil, dtype):
    return jax.ShapeDtypeStruct((tokens // dil, dil * D_ATTN), dtype)


def _ffn_apply(x, gain_ref, wg_ref, wu_ref, wd_ref, act_ref):
    h = _rms(x, gain_ref[...]).astype(BF16)
    for c in range(D_FF // FF_CHUNK):
        cols = slice(c * FF_CHUNK, (c + 1) * FF_CHUNK)
        g = jnp.dot(h, wg_ref[:, cols], preferred_element_type=F32)
        u = jnp.dot(h, wu_ref[:, cols], preferred_element_type=F32)
        act_ref[:, cols] = (g * jax.nn.sigmoid(g) * u).astype(BF16)
    return x + 0.5 * jnp.dot(act_ref[...], wd_ref[...], preferred_element_type=F32)


def _ffn_specs(layer):
    return [_resident((1, D_MODEL), layer), _resident((D_MODEL, D_FF), layer),
            _resident((D_MODEL, D_FF), layer), _resident((D_FF, D_MODEL), layer)]


def _gelu(x):
    return 0.5 * x * (1.0 + lax.erf(x * (1.0 / math.sqrt(2.0))))


def _store_views(t, dst_refs, stage_ref):
    dst_refs[0][...] = t.astype(BF16)
    for s in range(D_ATTN // LANES):
        stage_ref[s] = t[:, s * LANES:(s + 1) * LANES]
    for dil, dst in zip(DILATIONS[1:], dst_refs[1:]):
        for r in range(dil):
            for s in range(D_ATTN // LANES):
                col0 = r * D_ATTN + s * LANES
                dst[:, col0:col0 + LANES] = (
                    stage_ref[s, pl.ds(r, TM // dil, stride=dil), :].astype(BF16))


def _inproj_apply(x, gain_ref, w_ref, cos_ref, sina_ref, sinb_ref, lng_ref, lnb_ref, sw_ref, sb_ref,
                  q_refs, k_refs, v_refs, sg_ref, stage_ref):
    h = _rms(x, gain_ref[...]).astype(BF16)
    cos, sina, sinb = cos_ref[...], sina_ref[...], sinb_ref[...]

    def proj(col0, width):
        return jnp.dot(h, w_ref[:, col0:col0 + width], preferred_element_type=F32)

    def rope(t):
        half = ROPE_DIM // 2
        slabs = []
        for s in range(D_ATTN // LANES):
            ts = t[:, s * LANES:(s + 1) * LANES]
            slabs.append(ts * cos + pltpu.roll(ts, half, 1) * sina
                         + pltpu.roll(ts, LANES - half, 1) * sinb)
        return jnp.concatenate(slabs, axis=1)

    _store_views(rope(proj(0, D_ATTN)) * (1.0 / math.sqrt(HEAD_DIM)), q_refs, stage_ref)
    _store_views(rope(proj(D_ATTN, D_ATTN)), k_refs, stage_ref)
    _store_views(proj(2 * D_ATTN, D_ATTN), v_refs, stage_ref)

    u = _gelu(proj(3 * D_ATTN, D_SGU))
    g = _gelu(proj(3 * D_ATTN + D_SGU, D_SGU))
    gc = g - jnp.mean(g, axis=-1, keepdims=True)
    gn = gc * lax.rsqrt(jnp.mean(gc * gc, axis=-1, keepdims=True) + EPS) * lng_ref[...] + lnb_ref[...]
    gn = gn.astype(BF16)
    group = D_SGU // N_SGU_GROUPS
    for c in range(TM // CHUNK):
        rows = slice(c * CHUNK, (c + 1) * CHUNK)
        for gi in range(N_SGU_GROUPS):
            cols = slice(gi * group, (gi + 1) * group)
            mixed = jnp.dot(sw_ref[gi], gn[rows, cols], preferred_element_type=F32) + sb_ref[gi]
            sg_ref[rows, cols] = (u[rows, cols] * mixed).astype(BF16)


def _ffn_inproj_kernel(x_ref, *refs):
    ffn_refs, inproj_refs = refs[:4], refs[4:13]
    xo_ref = refs[13]
    q_refs, k_refs, v_refs = (refs[14 + i * N_PAT:14 + (i + 1) * N_PAT] for i in range(3))
    sg_ref, act_ref, stage_ref = refs[14 + 3 * N_PAT:]
    x = _ffn_apply(x_ref[...], *ffn_refs, act_ref)
    xo_ref[...] = x
    _inproj_apply(x, *inproj_refs, q_refs, k_refs, v_refs, sg_ref, stage_ref)


def _ffn_inproj(x, ffn_w, gain, w_in, rope, lng, lnb, sw, sb, layer, seq):
    tokens = x.shape[0]
    table = pl.BlockSpec((TM, LANES), lambda i: (i % (seq // TM), 0))
    views = [_view_spec(d) for d in DILATIONS]
    view_shapes = [_view_shape(tokens, d, BF16) for d in DILATIONS]
    outs = pl.pallas_call(
        _ffn_inproj_kernel,
        grid=(tokens // TM,),
        in_specs=[_rows(D_MODEL)] + _ffn_specs(layer)
                 + [_resident((1, D_MODEL), layer), _resident((D_MODEL, D_IN), layer),
                    table, table, table,
                    _resident((1, D_SGU), layer), _resident((1, D_SGU), layer),
                    _resident((N_SGU_GROUPS, CHUNK, CHUNK), layer),
                    _resident((N_SGU_GROUPS, CHUNK, 1), layer)],
        out_specs=[_rows(D_MODEL)] + views * 3 + [_rows(D_SGU)],
        out_shape=[jax.ShapeDtypeStruct((tokens, D_MODEL), F32)] + view_shapes * 3
                  + [jax.ShapeDtypeStruct((tokens, D_SGU), BF16)],
        scratch_shapes=[pltpu.VMEM((TM, D_FF), BF16), pltpu.VMEM((D_ATTN // LANES, TM, LANES), F32)],
        compiler_params=_params(1),
        name="ffn_inproj",
    )(x, *ffn_w, gain, w_in, *rope, lng, lnb, sw, sb)
    return (outs[0], outs[1:1 + N_PAT], outs[1 + N_PAT:1 + 2 * N_PAT],
            outs[1 + 2 * N_PAT:1 + 3 * N_PAT], outs[-1])


def _rope_tables(seq):
    half = ROPE_DIM // 2
    inv_freq = ROPE_THETA ** (-np.arange(0, ROPE_DIM, 2, dtype=np.float64) / ROPE_DIM)
    ang = np.arange(seq, dtype=np.float64)[:, None] * inv_freq[None, :]
    cos = np.ones((seq, LANES))
    sina = np.zeros((seq, LANES))
    sinb = np.zeros((seq, LANES))
    for head0 in range(0, LANES, HEAD_DIM):
        cos[:, head0:head0 + half] = np.cos(ang)
        cos[:, head0 + half:head0 + 2 * half] = np.cos(ang)
        sinb[:, head0:head0 + half] = -np.sin(ang)
        sina[:, head0 + half:head0 + 2 * half] = np.sin(ang)
    return tuple(jnp.asarray(t, F32) for t in (cos, sina, sinb))


def _attn_kernel(q_ref, kc_ref, kp_ref, kn_ref, vc_ref, vp_ref, vn_ref, o_ref, lse_ref,
                 kbuf, vbuf, *, sub_len):
    n_pairs = D_ATTN // LANES
    kbuf[0:BAND] = kp_ref[...]
    kbuf[BAND:BAND + ATT_ROWS] = kc_ref[...]
    kbuf[BAND + ATT_ROWS:] = kn_ref[...]
    ones = jnp.ones((ATT_ROWS + 2 * BAND, LANES), BF16)
    for pair in range(n_pairs):
        src = slice(pair * LANES, (pair + 1) * LANES)
        dst = slice(2 * pair * LANES, (2 * pair + 1) * LANES)
        vbuf[0:BAND, dst] = vp_ref[:, src]
        vbuf[BAND:BAND + ATT_ROWS, dst] = vc_ref[:, src]
        vbuf[BAND + ATT_ROWS:, dst] = vn_ref[:, src]
        vbuf[:, (2 * pair + 1) * LANES:(2 * pair + 2) * LANES] = ones

    first_row = (pl.program_id(0) % (sub_len // ATT_ROWS)) * ATT_ROWS
    delta = (lax.broadcasted_iota(jnp.int32, (QB, KW), 1)
             - lax.broadcasted_iota(jnp.int32, (QB, KW), 0))
    band_bias = jnp.where((delta >= 0) & (delta <= 2 * BAND), 0.0, NEG_INF)
    key_col = lax.broadcasted_iota(jnp.int32, (1, KW), 1)
    low_head = lax.broadcasted_iota(jnp.int32, (1, LANES), 1) < HEAD_DIM

    for j in range(ATT_ROWS // QB):
        row0 = j * QB
        key_pos = key_col + (first_row + row0 - BAND)
        bias = band_bias + jnp.where((key_pos >= 0) & (key_pos < sub_len), 0.0, NEG_INF)
        bias = jnp.concatenate([bias, bias], axis=0)
        for pair in range(n_pairs):
            cols = slice(pair * LANES, (pair + 1) * LANES)
            q2 = q_ref[row0:row0 + QB, cols]
            zero = jnp.zeros_like(q2)
            qs = jnp.concatenate([jnp.where(low_head, q2, zero), jnp.where(low_head, zero, q2)], axis=0)
            s = lax.dot_general(qs, kbuf[row0:row0 + KW, cols], (((1,), (1,)), ((), ())),
                                preferred_element_type=F32) + bias
            m = jnp.max(s, axis=-1, keepdims=True)
            p = jnp.exp(s - m).astype(BF16)
            pv = jnp.dot(p, vbuf[row0:row0 + KW, 2 * pair * LANES:(2 * pair + 2) * LANES],
                         preferred_element_type=F32)
            num = jnp.where(low_head, pv[:QB, :LANES], pv[QB:, :LANES])
            den = jnp.where(low_head, pv[:QB, LANES:], pv[QB:, LANES:])
            o_ref[row0:row0 + QB, cols] = (num / den).astype(BF16)
            lse_ref[row0:row0 + QB, cols] = jnp.where(low_head, m[:QB], m[QB:]) + jnp.log(den)


def _attention(q, k, v, dil, seq):
    rows = q.shape[0]
    halos_per_step = ATT_ROWS // BAND
    last_halo = rows // BAND - 1
    cur = pl.BlockSpec((ATT_ROWS, D_ATTN), lambda i, r: (i, r))
    prev = pl.BlockSpec((BAND, D_ATTN), lambda i, r: (jnp.maximum(i * halos_per_step - 1, 0), r))
    nxt = pl.BlockSpec((BAND, D_ATTN),
                       lambda i, r: (jnp.minimum((i + 1) * halos_per_step, last_halo), r))
    return pl.pallas_call(
        functools.partial(_attn_kernel, sub_len=seq // dil),
        grid=(rows // ATT_ROWS, dil),
        in_specs=[cur, cur, prev, nxt, cur, prev, nxt],
        out_specs=[cur, cur],
        out_shape=[jax.ShapeDtypeStruct(q.shape, BF16), jax.ShapeDtypeStruct(q.shape, F32)],
        scratch_shapes=[pltpu.VMEM((ATT_ROWS + 2 * BAND, D_ATTN), BF16),
                        pltpu.VMEM((ATT_ROWS + 2 * BAND, 2 * D_ATTN), BF16)],
        compiler_params=_params(2),
        name=f"attn_d{dil}",
    )(q, k, k, k, v, v, v)


def _merge_outproj_apply(x, o_refs, l_refs, sg_ref, ga_ref, gs_ref, wo_ref, stage_refs):
    outs, lses = [o_refs[0][...].astype(F32)], [l_refs[0][...]]
    for p, dil in enumerate(DILATIONS[1:]):
        o_stage, l_stage = stage_refs[2 * p], stage_refs[2 * p + 1]
        for r in range(dil):
            for s in range(D_ATTN // LANES):
                cols = slice(r * D_ATTN + s * LANES, r * D_ATTN + (s + 1) * LANES)
                rows = pl.ds(r, TM // dil, stride=dil)
                o_stage[s, rows, :] = o_refs[p + 1][:, cols].astype(F32)
                l_stage[s, rows, :] = l_refs[p + 1][:, cols]
        outs.append(jnp.concatenate([o_stage[s] for s in range(D_ATTN // LANES)], axis=1))
        lses.append(jnp.concatenate([l_stage[s] for s in range(D_ATTN // LANES)], axis=1))

    m = functools.reduce(jnp.maximum, lses)
    es = [jnp.exp(l - m) for l in lses]
    a = sum(e * o for e, o in zip(es, outs)) / sum(es)
    na = _rms(a, ga_ref[...]).astype(BF16)
    ns = _rms(sg_ref[...].astype(F32), gs_ref[...]).astype(BF16)
    y = (jnp.dot(na, wo_ref[0:D_ATTN, :], preferred_element_type=F32)
         + jnp.dot(ns, wo_ref[D_ATTN:, :], preferred_element_type=F32))
    return x + y


def _merge_ffn_kernel(x_ref, *refs, final):
    o_refs, l_refs = refs[0:2 * N_PAT:2], refs[1:2 * N_PAT:2]
    sg_ref, ga_ref, gs_ref, wo_ref = refs[2 * N_PAT:2 * N_PAT + 4]
    ffn_refs = refs[2 * N_PAT + 4:2 * N_PAT + 8]
    fgain_ref, out_ref, act_ref = refs[2 * N_PAT + 8:2 * N_PAT + 11]
    stage_refs = refs[2 * N_PAT + 11:]
    x = _merge_outproj_apply(x_ref[...], o_refs, l_refs, sg_ref, ga_ref, gs_ref, wo_ref, stage_refs)
    y = _ffn_apply(x, *ffn_refs, act_ref)
    out_ref[...] = _rms(y, fgain_ref[...]) if final else y


def _merge_ffn(x, attn, sg, ga, gs, wo, ffn_w, fgain, layer, final):
    tokens = x.shape[0]
    views = [_view_spec(d) for d in DILATIONS for _ in range(2)]
    return pl.pallas_call(
        functools.partial(_merge_ffn_kernel, final=final),
        grid=(tokens // TM,),
        in_specs=[_rows(D_MODEL)] + views + [_rows(D_SGU)]
                 + [_resident((1, D_ATTN), layer), _resident((1, D_SGU), layer),
                    _resident((D_ATTN + D_SGU, D_MODEL), layer)]
                 + _ffn_specs(layer) + [pl.BlockSpec((1, D_MODEL), lambda i: (0, 0))],
        out_specs=_rows(D_MODEL),
        out_shape=jax.ShapeDtypeStruct((tokens, D_MODEL), F32),
        scratch_shapes=[pltpu.VMEM((TM, D_FF), BF16)]
                       + [pltpu.VMEM((D_ATTN // LANES, TM, LANES), F32)] * (2 * (N_PAT - 1)),
        compiler_params=_params(1),
        name="merge_ffn",
    )(x, *attn, sg, ga, gs, wo, *ffn_w, fgain)


def kernel(x, norm_ffn1, ffn1_w_gate, ffn1_w_up, ffn1_w_down, norm_mix, w_in, sgu_ln_g, sgu_ln_b,
           sgu_w, sgu_b, out_norm_attn, out_norm_sgu, w_out, norm_ffn2, ffn2_w_gate, ffn2_w_up,
           ffn2_w_down, final_norm):
    batch, seq, _ = x.shape
    tokens = batch * seq
    assert x.shape[2] == D_MODEL and tokens % TM == 0 and seq % TM == 0 and TM % CHUNK == 0
    assert all(seq // d % ATT_ROWS == 0 for d in DILATIONS)

    row = lambda p: p[:, None, :]
    bf = lambda w: w.astype(BF16)
    ffn1 = (row(norm_ffn1), bf(ffn1_w_gate), bf(ffn1_w_up), bf(ffn1_w_down))
    ffn2 = (row(norm_ffn2), bf(ffn2_w_gate), bf(ffn2_w_up), bf(ffn2_w_down))
    w_in_b, w_out_b, sgu_w_b = bf(w_in), bf(w_out), bf(sgu_w)
    nm = row(norm_mix)
    lng, lnb = row(sgu_ln_g), row(sgu_ln_b)
    ga, gs = row(out_norm_attn), row(out_norm_sgu)
    sgu_b_col = sgu_b[..., None]
    fgain = final_norm[None, :]
    rope = _rope_tables(seq)

    xt = x.reshape(tokens, D_MODEL)
    for layer in range(DEPTH):
        xt, qs, ks, vs, sg = _ffn_inproj(xt, ffn1, nm, w_in_b, rope, lng, lnb, sgu_w_b, sgu_b_col,
                                         layer, seq)
        attn = []
        for q, k, v, dil in zip(qs, ks, vs, DILATIONS):
            attn.extend(_attention(q, k, v, dil, seq))
        xt = _merge_ffn(xt, attn, sg, ga, gs, w_out_b, ffn2, fgain, layer, layer == DEPTH - 1)
    return xt.reshape(batch, seq, D_MODEL)
```

```python
import functools
import math

import numpy as np
import jax
import jax.numpy as jnp
from jax import lax
from jax.experimental import pallas as pl
from jax.experimental.pallas import tpu as pltpu

F32 = jnp.float32
BF16 = jnp.bfloat16

D_MODEL = 1024
DEPTH = 4
HEAD_DIM = 64
D_ATTN = 512
D_SGU = 512
N_SGU_GROUPS = 4
CHUNK = 128
D_IN = 3 * D_ATTN + 2 * D_SGU
D_FF = 2816
DILATIONS = (1, 4, 16)
N_PAT = len(DILATIONS)
BAND = 64
ROPE_THETA = 500000.0
ROPE_DIM = HEAD_DIM // 4
EPS = 1e-6
NEG_INF = -1e30
LOG2_E = math.log2(math.e)
LN_2 = math.log(2.0)

LANES = 128
SUBLANES = 8
V7X_VMEM_BYTES = 64 * 1024 * 1024
VMEM_LIMIT = V7X_VMEM_BYTES * 7 // 8

TM = 512
FF_CHUNK = 256
N_FF_CHUNKS = D_FF // FF_CHUNK
N_SLABS = D_ATTN // LANES
MAX_ATT_ROWS = 1024
QB = 2 * BAND
KW = QB + 2 * BAND


def _rms(x, gain):
    return x * lax.rsqrt(jnp.mean(x * x, axis=-1, keepdims=True) + EPS) * gain


def _params(n_axes):
    return pltpu.CompilerParams(dimension_semantics=("arbitrary",) * n_axes,
                                vmem_limit_bytes=VMEM_LIMIT)


def _resident(shape, layer):
    nd = len(shape)
    return pl.BlockSpec((None,) + shape, lambda *_: (layer,) + (0,) * nd,
                        pipeline_mode=pl.Buffered(1))


def _view_shape(tokens, dil, dtype):
    return jax.ShapeDtypeStruct((tokens // dil, dil * D_ATTN), dtype)


def _ordering_zero(values):
    acc = None
    for v in values:
        bits = pltpu.bitcast(v[0:SUBLANES, 0:LANES], jnp.uint32)
        acc = bits if acc is None else acc | bits
    return ((acc >> 16) >> 16)[0:1, :].astype(F32)


def _ffn_apply(x, gain_ref, wg_ref, wu_ref, wd_ref, act_ref, side_work=()):
    h = _rms(x, gain_ref[...]).astype(BF16)
    for c in range(N_FF_CHUNKS):
        cols = slice(c * FF_CHUNK, (c + 1) * FF_CHUNK)
        g = jnp.dot(h, wg_ref[:, cols], preferred_element_type=F32)
        u = jnp.dot(h, wu_ref[:, cols], preferred_element_type=F32)
        a = g * jax.nn.sigmoid(g) * u
        todo = side_work[c * len(side_work) // N_FF_CHUNKS:(c + 1) * len(side_work) // N_FF_CHUNKS]
        if todo:
            zero = _ordering_zero([v for work in todo for v in work()])
            a = a + jnp.concatenate([zero] * (FF_CHUNK // LANES), axis=1)
        act_ref[:, cols] = a.astype(BF16)
    return x + 0.5 * jnp.dot(act_ref[...], wd_ref[...], preferred_element_type=F32)


def _ffn_specs(layer):
    return [_resident((1, D_MODEL), layer), _resident((D_MODEL, D_FF), layer),
            _resident((D_MODEL, D_FF), layer), _resident((D_FF, D_MODEL), layer)]


def _gelu(x):
    return 0.5 * x * (1.0 + lax.erf(x * (1.0 / math.sqrt(2.0))))


def _inproj_matmul(x, gain_ref, w_ref, raw_ref):
    h = _rms(x, gain_ref[...]).astype(BF16)
    for col0 in range(0, D_IN, D_ATTN):
        raw_ref[:, col0:col0 + D_ATTN] = jnp.dot(h, w_ref[:, col0:col0 + D_ATTN],
                                                 preferred_element_type=F32)


def _inproj_finish_work(raw_ref, cos_ref, sina_ref, sinb_ref, lng_ref, lnb_ref, sw_ref, sb_ref,
                        q_refs, k_refs, v_refs, sg_ref, stage_ref):
    def qkv_slab(col0, dst_refs, rotary, scale, s):
        def work():
            t = raw_ref[:, col0 + s * LANES:col0 + (s + 1) * LANES]
            if rotary:
                half = ROPE_DIM // 2
                t = (t * cos_ref[...] + pltpu.roll(t, half, 1) * sina_ref[...]
                     + pltpu.roll(t, LANES - half, 1) * sinb_ref[...])
            if scale != 1.0:
                t = t * scale
            dst_refs[0][:, s * LANES:(s + 1) * LANES] = t.astype(BF16)
            stage_ref[s] = t
            produced = [t]
            for dil, dst in zip(DILATIONS[1:], dst_refs[1:]):
                for r in range(dil):
                    blk = stage_ref[s, pl.ds(r, TM // dil, stride=dil), :]
                    dst[:, r * D_ATTN + s * LANES:r * D_ATTN + (s + 1) * LANES] = blk.astype(BF16)
                    produced.append(blk)
            return produced
        return work

    def sgu_chunk(c):
        def work():
            rows = slice(c * CHUNK, (c + 1) * CHUNK)
            u = _gelu(raw_ref[rows, 3 * D_ATTN:3 * D_ATTN + D_SGU])
            g = _gelu(raw_ref[rows, 3 * D_ATTN + D_SGU:])
            gc = g - jnp.mean(g, axis=-1, keepdims=True)
            gn = (gc * lax.rsqrt(jnp.mean(gc * gc, axis=-1, keepdims=True) + EPS) * lng_ref[...]
                  + lnb_ref[...]).astype(BF16)
            group = D_SGU // N_SGU_GROUPS
            produced = []
            for gi in range(N_SGU_GROUPS):
                cols = slice(gi * group, (gi + 1) * group)
                mixed = jnp.dot(sw_ref[gi], gn[:, cols], preferred_element_type=F32) + sb_ref[gi]
                gated = u[:, cols] * mixed
                sg_ref[rows, cols] = gated.astype(BF16)
                produced.append(gated)
            return produced
        return work

    q_scale = LOG2_E / math.sqrt(HEAD_DIM)
    pieces = [qkv_slab(0, q_refs, True, q_scale, s) for s in range(N_SLABS)]
    pieces += [qkv_slab(D_ATTN, k_refs, True, 1.0, s) for s in range(N_SLABS)]
    pieces += [qkv_slab(2 * D_ATTN, v_refs, False, 1.0, s) for s in range(N_SLABS)]
    pieces += [sgu_chunk(c) for c in range(TM // CHUNK)]
    return pieces


def _ffn_inproj_kernel(x_ref, *refs, n_tiles):
    ffn_refs, gain_ref, w_ref, finish_refs = refs[:4], refs[4], refs[5], refs[6:13]
    xo_ref = refs[13]
    q_refs, k_refs, v_refs = (refs[14 + i * N_PAT:14 + (i + 1) * N_PAT] for i in range(3))
    sg_ref, act_ref, stage_ref, raw_ref = refs[14 + 3 * N_PAT:]
    step = pl.program_id(0)
    finish = _inproj_finish_work(raw_ref, *finish_refs, q_refs, k_refs, v_refs, sg_ref, stage_ref)

    @pl.when(step == 0)
    def _():
        raw_ref[...] = jnp.zeros_like(raw_ref)

    @pl.when(step < n_tiles)
    def _():
        x = _ffn_apply(x_ref[...], *ffn_refs, act_ref, side_work=finish)
        xo_ref[...] = x
        _inproj_matmul(x, gain_ref, w_ref, raw_ref)

    @pl.when(step == n_tiles)
    def _():
        for work in finish:
            work()


def _ffn_inproj(x, ffn_w, gain, w_in, rope, lng, lnb, sw, sb, layer, seq):
    tokens = x.shape[0]
    n_tiles = tokens // TM
    lead = lambda s: (jnp.minimum(s, n_tiles - 1), 0)
    lag = lambda s: (jnp.maximum(s - 1, 0), 0)
    table = pl.BlockSpec((TM, LANES), lambda s: (jnp.maximum(s - 1, 0) % (seq // TM), 0))
    views = [pl.BlockSpec((TM // d, d * D_ATTN), lag) for d in DILATIONS]
    view_shapes = [_view_shape(tokens, d, BF16) for d in DILATIONS]
    outs = pl.pallas_call(
        functools.partial(_ffn_inproj_kernel, n_tiles=n_tiles),
        grid=(n_tiles + 1,),
        in_specs=[pl.BlockSpec((TM, D_MODEL), lead)] + _ffn_specs(layer)
                 + [_resident((1, D_MODEL), layer), _resident((D_MODEL, D_IN), layer),
                    table, table, table,
                    _resident((1, D_SGU), layer), _resident((1, D_SGU), layer),
                    _resident((N_SGU_GROUPS, CHUNK, CHUNK), layer),
                    _resident((N_SGU_GROUPS, CHUNK, 1), layer)],
        out_specs=[pl.BlockSpec((TM, D_MODEL), lead)] + views * 3 + [pl.BlockSpec((TM, D_SGU), lag)],
        out_shape=[jax.ShapeDtypeStruct((tokens, D_MODEL), F32)] + view_shapes * 3
                  + [jax.ShapeDtypeStruct((tokens, D_SGU), BF16)],
        scratch_shapes=[pltpu.VMEM((TM, D_FF), BF16), pltpu.VMEM((N_SLABS, TM, LANES), F32),
                        pltpu.VMEM((TM, D_IN), F32)],
        compiler_params=_params(1),
        name="ffn_inproj",
    )(x, *ffn_w, gain, w_in, *rope, lng, lnb, sw, sb)
    return (outs[0], outs[1:1 + N_PAT], outs[1 + N_PAT:1 + 2 * N_PAT],
            outs[1 + 2 * N_PAT:1 + 3 * N_PAT], outs[-1])


def _rope_tables(seq):
    half = ROPE_DIM // 2
    inv_freq = ROPE_THETA ** (-np.arange(0, ROPE_DIM, 2, dtype=np.float64) / ROPE_DIM)
    ang = np.arange(seq, dtype=np.float64)[:, None] * inv_freq[None, :]
    cos = np.ones((seq, LANES))
    sina = np.zeros((seq, LANES))
    sinb = np.zeros((seq, LANES))
    for head0 in range(0, LANES, HEAD_DIM):
        cos[:, head0:head0 + half] = np.cos(ang)
        cos[:, head0 + half:head0 + 2 * half] = np.cos(ang)
        sinb[:, head0:head0 + half] = -np.sin(ang)
        sina[:, head0 + half:head0 + 2 * half] = np.sin(ang)
    return tuple(jnp.asarray(t, F32) for t in (cos, sina, sinb))


def _attn_kernel(q_ref, kc_ref, kp_ref, kn_ref, vc_ref, vp_ref, vn_ref, o_ref, lse_ref,
                 kbuf, vbuf, *, sub_len, rows):
    kbuf[0:BAND] = kp_ref[...]
    kbuf[BAND:BAND + rows] = kc_ref[...]
    kbuf[BAND + rows:] = kn_ref[...]
    ones = jnp.ones((rows + 2 * BAND, LANES), BF16)
    for pair in range(N_SLABS):
        src = slice(pair * LANES, (pair + 1) * LANES)
        dst = slice(2 * pair * LANES, (2 * pair + 1) * LANES)
        vbuf[0:BAND, dst] = vp_ref[:, src]
        vbuf[BAND:BAND + rows, dst] = vc_ref[:, src]
        vbuf[BAND + rows:, dst] = vn_ref[:, src]
        vbuf[:, (2 * pair + 1) * LANES:(2 * pair + 2) * LANES] = ones

    first_row = (pl.program_id(0) % (sub_len // rows)) * rows
    delta = (lax.broadcasted_iota(jnp.int32, (QB, KW), 1)
             - lax.broadcasted_iota(jnp.int32, (QB, KW), 0))
    band_bias = jnp.where((delta >= 0) & (delta <= 2 * BAND), 0.0, NEG_INF)
    key_col = lax.broadcasted_iota(jnp.int32, (1, KW), 1)
    low_head = lax.broadcasted_iota(jnp.int32, (1, LANES), 1) < HEAD_DIM

    for j in range(rows // QB):
        row0 = j * QB
        key_pos = key_col + (first_row + row0 - BAND)
        bias = band_bias + jnp.where((key_pos >= 0) & (key_pos < sub_len), 0.0, NEG_INF)
        bias = jnp.concatenate([bias, bias], axis=0)
        for pair in range(N_SLABS):
            cols = slice(pair * LANES, (pair + 1) * LANES)
            q2 = q_ref[row0:row0 + QB, cols]
            zero = jnp.zeros_like(q2)
            qs = jnp.concatenate([jnp.where(low_head, q2, zero), jnp.where(low_head, zero, q2)], axis=0)
            s = lax.dot_general(qs, kbuf[row0:row0 + KW, cols], (((1,), (1,)), ((), ())),
                                preferred_element_type=F32) + bias
            m = jnp.max(s, axis=-1, keepdims=True)
            p = jnp.exp2(s - m).astype(BF16)
            pv = jnp.dot(p, vbuf[row0:row0 + KW, 2 * pair * LANES:(2 * pair + 2) * LANES],
                         preferred_element_type=F32)
            num = jnp.where(low_head, pv[:QB, :LANES], pv[QB:, :LANES])
            den = jnp.where(low_head, pv[:QB, LANES:], pv[QB:, LANES:])
            o_ref[row0:row0 + QB, cols] = (num / den).astype(BF16)
            lse_ref[row0:row0 + QB, cols] = (jnp.where(low_head, m[:QB], m[QB:]) * LN_2
                                             + jnp.log(den))


def _attention(q, k, v, dil, seq):
    total_rows = q.shape[0]
    sub_len = seq // dil
    rows = min(MAX_ATT_ROWS, sub_len)
    assert sub_len % rows == 0 and rows % QB == 0
    halos_per_step = rows // BAND
    last_halo = total_rows // BAND - 1
    cur = pl.BlockSpec((rows, D_ATTN), lambda i, r: (i, r))
    prev = pl.BlockSpec((BAND, D_ATTN), lambda i, r: (jnp.maximum(i * halos_per_step - 1, 0), r))
    nxt = pl.BlockSpec((BAND, D_ATTN),
                       lambda i, r: (jnp.minimum((i + 1) * halos_per_step, last_halo), r))
    return pl.pallas_call(
        functools.partial(_attn_kernel, sub_len=sub_len, rows=rows),
        grid=(total_rows // rows, dil),
        in_specs=[cur, cur, prev, nxt, cur, prev, nxt],
        out_specs=[cur, cur],
        out_shape=[jax.ShapeDtypeStruct(q.shape, BF16), jax.ShapeDtypeStruct(q.shape, F32)],
        scratch_shapes=[pltpu.VMEM((rows + 2 * BAND, D_ATTN), BF16),
                        pltpu.VMEM((rows + 2 * BAND, 2 * D_ATTN), BF16)],
        compiler_params=_params(2),
        name=f"attn_d{dil}",
    )(q, k, k, k, v, v, v)


def _merge_work(o_refs, l_refs, sg_ref, ga_ref, gs_ref, norm_ref, stage_refs):
    def unview_slab(s):
        def work():
            produced = []
            for p, dil in enumerate(DILATIONS[1:]):
                o_stage, l_stage = stage_refs[2 * p], stage_refs[2 * p + 1]
                for r in range(dil):
                    cols = slice(r * D_ATTN + s * LANES, r * D_ATTN + (s + 1) * LANES)
                    rows = pl.ds(r, TM // dil, stride=dil)
                    o_blk = o_refs[p + 1][:, cols].astype(F32)
                    l_blk = l_refs[p + 1][:, cols]
                    o_stage[s, rows, :] = o_blk
                    l_stage[s, rows, :] = l_blk
                    produced += [o_blk, l_blk]
            return produced
        return work

    def merge_rows(c):
        def work():
            rows = slice(c * CHUNK, (c + 1) * CHUNK)
            outs, lses = [o_refs[0][rows, :].astype(F32)], [l_refs[0][rows, :]]
            for p in range(N_PAT - 1):
                outs.append(jnp.concatenate([stage_refs[2 * p][s, rows, :] for s in range(N_SLABS)], axis=1))
                lses.append(jnp.concatenate([stage_refs[2 * p + 1][s, rows, :] for s in range(N_SLABS)], axis=1))
            m = functools.reduce(jnp.maximum, lses)
            es = [jnp.exp(l - m) for l in lses]
            a = sum(e * o for e, o in zip(es, outs)) / sum(es)
            na = _rms(a, ga_ref[...])
            ns = _rms(sg_ref[rows, :].astype(F32), gs_ref[...])
            norm_ref[rows, 0:D_ATTN] = na.astype(BF16)
            norm_ref[rows, D_ATTN:] = ns.astype(BF16)
            return [na, ns]
        return work

    return [unview_slab(s) for s in range(N_SLABS)] + [merge_rows(c) for c in range(TM // CHUNK)]


def _merge_ffn_kernel(x_ref, *refs, n_tiles, final):
    o_refs, l_refs = refs[0:2 * N_PAT:2], refs[1:2 * N_PAT:2]
    sg_ref, ga_ref, gs_ref, wo_ref = refs[2 * N_PAT:2 * N_PAT + 4]
    ffn_refs = refs[2 * N_PAT + 4:2 * N_PAT + 8]
    fgain_ref, out_ref, act_ref, norm_ref = refs[2 * N_PAT + 8:2 * N_PAT + 12]
    stage_refs = refs[2 * N_PAT + 12:]
    step = pl.program_id(0)
    merge = _merge_work(o_refs, l_refs, sg_ref, ga_ref, gs_ref, norm_ref, stage_refs)

    def project_and_ffn(side_work):
        x = x_ref[...] + jnp.dot(norm_ref[...], wo_ref[...], preferred_element_type=F32)
        y = _ffn_apply(x, *ffn_refs, act_ref, side_work=side_work)
        out_ref[...] = _rms(y, fgain_ref[...]) if final else y

    @pl.when(step == 0)
    def _():
        for work in merge:
            work()

    @pl.when((step > 0) & (step < n_tiles))
    def _():
        project_and_ffn(merge)

    @pl.when(step == n_tiles)
    def _():
        project_and_ffn(())


def _merge_ffn(x, attn, sg, ga, gs, wo, ffn_w, fgain, layer, final):
    tokens = x.shape[0]
    n_tiles = tokens // TM
    lead = lambda s: (jnp.minimum(s, n_tiles - 1), 0)
    lag = lambda s: (jnp.maximum(s - 1, 0), 0)
    views = [pl.BlockSpec((TM // d, d * D_ATTN), lead) for d in DILATIONS for _ in range(2)]
    return pl.pallas_call(
        functools.partial(_merge_ffn_kernel, n_tiles=n_tiles, final=final),
        grid=(n_tiles + 1,),
        in_specs=[pl.BlockSpec((TM, D_MODEL), lag)] + views + [pl.BlockSpec((TM, D_SGU), lead)]
                 + [_resident((1, D_ATTN), layer), _resident((1, D_SGU), layer),
                    _resident((D_ATTN + D_SGU, D_MODEL), layer)]
                 + _ffn_specs(layer) + [pl.BlockSpec((1, D_MODEL), lambda s: (0, 0))],
        out_specs=pl.BlockSpec((TM, D_MODEL), lag),
        out_shape=jax.ShapeDtypeStruct((tokens, D_MODEL), F32),
        scratch_shapes=[pltpu.VMEM((TM, D_FF), BF16), pltpu.VMEM((TM, D_ATTN + D_SGU), BF16)]
                       + [pltpu.VMEM((N_SLABS, TM, LANES), F32)] * (2 * (N_PAT - 1)),
        compiler_params=_params(1),
        name="merge_ffn",
    )(x, *attn, sg, ga, gs, wo, *ffn_w, fgain)


def kernel(x, norm_ffn1, ffn1_w_gate, ffn1_w_up, ffn1_w_down, norm_mix, w_in, sgu_ln_g, sgu_ln_b,
           sgu_w, sgu_b, out_norm_attn, out_norm_sgu, w_out, norm_ffn2, ffn2_w_gate, ffn2_w_up,
           ffn2_w_down, final_norm):
    batch, seq, _ = x.shape
    tokens = batch * seq
    assert x.shape[2] == D_MODEL and tokens % TM == 0 and seq % TM == 0 and TM % CHUNK == 0

    row = lambda p: p[:, None, :]
    bf = lambda w: w.astype(BF16)
    ffn1 = (row(norm_ffn1), bf(ffn1_w_gate), bf(ffn1_w_up), bf(ffn1_w_down))
    ffn2 = (row(norm_ffn2), bf(ffn2_w_gate), bf(ffn2_w_up), bf(ffn2_w_down))
    w_in_b, w_out_b, sgu_w_b = bf(w_in), bf(w_out), bf(sgu_w)
    nm = row(norm_mix)
    lng, lnb = row(sgu_ln_g), row(sgu_ln_b)
    ga, gs = row(out_norm_attn), row(out_norm_sgu)
    sgu_b_col = sgu_b[..., None]
    fgain = final_norm[None, :]
    rope = _rope_tables(seq)

    xt = x.reshape(tokens, D_MODEL)
    for layer in range(DEPTH):
        xt, qs, ks, vs, sg = _ffn_inproj(xt, ffn1, nm, w_in_b, rope, lng, lnb, sgu_w_b, sgu_b_col,
                                         layer, seq)
        attn = []
        for q, k, v, dil in zip(qs, ks, vs, DILATIONS):
            attn.extend(_attention(q, k, v, dil, seq))
        xt = _merge_ffn(xt, attn, sg, ga, gs, w_out_b, ffn2, fgain, layer, layer == DEPTH - 1)
    return xt.reshape(batch, seq, D_MODEL)
```

```python
import functools
import math

import numpy as np
import jax
import jax.numpy as jnp
from jax import lax
from jax.experimental import pallas as pl
from jax.experimental.pallas import tpu as pltpu

F32 = jnp.float32
BF16 = jnp.bfloat16

D_MODEL = 1024
DEPTH = 4
HEAD_DIM = 64
D_ATTN = 512
D_SGU = 512
N_SGU_GROUPS = 4
CHUNK = 128
D_IN = 3 * D_ATTN + 2 * D_SGU
D_FF = 2816
DILATIONS = (1, 4, 16)
N_PAT = len(DILATIONS)
BAND = 64
ROPE_THETA = 500000.0
ROPE_DIM = HEAD_DIM // 4
EPS = 1e-6
NEG_INF = -1e30
LOG2_E = math.log2(math.e)
LN_2 = math.log(2.0)

LANES = 128
SUBLANES = 8
V7X_VMEM_BYTES = 64 * 1024 * 1024
VMEM_LIMIT = V7X_VMEM_BYTES * 7 // 8

TM = 512
FF_CHUNK = 256
N_FF_CHUNKS = D_FF // FF_CHUNK
N_SLABS = D_ATTN // LANES
MAX_ATT_ROWS = 1024
QB = 2 * BAND
KW = QB + 2 * BAND


def _rms(x, gain):
    return x * lax.rsqrt(jnp.mean(x * x, axis=-1, keepdims=True) + EPS) * gain


def _params(n_axes):
    return pltpu.CompilerParams(dimension_semantics=("arbitrary",) * n_axes,
                                vmem_limit_bytes=VMEM_LIMIT)


def _resident(shape, layer):
    nd = len(shape)
    return pl.BlockSpec((None,) + shape, lambda *_: (layer,) + (0,) * nd,
                        pipeline_mode=pl.Buffered(1))


def _view_shape(tokens, dil, dtype):
    return jax.ShapeDtypeStruct((tokens // dil, dil * D_ATTN), dtype)


def _ordering_zero(values):
    acc = None
    for v in values:
        bits = pltpu.bitcast(v[0:SUBLANES, 0:LANES], jnp.uint32)
        acc = bits if acc is None else acc | bits
    return ((acc >> 16) >> 16)[0:1, :].astype(F32)


def _ffn_apply(x, gain_ref, wg_ref, wu_ref, wd_ref, act_ref, side_work=()):
    h = _rms(x, gain_ref[...]).astype(BF16)
    for c in range(N_FF_CHUNKS):
        cols = slice(c * FF_CHUNK, (c + 1) * FF_CHUNK)
        g = jnp.dot(h, wg_ref[:, cols], preferred_element_type=F32)
        u = jnp.dot(h, wu_ref[:, cols], preferred_element_type=F32)
        a = g * jax.nn.sigmoid(g) * u
        todo = side_work[c * len(side_work) // N_FF_CHUNKS:(c + 1) * len(side_work) // N_FF_CHUNKS]
        if todo:
            zero = _ordering_zero([v for work in todo for v in work()])
            a = a + jnp.concatenate([zero] * (FF_CHUNK // LANES), axis=1)
        act_ref[:, cols] = a.astype(BF16)
    return x + 0.5 * jnp.dot(act_ref[...], wd_ref[...], preferred_element_type=F32)


def _ffn_specs(layer):
    return [_resident((1, D_MODEL), layer), _resident((D_MODEL, D_FF), layer),
            _resident((D_MODEL, D_FF), layer), _resident((D_FF, D_MODEL), layer)]


def _gelu(x):
    return 0.5 * x * (1.0 + lax.erf(x * (1.0 / math.sqrt(2.0))))


def _inproj_matmul(x, gain_ref, w_ref, raw_ref):
    h = _rms(x, gain_ref[...]).astype(BF16)
    for col0 in range(0, D_IN, D_ATTN):
        raw_ref[:, col0:col0 + D_ATTN] = jnp.dot(h, w_ref[:, col0:col0 + D_ATTN],
                                                 preferred_element_type=F32)


def _inproj_finish_work(raw_ref, cos_ref, sina_ref, sinb_ref, lng_ref, lnb_ref, sw_ref, sb_ref,
                        q_refs, k_refs, v_refs, sg_ref, stage_ref):
    def qkv_slab(col0, dst_refs, rotary, scale, s):
        def work():
            t = raw_ref[:, col0 + s * LANES:col0 + (s + 1) * LANES]
            if rotary:
                half = ROPE_DIM // 2
                t = (t * cos_ref[...] + pltpu.roll(t, half, 1) * sina_ref[...]
                     + pltpu.roll(t, LANES - half, 1) * sinb_ref[...])
            if scale != 1.0:
                t = t * scale
            dst_refs[0][:, s * LANES:(s + 1) * LANES] = t.astype(BF16)
            stage_ref[s] = t
            produced = [t]
            for dil, dst in zip(DILATIONS[1:], dst_refs[1:]):
                for r in range(dil):
                    blk = stage_ref[s, pl.ds(r, TM // dil, stride=dil), :]
                    dst[:, r * D_ATTN + s * LANES:r * D_ATTN + (s + 1) * LANES] = blk.astype(BF16)
                    produced.append(blk)
            return produced
        return work

    def sgu_chunk(c):
        def work():
            rows = slice(c * CHUNK, (c + 1) * CHUNK)
            u = _gelu(raw_ref[rows, 3 * D_ATTN:3 * D_ATTN + D_SGU])
            g = _gelu(raw_ref[rows, 3 * D_ATTN + D_SGU:])
            gc = g - jnp.mean(g, axis=-1, keepdims=True)
            gn = (gc * lax.rsqrt(jnp.mean(gc * gc, axis=-1, keepdims=True) + EPS) * lng_ref[...]
                  + lnb_ref[...]).astype(BF16)
            group = D_SGU // N_SGU_GROUPS
            produced = []
            for gi in range(N_SGU_GROUPS):
                cols = slice(gi * group, (gi + 1) * group)
                mixed = jnp.dot(sw_ref[gi], gn[:, cols], preferred_element_type=F32) + sb_ref[gi]
                gated = u[:, cols] * mixed
                sg_ref[rows, cols] = gated.astype(BF16)
                produced.append(gated)
            return produced
        return work

    q_scale = LOG2_E / math.sqrt(HEAD_DIM)
    pieces = [qkv_slab(0, q_refs, True, q_scale, s) for s in range(N_SLABS)]
    pieces += [qkv_slab(D_ATTN, k_refs, True, 1.0, s) for s in range(N_SLABS)]
    pieces += [qkv_slab(2 * D_ATTN, v_refs, False, 1.0, s) for s in range(N_SLABS)]
    pieces += [sgu_chunk(c) for c in range(TM // CHUNK)]
    return pieces


def _ffn_inproj_kernel(x_ref, *refs, n_tiles):
    ffn_refs, gain_ref, w_ref, finish_refs = refs[:4], refs[4], refs[5], refs[6:13]
    xo_ref = refs[13]
    q_refs, k_refs, v_refs = (refs[14 + i * N_PAT:14 + (i + 1) * N_PAT] for i in range(3))
    sg_ref, act_ref, stage_ref, raw_ref = refs[14 + 3 * N_PAT:]
    step = pl.program_id(0)
    finish = _inproj_finish_work(raw_ref, *finish_refs, q_refs, k_refs, v_refs, sg_ref, stage_ref)

    @pl.when(step == 0)
    def _():
        raw_ref[...] = jnp.zeros_like(raw_ref)

    @pl.when(step < n_tiles)
    def _():
        x = _ffn_apply(x_ref[...], *ffn_refs, act_ref, side_work=finish)
        xo_ref[...] = x
        _inproj_matmul(x, gain_ref, w_ref, raw_ref)

    @pl.when(step == n_tiles)
    def _():
        for work in finish:
            work()


def _ffn_inproj(x, ffn_w, gain, w_in, rope, lng, lnb, sw, sb, layer, seq):
    tokens = x.shape[0]
    n_tiles = tokens // TM
    lead = lambda s: (jnp.minimum(s, n_tiles - 1), 0)
    lag = lambda s: (jnp.maximum(s - 1, 0), 0)
    table = pl.BlockSpec((TM, LANES), lambda s: (jnp.maximum(s - 1, 0) % (seq // TM), 0))
    views = [pl.BlockSpec((TM // d, d * D_ATTN), lag) for d in DILATIONS]
    view_shapes = [_view_shape(tokens, d, BF16) for d in DILATIONS]
    outs = pl.pallas_call(
        functools.partial(_ffn_inproj_kernel, n_tiles=n_tiles),
        grid=(n_tiles + 1,),
        in_specs=[pl.BlockSpec((TM, D_MODEL), lead)] + _ffn_specs(layer)
                 + [_resident((1, D_MODEL), layer), _resident((D_MODEL, D_IN), layer),
                    table, table, table,
                    _resident((1, D_SGU), layer), _resident((1, D_SGU), layer),
                    _resident((N_SGU_GROUPS, CHUNK, CHUNK), layer),
                    _resident((N_SGU_GROUPS, CHUNK, 1), layer)],
        out_specs=[pl.BlockSpec((TM, D_MODEL), lead)] + views * 3 + [pl.BlockSpec((TM, D_SGU), lag)],
        out_shape=[jax.ShapeDtypeStruct((tokens, D_MODEL), F32)] + view_shapes * 3
                  + [jax.ShapeDtypeStruct((tokens, D_SGU), BF16)],
        scratch_shapes=[pltpu.VMEM((TM, D_FF), BF16), pltpu.VMEM((N_SLABS, TM, LANES), F32),
                        pltpu.VMEM((TM, D_IN), F32)],
        compiler_params=_params(1),
        name="ffn_inproj",
    )(x, *ffn_w, gain, w_in, *rope, lng, lnb, sw, sb)
    return (outs[0], outs[1:1 + N_PAT], outs[1 + N_PAT:1 + 2 * N_PAT],
            outs[1 + 2 * N_PAT:1 + 3 * N_PAT], outs[-1])


def _rope_tables(seq):
    half = ROPE_DIM // 2
    inv_freq = ROPE_THETA ** (-np.arange(0, ROPE_DIM, 2, dtype=np.float64) / ROPE_DIM)
    ang = np.arange(seq, dtype=np.float64)[:, None] * inv_freq[None, :]
    cos = np.ones((seq, LANES))
    sina = np.zeros((seq, LANES))
    sinb = np.zeros((seq, LANES))
    for head0 in range(0, LANES, HEAD_DIM):
        cos[:, head0:head0 + half] = np.cos(ang)
        cos[:, head0 + half:head0 + 2 * half] = np.cos(ang)
        sinb[:, head0:head0 + half] = -np.sin(ang)
        sina[:, head0 + half:head0 + 2 * half] = np.sin(ang)
    return tuple(jnp.asarray(t, F32) for t in (cos, sina, sinb))


def _attn_kernel(q_ref, kc_ref, kp_ref, kn_ref, vc_ref, vp_ref, vn_ref, o_ref, lse_ref,
                 kbuf, vbuf, *, sub_len, rows, n_res):
    ones = jnp.ones((rows + 2 * BAND, LANES), BF16)
    for res in range(n_res):
        c0 = res * D_ATTN
        kbuf[res, 0:BAND] = kp_ref[:, c0:c0 + D_ATTN]
        kbuf[res, BAND:BAND + rows] = kc_ref[:, c0:c0 + D_ATTN]
        kbuf[res, BAND + rows:] = kn_ref[:, c0:c0 + D_ATTN]
        for pair in range(N_SLABS):
            src = slice(c0 + pair * LANES, c0 + (pair + 1) * LANES)
            dst = slice(2 * pair * LANES, (2 * pair + 1) * LANES)
            vbuf[res, 0:BAND, dst] = vp_ref[:, src]
            vbuf[res, BAND:BAND + rows, dst] = vc_ref[:, src]
            vbuf[res, BAND + rows:, dst] = vn_ref[:, src]
            vbuf[res, :, (2 * pair + 1) * LANES:(2 * pair + 2) * LANES] = ones

    first_row = (pl.program_id(0) % (sub_len // rows)) * rows
    delta = (lax.broadcasted_iota(jnp.int32, (QB, KW), 1)
             - lax.broadcasted_iota(jnp.int32, (QB, KW), 0))
    band_bias = jnp.where((delta >= 0) & (delta <= 2 * BAND), 0.0, NEG_INF)
    key_col = lax.broadcasted_iota(jnp.int32, (1, KW), 1)
    lane = lax.broadcasted_iota(jnp.int32, (1, LANES), 1)
    low_head = lane < HEAD_DIM

    for j in range(rows // QB):
        row0 = j * QB
        key_pos = key_col + (first_row + row0 - BAND)
        bias = band_bias + jnp.where((key_pos >= 0) & (key_pos < sub_len), 0.0, NEG_INF)
        bias = jnp.concatenate([bias, bias], axis=0)
        for res in range(n_res):
            lse_tile = jnp.zeros((QB, LANES), F32)
            for pair in range(N_SLABS):
                cols = slice(res * D_ATTN + pair * LANES, res * D_ATTN + (pair + 1) * LANES)
                q2 = q_ref[row0:row0 + QB, cols]
                zero = jnp.zeros_like(q2)
                qs = jnp.concatenate([jnp.where(low_head, q2, zero), jnp.where(low_head, zero, q2)],
                                     axis=0)
                k2 = kbuf[res, row0:row0 + KW, pair * LANES:(pair + 1) * LANES]
                s = lax.dot_general(qs, k2, (((1,), (1,)), ((), ())),
                                    preferred_element_type=F32) + bias
                m = jnp.max(s, axis=-1, keepdims=True)
                p = jnp.exp2(s - m).astype(BF16)
                pv = jnp.dot(p, vbuf[res, row0:row0 + KW, 2 * pair * LANES:(2 * pair + 2) * LANES],
                             preferred_element_type=F32)
                num = jnp.where(low_head, pv[:QB, :LANES], pv[QB:, :LANES])
                den = jnp.where(low_head, pv[:QB, LANES:], pv[QB:, LANES:])
                o_ref[row0:row0 + QB, cols] = (num / den).astype(BF16)
                lse_pair = jnp.where(low_head, m[:QB], m[QB:]) * LN_2 + jnp.log(den)
                lse_tile = jnp.where((lane == pair) | (lane == HEAD_DIM + pair), lse_pair, lse_tile)
            lse_ref[row0:row0 + QB, res * LANES:(res + 1) * LANES] = lse_tile


def _attention(q, k, v, dil, seq):
    total_rows = q.shape[0]
    sub_len = seq // dil
    rows = min(MAX_ATT_ROWS, sub_len)
    n_res = min(MAX_ATT_ROWS // rows, dil)
    assert sub_len % rows == 0 and rows % QB == 0 and dil % n_res == 0
    halos_per_step = rows // BAND
    last_halo = total_rows // BAND - 1
    width = n_res * D_ATTN
    cur = pl.BlockSpec((rows, width), lambda i, r: (i, r))
    prev = pl.BlockSpec((BAND, width), lambda i, r: (jnp.maximum(i * halos_per_step - 1, 0), r))
    nxt = pl.BlockSpec((BAND, width),
                       lambda i, r: (jnp.minimum((i + 1) * halos_per_step, last_halo), r))
    return pl.pallas_call(
        functools.partial(_attn_kernel, sub_len=sub_len, rows=rows, n_res=n_res),
        grid=(total_rows // rows, dil // n_res),
        in_specs=[cur, cur, prev, nxt, cur, prev, nxt],
        out_specs=[cur, pl.BlockSpec((rows, n_res * LANES), lambda i, r: (i, r))],
        out_shape=[jax.ShapeDtypeStruct(q.shape, BF16),
                   jax.ShapeDtypeStruct((total_rows, dil * LANES), F32)],
        scratch_shapes=[pltpu.VMEM((n_res, rows + 2 * BAND, D_ATTN), BF16),
                        pltpu.VMEM((n_res, rows + 2 * BAND, 2 * D_ATTN), BF16)],
        compiler_params=_params(2),
        name=f"attn_d{dil}",
    )(q, k, k, k, v, v, v)


def _merge_work(o_refs, l_refs, sg_ref, ga_ref, gs_ref, norm_ref, stage_refs):
    def unview_slab(s):
        def work():
            produced = []
            for p, dil in enumerate(DILATIONS[1:]):
                for r in range(dil):
                    blk = o_refs[p + 1][:, r * D_ATTN + s * LANES:r * D_ATTN + (s + 1) * LANES]
                    blk = blk.astype(F32)
                    stage_refs[2 * p][s, pl.ds(r, TM // dil, stride=dil), :] = blk
                    produced.append(blk)
            return produced
        return work

    def unview_lse():
        produced = []
        for p, dil in enumerate(DILATIONS[1:]):
            for r in range(dil):
                blk = l_refs[p + 1][:, r * LANES:(r + 1) * LANES]
                stage_refs[2 * p + 1][pl.ds(r, TM // dil, stride=dil), :] = blk
                produced.append(blk)
        return produced

    def merge_rows(c):
        def work():
            rows = slice(c * CHUNK, (c + 1) * CHUNK)
            low_head = lax.broadcasted_iota(jnp.int32, (1, LANES), 1) < HEAD_DIM
            lses = [l_refs[0][rows, :]] + [stage_refs[2 * p + 1][rows, :] for p in range(N_PAT - 1)]
            m = functools.reduce(jnp.maximum, lses)
            es = [jnp.exp(l - m) for l in lses]
            inv = 1.0 / sum(es)
            slabs = []
            for s in range(N_SLABS):
                cols = slice(s * LANES, (s + 1) * LANES)
                outs = [o_refs[0][rows, cols].astype(F32)]
                outs += [stage_refs[2 * p][s, rows, :] for p in range(N_PAT - 1)]
                acc = None
                for e, o in zip(es, outs):
                    w = e * inv
                    w = jnp.where(low_head, w[:, s:s + 1], w[:, HEAD_DIM + s:HEAD_DIM + s + 1])
                    acc = w * o if acc is None else acc + w * o
                slabs.append(acc)
            na = _rms(jnp.concatenate(slabs, axis=1), ga_ref[...])
            ns = _rms(sg_ref[rows, :].astype(F32), gs_ref[...])
            norm_ref[rows, 0:D_ATTN] = na.astype(BF16)
            norm_ref[rows, D_ATTN:] = ns.astype(BF16)
            return [na, ns]
        return work

    return ([unview_slab(s) for s in range(N_SLABS)] + [unview_lse]
            + [merge_rows(c) for c in range(TM // CHUNK)])


def _merge_ffn_kernel(x_ref, *refs, n_tiles, final):
    o_refs, l_refs = refs[0:2 * N_PAT:2], refs[1:2 * N_PAT:2]
    sg_ref, ga_ref, gs_ref, wo_ref = refs[2 * N_PAT:2 * N_PAT + 4]
    ffn_refs = refs[2 * N_PAT + 4:2 * N_PAT + 8]
    fgain_ref, out_ref, act_ref, norm_ref = refs[2 * N_PAT + 8:2 * N_PAT + 12]
    stage_refs = refs[2 * N_PAT + 12:]
    step = pl.program_id(0)
    merge = _merge_work(o_refs, l_refs, sg_ref, ga_ref, gs_ref, norm_ref, stage_refs)

    def project_and_ffn(side_work):
        x = x_ref[...] + jnp.dot(norm_ref[...], wo_ref[...], preferred_element_type=F32)
        y = _ffn_apply(x, *ffn_refs, act_ref, side_work=side_work)
        out_ref[...] = _rms(y, fgain_ref[...]) if final else y

    @pl.when(step == 0)
    def _():
        for work in merge:
            work()

    @pl.when((step > 0) & (step < n_tiles))
    def _():
        project_and_ffn(merge)

    @pl.when(step == n_tiles)
    def _():
        project_and_ffn(())


def _merge_ffn(x, attn, sg, ga, gs, wo, ffn_w, fgain, layer, final):
    tokens = x.shape[0]
    n_tiles = tokens // TM
    lead = lambda s: (jnp.minimum(s, n_tiles - 1), 0)
    lag = lambda s: (jnp.maximum(s - 1, 0), 0)
    views = [pl.BlockSpec((TM // d, d * width), lead) for d in DILATIONS for width in (D_ATTN, LANES)]
    return pl.pallas_call(
        functools.partial(_merge_ffn_kernel, n_tiles=n_tiles, final=final),
        grid=(n_tiles + 1,),
        in_specs=[pl.BlockSpec((TM, D_MODEL), lag)] + views + [pl.BlockSpec((TM, D_SGU), lead)]
                 + [_resident((1, D_ATTN), layer), _resident((1, D_SGU), layer),
                    _resident((D_ATTN + D_SGU, D_MODEL), layer)]
                 + _ffn_specs(layer) + [pl.BlockSpec((1, D_MODEL), lambda s: (0, 0))],
        out_specs=pl.BlockSpec((TM, D_MODEL), lag),
        out_shape=jax.ShapeDtypeStruct((tokens, D_MODEL), F32),
        scratch_shapes=[pltpu.VMEM((TM, D_FF), BF16), pltpu.VMEM((TM, D_ATTN + D_SGU), BF16)]
                       + [pltpu.VMEM((N_SLABS, TM, LANES), F32), pltpu.VMEM((TM, LANES), F32)] * (N_PAT - 1),
        compiler_params=_params(1),
        name="merge_ffn",
    )(x, *attn, sg, ga, gs, wo, *ffn_w, fgain)


def kernel(x, norm_ffn1, ffn1_w_gate, ffn1_w_up, ffn1_w_down, norm_mix, w_in, sgu_ln_g, sgu_ln_b,
           sgu_w, sgu_b, out_norm_attn, out_norm_sgu, w_out, norm_ffn2, ffn2_w_gate, ffn2_w_up,
           ffn2_w_down, final_norm):
    batch, seq, _ = x.shape
    tokens = batch * seq
    assert x.shape[2] == D_MODEL and tokens % TM == 0 and seq % TM == 0 and TM % CHUNK == 0

    row = lambda p: p[:, None, :]
    bf = lambda w: w.astype(BF16)
    ffn1 = (row(norm_ffn1), bf(ffn1_w_gate), bf(ffn1_w_up), bf(ffn1_w_down))
    ffn2 = (row(norm_ffn2), bf(ffn2_w_gate), bf(ffn2_w_up), bf(ffn2_w_down))
    w_in_b, w_out_b, sgu_w_b = bf(w_in), bf(w_out), bf(sgu_w)
    nm = row(norm_mix)
    lng, lnb = row(sgu_ln_g), row(sgu_ln_b)
    ga, gs = row(out_norm_attn), row(out_norm_sgu)
    sgu_b_col = sgu_b[..., None]
    fgain = final_norm[None, :]
    rope = _rope_tables(seq)

    xt = x.reshape(tokens, D_MODEL)
    for layer in range(DEPTH):
        xt, qs, ks, vs, sg = _ffn_inproj(xt, ffn1, nm, w_in_b, rope, lng, lnb, sgu_w_b, sgu_b_col,
                                         layer, seq)
        attn = []
        for q, k, v, dil in zip(qs, ks, vs, DILATIONS):
            attn.extend(_attention(q, k, v, dil, seq))
        xt = _merge_ffn(xt, attn, sg, ga, gs, w_out_b, ffn2, fgain, layer, layer == DEPTH - 1)
    return xt.reshape(batch, seq, D_MODEL)
```

```python
import functools
import math

import numpy as np
import jax
import jax.numpy as jnp
from jax import lax
from jax.experimental import pallas as pl
from jax.experimental.pallas import tpu as pltpu

F32 = jnp.float32
BF16 = jnp.bfloat16

D_MODEL = 1024
DEPTH = 4
HEAD_DIM = 64
D_ATTN = 512
D_SGU = 512
N_SGU_GROUPS = 4
CHUNK = 128
D_IN = 3 * D_ATTN + 2 * D_SGU
D_FF = 2816
DILATIONS = (1, 4, 16)
N_PAT = len(DILATIONS)
BAND = 64
ROPE_THETA = 500000.0
ROPE_DIM = HEAD_DIM // 4
EPS = 1e-6
NEG_INF = -1e30
LOG2_E = math.log2(math.e)
LN_2 = math.log(2.0)

LANES = 128
SUBLANES = 8
V7X_VMEM_BYTES = 64 * 1024 * 1024
VMEM_LIMIT = V7X_VMEM_BYTES * 7 // 8

TM = 512
FF_CHUNK = 256
N_FF_CHUNKS = D_FF // FF_CHUNK
N_SLABS = D_ATTN // LANES
MAX_ATT_ROWS = 2048
QB = 2 * BAND
KW = QB + 2 * BAND


def _rms(x, gain):
    return x * lax.rsqrt(jnp.mean(x * x, axis=-1, keepdims=True) + EPS) * gain


def _params(n_axes):
    return pltpu.CompilerParams(dimension_semantics=("arbitrary",) * n_axes,
                                vmem_limit_bytes=VMEM_LIMIT)


def _resident(shape, layer):
    nd = len(shape)
    return pl.BlockSpec((None,) + shape, lambda *_: (layer,) + (0,) * nd,
                        pipeline_mode=pl.Buffered(1))


def _view_shape(tokens, dil, dtype):
    return jax.ShapeDtypeStruct((tokens // dil, dil * D_ATTN), dtype)


def _ordering_zero(values):
    acc = None
    for v in values:
        bits = pltpu.bitcast(v[0:SUBLANES, 0:LANES], jnp.uint32)
        acc = bits if acc is None else acc | bits
    return ((acc >> 16) >> 16)[0:1, :].astype(F32)


def _ffn_apply(x, gain_ref, wg_ref, wu_ref, wd_ref, act_ref, side_work=()):
    h = _rms(x, gain_ref[...]).astype(BF16)
    for c in range(N_FF_CHUNKS):
        cols = slice(c * FF_CHUNK, (c + 1) * FF_CHUNK)
        g = jnp.dot(h, wg_ref[:, cols], preferred_element_type=F32)
        u = jnp.dot(h, wu_ref[:, cols], preferred_element_type=F32)
        a = g * jax.nn.sigmoid(g) * u
        todo = side_work[c * len(side_work) // N_FF_CHUNKS:(c + 1) * len(side_work) // N_FF_CHUNKS]
        if todo:
            zero = _ordering_zero([v for work in todo for v in work()])
            a = a + jnp.concatenate([zero] * (FF_CHUNK // LANES), axis=1)
        act_ref[:, cols] = a.astype(BF16)
    return x + 0.5 * jnp.dot(act_ref[...], wd_ref[...], preferred_element_type=F32)


def _ffn_specs(layer):
    return [_resident((1, D_MODEL), layer), _resident((D_MODEL, D_FF), layer),
            _resident((D_MODEL, D_FF), layer), _resident((D_FF, D_MODEL), layer)]


def _gelu(x):
    return 0.5 * x * (1.0 + lax.erf(x * (1.0 / math.sqrt(2.0))))


def _inproj_matmul(x, gain_ref, w_ref, raw_ref):
    h = _rms(x, gain_ref[...]).astype(BF16)
    for col0 in range(0, D_IN, D_ATTN):
        raw_ref[:, col0:col0 + D_ATTN] = jnp.dot(h, w_ref[:, col0:col0 + D_ATTN],
                                                 preferred_element_type=F32)


def _inproj_finish_work(raw_ref, cos_ref, sina_ref, sinb_ref, lng_ref, lnb_ref, sw_ref, sb_ref,
                        q_refs, k_refs, v_refs, sg_ref, stage_ref):
    def qkv_slab(col0, dst_refs, rotary, scale, s):
        def work():
            t = raw_ref[:, col0 + s * LANES:col0 + (s + 1) * LANES]
            if rotary:
                half = ROPE_DIM // 2
                t = (t * cos_ref[...] + pltpu.roll(t, half, 1) * sina_ref[...]
                     + pltpu.roll(t, LANES - half, 1) * sinb_ref[...])
            if scale != 1.0:
                t = t * scale
            dst_refs[0][:, s * LANES:(s + 1) * LANES] = t.astype(BF16)
            stage_ref[s] = t
            produced = [t]
            for dil, dst in zip(DILATIONS[1:], dst_refs[1:]):
                for r in range(dil):
                    blk = stage_ref[s, pl.ds(r, TM // dil, stride=dil), :]
                    dst[:, r * D_ATTN + s * LANES:r * D_ATTN + (s + 1) * LANES] = blk.astype(BF16)
                    produced.append(blk)
            return produced
        return work

    def sgu_chunk(c):
        def work():
            rows = slice(c * CHUNK, (c + 1) * CHUNK)
            u = _gelu(raw_ref[rows, 3 * D_ATTN:3 * D_ATTN + D_SGU])
            g = _gelu(raw_ref[rows, 3 * D_ATTN + D_SGU:])
            gc = g - jnp.mean(g, axis=-1, keepdims=True)
            gn = (gc * lax.rsqrt(jnp.mean(gc * gc, axis=-1, keepdims=True) + EPS) * lng_ref[...]
                  + lnb_ref[...]).astype(BF16)
            group = D_SGU // N_SGU_GROUPS
            produced = []
            for gi in range(N_SGU_GROUPS):
                cols = slice(gi * group, (gi + 1) * group)
                mixed = jnp.dot(sw_ref[gi], gn[:, cols], preferred_element_type=F32) + sb_ref[gi]
                gated = u[:, cols] * mixed
                sg_ref[rows, cols] = gated.astype(BF16)
                produced.append(gated)
            return produced
        return work

    q_scale = LOG2_E / math.sqrt(HEAD_DIM)
    pieces = [qkv_slab(0, q_refs, True, q_scale, s) for s in range(N_SLABS)]
    pieces += [qkv_slab(D_ATTN, k_refs, True, 1.0, s) for s in range(N_SLABS)]
    pieces += [qkv_slab(2 * D_ATTN, v_refs, False, 1.0, s) for s in range(N_SLABS)]
    pieces += [sgu_chunk(c) for c in range(TM // CHUNK)]
    return pieces


def _ffn_inproj_kernel(x_ref, *refs, n_tiles):
    ffn_refs, gain_ref, w_ref, finish_refs = refs[:4], refs[4], refs[5], refs[6:13]
    xo_ref = refs[13]
    q_refs, k_refs, v_refs = (refs[14 + i * N_PAT:14 + (i + 1) * N_PAT] for i in range(3))
    sg_ref, act_ref, stage_ref, raw_ref = refs[14 + 3 * N_PAT:]
    step = pl.program_id(0)
    finish = _inproj_finish_work(raw_ref, *finish_refs, q_refs, k_refs, v_refs, sg_ref, stage_ref)

    @pl.when(step == 0)
    def _():
        raw_ref[...] = jnp.zeros_like(raw_ref)

    @pl.when(step < n_tiles)
    def _():
        x = _ffn_apply(x_ref[...], *ffn_refs, act_ref, side_work=finish)
        xo_ref[...] = x
        _inproj_matmul(x, gain_ref, w_ref, raw_ref)

    @pl.when(step == n_tiles)
    def _():
        for work in finish:
            work()


def _ffn_inproj(x, ffn_w, gain, w_in, rope, lng, lnb, sw, sb, layer, seq):
    tokens = x.shape[0]
    n_tiles = tokens // TM
    lead = lambda s: (jnp.minimum(s, n_tiles - 1), 0)
    lag = lambda s: (jnp.maximum(s - 1, 0), 0)
    table = pl.BlockSpec((TM, LANES), lambda s: (jnp.maximum(s - 1, 0) % (seq // TM), 0))
    views = [pl.BlockSpec((TM // d, d * D_ATTN), lag) for d in DILATIONS]
    view_shapes = [_view_shape(tokens, d, BF16) for d in DILATIONS]
    outs = pl.pallas_call(
        functools.partial(_ffn_inproj_kernel, n_tiles=n_tiles),
        grid=(n_tiles + 1,),
        in_specs=[pl.BlockSpec((TM, D_MODEL), lead)] + _ffn_specs(layer)
                 + [_resident((1, D_MODEL), layer), _resident((D_MODEL, D_IN), layer),
                    table, table, table,
                    _resident((1, D_SGU), layer), _resident((1, D_SGU), layer),
                    _resident((N_SGU_GROUPS, CHUNK, CHUNK), layer),
                    _resident((N_SGU_GROUPS, CHUNK, 1), layer)],
        out_specs=[pl.BlockSpec((TM, D_MODEL), lead)] + views * 3 + [pl.BlockSpec((TM, D_SGU), lag)],
        out_shape=[jax.ShapeDtypeStruct((tokens, D_MODEL), F32)] + view_shapes * 3
                  + [jax.ShapeDtypeStruct((tokens, D_SGU), BF16)],
        scratch_shapes=[pltpu.VMEM((TM, D_FF), BF16), pltpu.VMEM((N_SLABS, TM, LANES), F32),
                        pltpu.VMEM((TM, D_IN), F32)],
        compiler_params=_params(1),
        name="ffn_inproj",
    )(x, *ffn_w, gain, w_in, *rope, lng, lnb, sw, sb)
    return (outs[0], outs[1:1 + N_PAT], outs[1 + N_PAT:1 + 2 * N_PAT],
            outs[1 + 2 * N_PAT:1 + 3 * N_PAT], outs[-1])


def _rope_tables(seq):
    half = ROPE_DIM // 2
    inv_freq = ROPE_THETA ** (-np.arange(0, ROPE_DIM, 2, dtype=np.float64) / ROPE_DIM)
    ang = np.arange(seq, dtype=np.float64)[:, None] * inv_freq[None, :]
    cos = np.ones((seq, LANES))
    sina = np.zeros((seq, LANES))
    sinb = np.zeros((seq, LANES))
    for head0 in range(0, LANES, HEAD_DIM):
        cos[:, head0:head0 + half] = np.cos(ang)
        cos[:, head0 + half:head0 + 2 * half] = np.cos(ang)
        sinb[:, head0:head0 + half] = -np.sin(ang)
        sina[:, head0 + half:head0 + 2 * half] = np.sin(ang)
    return tuple(jnp.asarray(t, F32) for t in (cos, sina, sinb))


def _attn_kernel(q_ref, kc_ref, kp_ref, kn_ref, vc_ref, vp_ref, vn_ref, o_ref, lse_ref,
                 kbuf, vbuf, *, sub_len, rows, n_res):
    ones = jnp.ones((rows + 2 * BAND, LANES), BF16)
    for res in range(n_res):
        c0 = res * D_ATTN
        kbuf[res, 0:BAND] = kp_ref[:, c0:c0 + D_ATTN]
        kbuf[res, BAND:BAND + rows] = kc_ref[:, c0:c0 + D_ATTN]
        kbuf[res, BAND + rows:] = kn_ref[:, c0:c0 + D_ATTN]
        for pair in range(N_SLABS):
            src = slice(c0 + pair * LANES, c0 + (pair + 1) * LANES)
            dst = slice(2 * pair * LANES, (2 * pair + 1) * LANES)
            vbuf[res, 0:BAND, dst] = vp_ref[:, src]
            vbuf[res, BAND:BAND + rows, dst] = vc_ref[:, src]
            vbuf[res, BAND + rows:, dst] = vn_ref[:, src]
            vbuf[res, :, (2 * pair + 1) * LANES:(2 * pair + 2) * LANES] = ones

    first_row = (pl.program_id(0) % (sub_len // rows)) * rows
    delta = (lax.broadcasted_iota(jnp.int32, (QB, KW), 1)
             - lax.broadcasted_iota(jnp.int32, (QB, KW), 0))
    band_bias = jnp.where((delta >= 0) & (delta <= 2 * BAND), 0.0, NEG_INF)
    key_col = lax.broadcasted_iota(jnp.int32, (1, KW), 1)
    lane = lax.broadcasted_iota(jnp.int32, (1, LANES), 1)
    low_head = lane < HEAD_DIM

    for j in range(rows // QB):
        row0 = j * QB
        key_pos = key_col + (first_row + row0 - BAND)
        bias = band_bias + jnp.where((key_pos >= 0) & (key_pos < sub_len), 0.0, NEG_INF)
        bias = jnp.concatenate([bias, bias], axis=0)
        for res in range(n_res):
            lse_tile = jnp.zeros((QB, LANES), F32)
            for pair in range(N_SLABS):
                cols = slice(res * D_ATTN + pair * LANES, res * D_ATTN + (pair + 1) * LANES)
                q2 = q_ref[row0:row0 + QB, cols]
                zero = jnp.zeros_like(q2)
                qs = jnp.concatenate([jnp.where(low_head, q2, zero), jnp.where(low_head, zero, q2)],
                                     axis=0)
                k2 = kbuf[res, row0:row0 + KW, pair * LANES:(pair + 1) * LANES]
                s = lax.dot_general(qs, k2, (((1,), (1,)), ((), ())),
                                    preferred_element_type=F32) + bias
                m = jnp.max(s, axis=-1, keepdims=True)
                p = jnp.exp2(s - m).astype(BF16)
                pv = jnp.dot(p, vbuf[res, row0:row0 + KW, 2 * pair * LANES:(2 * pair + 2) * LANES],
                             preferred_element_type=F32)
                num = jnp.where(low_head, pv[:QB, :LANES], pv[QB:, :LANES])
                den = jnp.where(low_head, pv[:QB, LANES:], pv[QB:, LANES:])
                o_ref[row0:row0 + QB, cols] = (num / den).astype(BF16)
                lse_pair = jnp.where(low_head, m[:QB], m[QB:]) * LN_2 + jnp.log(den)
                lse_tile = jnp.where((lane == pair) | (lane == HEAD_DIM + pair), lse_pair, lse_tile)
            lse_ref[row0:row0 + QB, res * LANES:(res + 1) * LANES] = lse_tile


def _attention(q, k, v, dil, seq):
    total_rows = q.shape[0]
    sub_len = seq // dil
    rows = min(MAX_ATT_ROWS, sub_len)
    n_res = min(MAX_ATT_ROWS // rows, dil)
    assert sub_len % rows == 0 and rows % QB == 0 and dil % n_res == 0
    halos_per_step = rows // BAND
    last_halo = total_rows // BAND - 1
    width = n_res * D_ATTN
    cur = pl.BlockSpec((rows, width), lambda i, r: (i, r))
    prev = pl.BlockSpec((BAND, width), lambda i, r: (jnp.maximum(i * halos_per_step - 1, 0), r))
    nxt = pl.BlockSpec((BAND, width),
                       lambda i, r: (jnp.minimum((i + 1) * halos_per_step, last_halo), r))
    return pl.pallas_call(
        functools.partial(_attn_kernel, sub_len=sub_len, rows=rows, n_res=n_res),
        grid=(total_rows // rows, dil // n_res),
        in_specs=[cur, cur, prev, nxt, cur, prev, nxt],
        out_specs=[cur, pl.BlockSpec((rows, n_res * LANES), lambda i, r: (i, r))],
        out_shape=[jax.ShapeDtypeStruct(q.shape, BF16),
                   jax.ShapeDtypeStruct((total_rows, dil * LANES), F32)],
        scratch_shapes=[pltpu.VMEM((n_res, rows + 2 * BAND, D_ATTN), BF16),
                        pltpu.VMEM((n_res, rows + 2 * BAND, 2 * D_ATTN), BF16)],
        compiler_params=_params(2),
        name=f"attn_d{dil}",
    )(q, k, k, k, v, v, v)


def _merge_work(o_refs, l_refs, sg_ref, ga_ref, gs_ref, norm_ref, stage_refs):
    def unview_slab(s):
        def work():
            produced = []
            for p, dil in enumerate(DILATIONS[1:]):
                for r in range(dil):
                    blk = o_refs[p + 1][:, r * D_ATTN + s * LANES:r * D_ATTN + (s + 1) * LANES]
                    blk = blk.astype(F32)
                    stage_refs[2 * p][s, pl.ds(r, TM // dil, stride=dil), :] = blk
                    produced.append(blk)
            return produced
        return work

    def unview_lse():
        produced = []
        for p, dil in enumerate(DILATIONS[1:]):
            for r in range(dil):
                blk = l_refs[p + 1][:, r * LANES:(r + 1) * LANES]
                stage_refs[2 * p + 1][pl.ds(r, TM // dil, stride=dil), :] = blk
                produced.append(blk)
        return produced

    def merge_rows(c):
        def work():
            rows = slice(c * CHUNK, (c + 1) * CHUNK)
            low_head = lax.broadcasted_iota(jnp.int32, (1, LANES), 1) < HEAD_DIM
            lses = [l_refs[0][rows, :]] + [stage_refs[2 * p + 1][rows, :] for p in range(N_PAT - 1)]
            m = functools.reduce(jnp.maximum, lses)
            es = [jnp.exp(l - m) for l in lses]
            inv = 1.0 / sum(es)
            slabs = []
            for s in range(N_SLABS):
                cols = slice(s * LANES, (s + 1) * LANES)
                outs = [o_refs[0][rows, cols].astype(F32)]
                outs += [stage_refs[2 * p][s, rows, :] for p in range(N_PAT - 1)]
                acc = None
                for e, o in zip(es, outs):
                    w = e * inv
                    w = jnp.where(low_head, w[:, s:s + 1], w[:, HEAD_DIM + s:HEAD_DIM + s + 1])
                    acc = w * o if acc is None else acc + w * o
                slabs.append(acc)
            na = _rms(jnp.concatenate(slabs, axis=1), ga_ref[...])
            ns = _rms(sg_ref[rows, :].astype(F32), gs_ref[...])
            norm_ref[rows, 0:D_ATTN] = na.astype(BF16)
            norm_ref[rows, D_ATTN:] = ns.astype(BF16)
            return [na, ns]
        return work

    return ([unview_slab(s) for s in range(N_SLABS)] + [unview_lse]
            + [merge_rows(c) for c in range(TM // CHUNK)])


def _merge_ffn_kernel(x_ref, *refs, n_tiles, final):
    o_refs, l_refs = refs[0:2 * N_PAT:2], refs[1:2 * N_PAT:2]
    sg_ref, ga_ref, gs_ref, wo_ref = refs[2 * N_PAT:2 * N_PAT + 4]
    ffn_refs = refs[2 * N_PAT + 4:2 * N_PAT + 8]
    fgain_ref, out_ref, act_ref, norm_ref = refs[2 * N_PAT + 8:2 * N_PAT + 12]
    stage_refs = refs[2 * N_PAT + 12:]
    step = pl.program_id(0)
    merge = _merge_work(o_refs, l_refs, sg_ref, ga_ref, gs_ref, norm_ref, stage_refs)

    def project_and_ffn(side_work):
        x = x_ref[...] + jnp.dot(norm_ref[...], wo_ref[...], preferred_element_type=F32)
        y = _ffn_apply(x, *ffn_refs, act_ref, side_work=side_work)
        out_ref[...] = _rms(y, fgain_ref[...]) if final else y

    @pl.when(step == 0)
    def _():
        for work in merge:
            work()

    @pl.when((step > 0) & (step < n_tiles))
    def _():
        project_and_ffn(merge)

    @pl.when(step == n_tiles)
    def _():
        project_and_ffn(())


def _merge_ffn(x, attn, sg, ga, gs, wo, ffn_w, fgain, layer, final):
    tokens = x.shape[0]
    n_tiles = tokens // TM
    lead = lambda s: (jnp.minimum(s, n_tiles - 1), 0)
    lag = lambda s: (jnp.maximum(s - 1, 0), 0)
    views = [pl.BlockSpec((TM // d, d * width), lead) for d in DILATIONS for width in (D_ATTN, LANES)]
    return pl.pallas_call(
        functools.partial(_merge_ffn_kernel, n_tiles=n_tiles, final=final),
        grid=(n_tiles + 1,),
        in_specs=[pl.BlockSpec((TM, D_MODEL), lag)] + views + [pl.BlockSpec((TM, D_SGU), lead)]
                 + [_resident((1, D_ATTN), layer), _resident((1, D_SGU), layer),
                    _resident((D_ATTN + D_SGU, D_MODEL), layer)]
                 + _ffn_specs(layer) + [pl.BlockSpec((1, D_MODEL), lambda s: (0, 0))],
        out_specs=pl.BlockSpec((TM, D_MODEL), lag),
        out_shape=jax.ShapeDtypeStruct((tokens, D_MODEL), F32),
        scratch_shapes=[pltpu.VMEM((TM, D_FF), BF16), pltpu.VMEM((TM, D_ATTN + D_SGU), BF16)]
                       + [pltpu.VMEM((N_SLABS, TM, LANES), F32), pltpu.VMEM((TM, LANES), F32)] * (N_PAT - 1),
        compiler_params=_params(1),
        name="merge_ffn",
    )(x, *attn, sg, ga, gs, wo, *ffn_w, fgain)


def kernel(x, norm_ffn1, ffn1_w_gate, ffn1_w_up, ffn1_w_down, norm_mix, w_in, sgu_ln_g, sgu_ln_b,
           sgu_w, sgu_b, out_norm_attn, out_norm_sgu, w_out, norm_ffn2, ffn2_w_gate, ffn2_w_up,
           ffn2_w_down, final_norm):
    batch, seq, _ = x.shape
    tokens = batch * seq
    assert x.shape[2] == D_MODEL and tokens % TM == 0 and seq % TM == 0 and TM % CHUNK == 0

    row = lambda p: p[:, None, :]
    bf = lambda w: w.astype(BF16)
    ffn1 = (row(norm_ffn1), bf(ffn1_w_gate), bf(ffn1_w_up), bf(ffn1_w_down))
    ffn2 = (row(norm_ffn2), bf(ffn2_w_gate), bf(ffn2_w_up), bf(ffn2_w_down))
    w_in_b, w_out_b, sgu_w_b = bf(w_in), bf(w_out), bf(sgu_w)
    nm = row(norm_mix)
    lng, lnb = row(sgu_ln_g), row(sgu_ln_b)
    ga, gs = row(out_norm_attn), row(out_norm_sgu)
    sgu_b_col = sgu_b[..., None]
    fgain = final_norm[None, :]
    rope = _rope_tables(seq)

    xt = x.reshape(tokens, D_MODEL)
    for layer in range(DEPTH):
        xt, qs, ks, vs, sg = _ffn_inproj(xt, ffn1, nm, w_in_b, rope, lng, lnb, sgu_w_b, sgu_b_col,
                                         layer, seq)
        attn = []
        for q, k, v, dil in zip(qs, ks, vs, DILATIONS):
            attn.extend(_attention(q, k, v, dil, seq))
        xt = _merge_ffn(xt, attn, sg, ga, gs, w_out_b, ffn2, fgain, layer, layer == DEPTH - 1)
    return xt.reshape(batch, seq, D_MODEL)
```

```python
import functools
import math

import numpy as np
import jax
import jax.numpy as jnp
from jax import lax
from jax.experimental import pallas as pl
from jax.experimental.pallas import tpu as pltpu

F32 = jnp.float32
BF16 = jnp.bfloat16

D_MODEL = 1024
DEPTH = 4
HEAD_DIM = 64
D_ATTN = 512
D_SGU = 512
N_SGU_GROUPS = 4
CHUNK = 128
D_IN = 3 * D_ATTN + 2 * D_SGU
D_FF = 2816
DILATIONS = (1, 4, 16)
N_PAT = len(DILATIONS)
BAND = 64
ROPE_THETA = 500000.0
ROPE_DIM = HEAD_DIM // 4
EPS = 1e-6
NEG_INF = -1e30
LOG2_E = math.log2(math.e)
LN_2 = math.log(2.0)

LANES = 128
SUBLANES = 8
V7X_VMEM_BYTES = 64 * 1024 * 1024
VMEM_LIMIT = V7X_VMEM_BYTES * 7 // 8

TM = 512
FF_CHUNK = 256
N_FF_CHUNKS = D_FF // FF_CHUNK
N_SLABS = D_ATTN // LANES
MAX_ATT_ROWS = 2048
QB = 2 * BAND
KW = QB + 2 * BAND


def _rms(x, gain):
    return x * lax.rsqrt(jnp.mean(x * x, axis=-1, keepdims=True) + EPS) * gain


def _params(n_axes):
    return pltpu.CompilerParams(dimension_semantics=("arbitrary",) * n_axes,
                                vmem_limit_bytes=VMEM_LIMIT)


def _resident(shape, layer):
    nd = len(shape)
    return pl.BlockSpec((None,) + shape, lambda *_: (layer,) + (0,) * nd,
                        pipeline_mode=pl.Buffered(1))


def _view_shape(tokens, dil, dtype):
    return jax.ShapeDtypeStruct((tokens // dil, dil * D_ATTN), dtype)


def _ordering_zero(values):
    acc = None
    for v in values:
        bits = pltpu.bitcast(v[0:SUBLANES, 0:LANES], jnp.uint32)
        acc = bits if acc is None else acc | bits
    return ((acc >> 16) >> 16)[0:1, :].astype(F32)


def _ffn_apply(x, gain_ref, wg_ref, wu_ref, wd_ref, act_ref, side_work=()):
    h = _rms(x, gain_ref[...]).astype(BF16)
    for c in range(N_FF_CHUNKS):
        cols = slice(c * FF_CHUNK, (c + 1) * FF_CHUNK)
        g = jnp.dot(h, wg_ref[:, cols], preferred_element_type=F32)
        u = jnp.dot(h, wu_ref[:, cols], preferred_element_type=F32)
        a = g * jax.nn.sigmoid(g) * u
        todo = side_work[c * len(side_work) // N_FF_CHUNKS:(c + 1) * len(side_work) // N_FF_CHUNKS]
        if todo:
            zero = _ordering_zero([v for work in todo for v in work()])
            a = a + jnp.concatenate([zero] * (FF_CHUNK // LANES), axis=1)
        act_ref[:, cols] = a.astype(BF16)
    return x + 0.5 * jnp.dot(act_ref[...], wd_ref[...], preferred_element_type=F32)


def _ffn_specs(layer):
    return [_resident((1, D_MODEL), layer), _resident((D_MODEL, D_FF), layer),
            _resident((D_MODEL, D_FF), layer), _resident((D_FF, D_MODEL), layer)]


def _gelu(x):
    return 0.5 * x * (1.0 + lax.erf(x * (1.0 / math.sqrt(2.0))))


def _inproj_matmul(x, gain_ref, w_ref, raw_ref):
    h = _rms(x, gain_ref[...]).astype(BF16)
    for col0 in range(0, D_IN, D_ATTN):
        raw_ref[:, col0:col0 + D_ATTN] = jnp.dot(h, w_ref[:, col0:col0 + D_ATTN],
                                                 preferred_element_type=F32)


def _inproj_finish_work(raw_ref, cos_ref, sina_ref, sinb_ref, lng_ref, lnb_ref, sw_ref, sb_ref,
                        q_refs, k_refs, v_refs, sg_ref, stage_ref):
    def qkv_slab(col0, dst_refs, rotary, scale, s):
        def work():
            t = raw_ref[:, col0 + s * LANES:col0 + (s + 1) * LANES]
            if rotary:
                half = ROPE_DIM // 2
                t = (t * cos_ref[...] + pltpu.roll(t, half, 1) * sina_ref[...]
                     + pltpu.roll(t, LANES - half, 1) * sinb_ref[...])
            if scale != 1.0:
                t = t * scale
            dst_refs[0][:, s * LANES:(s + 1) * LANES] = t.astype(BF16)
            stage_ref[0, s] = t
            produced = [t]
            for k in range(1, N_PAT):
                d_prev, dil = DILATIONS[k - 1], DILATIONS[k]
                ratio, n_prev, n = dil // d_prev, TM // d_prev, TM // dil
                for e in range(d_prev):
                    for c in range(ratio):
                        r = c * d_prev + e
                        blk = stage_ref[k - 1, s, pl.ds(e * n_prev + c, n, stride=ratio), :]
                        dst_refs[k][:, r * D_ATTN + s * LANES:r * D_ATTN + (s + 1) * LANES] = (
                            blk.astype(BF16))
                        if k + 1 < N_PAT:
                            stage_ref[k, s, r * n:(r + 1) * n, :] = blk
                        produced.append(blk)
            return produced
        return work

    def sgu_chunk(c):
        def work():
            rows = slice(c * CHUNK, (c + 1) * CHUNK)
            u = _gelu(raw_ref[rows, 3 * D_ATTN:3 * D_ATTN + D_SGU])
            g = _gelu(raw_ref[rows, 3 * D_ATTN + D_SGU:])
            gc = g - jnp.mean(g, axis=-1, keepdims=True)
            gn = (gc * lax.rsqrt(jnp.mean(gc * gc, axis=-1, keepdims=True) + EPS) * lng_ref[...]
                  + lnb_ref[...]).astype(BF16)
            group = D_SGU // N_SGU_GROUPS
            produced = []
            for gi in range(N_SGU_GROUPS):
                cols = slice(gi * group, (gi + 1) * group)
                mixed = jnp.dot(sw_ref[gi], gn[:, cols], preferred_element_type=F32) + sb_ref[gi]
                gated = u[:, cols] * mixed
                sg_ref[rows, cols] = gated.astype(BF16)
                produced.append(gated)
            return produced
        return work

    q_scale = LOG2_E / math.sqrt(HEAD_DIM)
    pieces = [qkv_slab(0, q_refs, True, q_scale, s) for s in range(N_SLABS)]
    pieces += [qkv_slab(D_ATTN, k_refs, True, 1.0, s) for s in range(N_SLABS)]
    pieces += [qkv_slab(2 * D_ATTN, v_refs, False, 1.0, s) for s in range(N_SLABS)]
    pieces += [sgu_chunk(c) for c in range(TM // CHUNK)]
    return pieces


def _ffn_inproj_kernel(x_ref, *refs, n_tiles):
    ffn_refs, gain_ref, w_ref, finish_refs = refs[:4], refs[4], refs[5], refs[6:13]
    xo_ref = refs[13]
    q_refs, k_refs, v_refs = (refs[14 + i * N_PAT:14 + (i + 1) * N_PAT] for i in range(3))
    sg_ref, act_ref, stage_ref, raw_ref = refs[14 + 3 * N_PAT:]
    step = pl.program_id(0)
    finish = _inproj_finish_work(raw_ref, *finish_refs, q_refs, k_refs, v_refs, sg_ref, stage_ref)

    @pl.when(step == 0)
    def _():
        raw_ref[...] = jnp.zeros_like(raw_ref)

    @pl.when(step < n_tiles)
    def _():
        x = _ffn_apply(x_ref[...], *ffn_refs, act_ref, side_work=finish)
        xo_ref[...] = x
        _inproj_matmul(x, gain_ref, w_ref, raw_ref)

    @pl.when(step == n_tiles)
    def _():
        for work in finish:
            work()


def _ffn_inproj(x, ffn_w, gain, w_in, rope, lng, lnb, sw, sb, layer, seq):
    tokens = x.shape[0]
    n_tiles = tokens // TM
    lead = lambda s: (jnp.minimum(s, n_tiles - 1), 0)
    lag = lambda s: (jnp.maximum(s - 1, 0), 0)
    table = pl.BlockSpec((TM, LANES), lambda s: (jnp.maximum(s - 1, 0) % (seq // TM), 0))
    views = [pl.BlockSpec((TM // d, d * D_ATTN), lag) for d in DILATIONS]
    view_shapes = [_view_shape(tokens, d, BF16) for d in DILATIONS]
    outs = pl.pallas_call(
        functools.partial(_ffn_inproj_kernel, n_tiles=n_tiles),
        grid=(n_tiles + 1,),
        in_specs=[pl.BlockSpec((TM, D_MODEL), lead)] + _ffn_specs(layer)
                 + [_resident((1, D_MODEL), layer), _resident((D_MODEL, D_IN), layer),
                    table, table, table,
                    _resident((1, D_SGU), layer), _resident((1, D_SGU), layer),
                    _resident((N_SGU_GROUPS, CHUNK, CHUNK), layer),
                    _resident((N_SGU_GROUPS, CHUNK, 1), layer)],
        out_specs=[pl.BlockSpec((TM, D_MODEL), lead)] + views * 3 + [pl.BlockSpec((TM, D_SGU), lag)],
        out_shape=[jax.ShapeDtypeStruct((tokens, D_MODEL), F32)] + view_shapes * 3
                  + [jax.ShapeDtypeStruct((tokens, D_SGU), BF16)],
        scratch_shapes=[pltpu.VMEM((TM, D_FF), BF16), pltpu.VMEM((N_PAT - 1, N_SLABS, TM, LANES), F32),
                        pltpu.VMEM((TM, D_IN), F32)],
        compiler_params=_params(1),
        name="ffn_inproj",
    )(x, *ffn_w, gain, w_in, *rope, lng, lnb, sw, sb)
    return (outs[0], outs[1:1 + N_PAT], outs[1 + N_PAT:1 + 2 * N_PAT],
            outs[1 + 2 * N_PAT:1 + 3 * N_PAT], outs[-1])


def _rope_tables(seq):
    half = ROPE_DIM // 2
    inv_freq = ROPE_THETA ** (-np.arange(0, ROPE_DIM, 2, dtype=np.float64) / ROPE_DIM)
    ang = np.arange(seq, dtype=np.float64)[:, None] * inv_freq[None, :]
    cos = np.ones((seq, LANES))
    sina = np.zeros((seq, LANES))
    sinb = np.zeros((seq, LANES))
    for head0 in range(0, LANES, HEAD_DIM):
        cos[:, head0:head0 + half] = np.cos(ang)
        cos[:, head0 + half:head0 + 2 * half] = np.cos(ang)
        sinb[:, head0:head0 + half] = -np.sin(ang)
        sina[:, head0 + half:head0 + 2 * half] = np.sin(ang)
    return tuple(jnp.asarray(t, F32) for t in (cos, sina, sinb))


def _attn_kernel(q_ref, kc_ref, kp_ref, kn_ref, vc_ref, vp_ref, vn_ref, o_ref, lse_ref,
                 kbuf, vbuf, *, sub_len, rows, n_res):
    ones = jnp.ones((rows + 2 * BAND, LANES), BF16)
    for res in range(n_res):
        c0 = res * D_ATTN
        kbuf[res, 0:BAND] = kp_ref[:, c0:c0 + D_ATTN]
        kbuf[res, BAND:BAND + rows] = kc_ref[:, c0:c0 + D_ATTN]
        kbuf[res, BAND + rows:] = kn_ref[:, c0:c0 + D_ATTN]
        for pair in range(N_SLABS):
            src = slice(c0 + pair * LANES, c0 + (pair + 1) * LANES)
            dst = slice(2 * pair * LANES, (2 * pair + 1) * LANES)
            vbuf[res, 0:BAND, dst] = vp_ref[:, src]
            vbuf[res, BAND:BAND + rows, dst] = vc_ref[:, src]
            vbuf[res, BAND + rows:, dst] = vn_ref[:, src]
            vbuf[res, :, (2 * pair + 1) * LANES:(2 * pair + 2) * LANES] = ones

    first_row = (pl.program_id(0) % (sub_len // rows)) * rows
    delta = (lax.broadcasted_iota(jnp.int32, (QB, KW), 1)
             - lax.broadcasted_iota(jnp.int32, (QB, KW), 0))
    band_bias = jnp.where((delta >= 0) & (delta <= 2 * BAND), 0.0, NEG_INF)
    key_col = lax.broadcasted_iota(jnp.int32, (1, KW), 1)
    lane = lax.broadcasted_iota(jnp.int32, (1, LANES), 1)
    low_head = lane < HEAD_DIM

    for j in range(rows // QB):
        row0 = j * QB
        key_pos = key_col + (first_row + row0 - BAND)
        bias = band_bias + jnp.where((key_pos >= 0) & (key_pos < sub_len), 0.0, NEG_INF)
        bias = jnp.concatenate([bias, bias], axis=0)
        for res in range(n_res):
            lse_tile = jnp.zeros((QB, LANES), F32)
            for pair in range(N_SLABS):
                cols = slice(res * D_ATTN + pair * LANES, res * D_ATTN + (pair + 1) * LANES)
                q2 = q_ref[row0:row0 + QB, cols]
                zero = jnp.zeros_like(q2)
                qs = jnp.concatenate([jnp.where(low_head, q2, zero), jnp.where(low_head, zero, q2)],
                                     axis=0)
                k2 = kbuf[res, row0:row0 + KW, pair * LANES:(pair + 1) * LANES]
                s = lax.dot_general(qs, k2, (((1,), (1,)), ((), ())),
                                    preferred_element_type=F32) + bias
                m = jnp.max(s, axis=-1, keepdims=True)
                p = jnp.exp2(s - m).astype(BF16)
                pv = jnp.dot(p, vbuf[res, row0:row0 + KW, 2 * pair * LANES:(2 * pair + 2) * LANES],
                             preferred_element_type=F32)
                num = jnp.where(low_head, pv[:QB, :LANES], pv[QB:, :LANES])
                den = jnp.where(low_head, pv[:QB, LANES:], pv[QB:, LANES:])
                o_ref[row0:row0 + QB, cols] = (num / den).astype(BF16)
                lse_pair = jnp.where(low_head, m[:QB], m[QB:]) * LN_2 + jnp.log(den)
                lse_tile = jnp.where((lane == pair) | (lane == HEAD_DIM + pair), lse_pair, lse_tile)
            lse_ref[row0:row0 + QB, res * LANES:(res + 1) * LANES] = lse_tile


def _attention(q, k, v, dil, seq):
    total_rows = q.shape[0]
    sub_len = seq // dil
    rows = min(MAX_ATT_ROWS, sub_len)
    n_res = min(MAX_ATT_ROWS // rows, dil)
    assert sub_len % rows == 0 and rows % QB == 0 and dil % n_res == 0
    halos_per_step = rows // BAND
    last_halo = total_rows // BAND - 1
    width = n_res * D_ATTN
    cur = pl.BlockSpec((rows, width), lambda i, r: (i, r))
    prev = pl.BlockSpec((BAND, width), lambda i, r: (jnp.maximum(i * halos_per_step - 1, 0), r))
    nxt = pl.BlockSpec((BAND, width),
                       lambda i, r: (jnp.minimum((i + 1) * halos_per_step, last_halo), r))
    return pl.pallas_call(
        functools.partial(_attn_kernel, sub_len=sub_len, rows=rows, n_res=n_res),
        grid=(total_rows // rows, dil // n_res),
        in_specs=[cur, cur, prev, nxt, cur, prev, nxt],
        out_specs=[cur, pl.BlockSpec((rows, n_res * LANES), lambda i, r: (i, r))],
        out_shape=[jax.ShapeDtypeStruct(q.shape, BF16),
                   jax.ShapeDtypeStruct((total_rows, dil * LANES), F32)],
        scratch_shapes=[pltpu.VMEM((n_res, rows + 2 * BAND, D_ATTN), BF16),
                        pltpu.VMEM((n_res, rows + 2 * BAND, 2 * D_ATTN), BF16)],
        compiler_params=_params(2),
        name=f"attn_d{dil}",
    )(q, k, k, k, v, v, v)


def _merge_work(o_refs, l_refs, sg_ref, ga_ref, gs_ref, norm_ref, stage_refs):
    def unview_slab(s):
        def work():
            produced = []
            for p, dil in enumerate(DILATIONS[1:]):
                for r in range(dil):
                    blk = o_refs[p + 1][:, r * D_ATTN + s * LANES:r * D_ATTN + (s + 1) * LANES]
                    blk = blk.astype(F32)
                    stage_refs[2 * p][s, pl.ds(r, TM // dil, stride=dil), :] = blk
                    produced.append(blk)
            return produced
        return work

    def unview_lse():
        produced = []
        for p, dil in enumerate(DILATIONS[1:]):
            for r in range(dil):
                blk = l_refs[p + 1][:, r * LANES:(r + 1) * LANES]
                stage_refs[2 * p + 1][pl.ds(r, TM // dil, stride=dil), :] = blk
                produced.append(blk)
        return produced

    def merge_rows(c):
        def work():
            rows = slice(c * CHUNK, (c + 1) * CHUNK)
            low_head = lax.broadcasted_iota(jnp.int32, (1, LANES), 1) < HEAD_DIM
            lses = [l_refs[0][rows, :]] + [stage_refs[2 * p + 1][rows, :] for p in range(N_PAT - 1)]
            m = functools.reduce(jnp.maximum, lses)
            es = [jnp.exp(l - m) for l in lses]
            inv = 1.0 / sum(es)
            slabs = []
            for s in range(N_SLABS):
                cols = slice(s * LANES, (s + 1) * LANES)
                outs = [o_refs[0][rows, cols].astype(F32)]
                outs += [stage_refs[2 * p][s, rows, :] for p in range(N_PAT - 1)]
                acc = None
                for e, o in zip(es, outs):
                    w = e * inv
                    w = jnp.where(low_head, w[:, s:s + 1], w[:, HEAD_DIM + s:HEAD_DIM + s + 1])
                    acc = w * o if acc is None else acc + w * o
                slabs.append(acc)
            na = _rms(jnp.concatenate(slabs, axis=1), ga_ref[...])
            ns = _rms(sg_ref[rows, :].astype(F32), gs_ref[...])
            norm_ref[rows, 0:D_ATTN] = na.astype(BF16)
            norm_ref[rows, D_ATTN:] = ns.astype(BF16)
            return [na, ns]
        return work

    return ([unview_slab(s) for s in range(N_SLABS)] + [unview_lse]
            + [merge_rows(c) for c in range(TM // CHUNK)])


def _merge_ffn_kernel(x_ref, *refs, n_tiles, final):
    o_refs, l_refs = refs[0:2 * N_PAT:2], refs[1:2 * N_PAT:2]
    sg_ref, ga_ref, gs_ref, wo_ref = refs[2 * N_PAT:2 * N_PAT + 4]
    ffn_refs = refs[2 * N_PAT + 4:2 * N_PAT + 8]
    fgain_ref, out_ref, act_ref, norm_ref = refs[2 * N_PAT + 8:2 * N_PAT + 12]
    stage_refs = refs[2 * N_PAT + 12:]
    step = pl.program_id(0)
    merge = _merge_work(o_refs, l_refs, sg_ref, ga_ref, gs_ref, norm_ref, stage_refs)

    def project_and_ffn(side_work):
        x = x_ref[...] + jnp.dot(norm_ref[...], wo_ref[...], preferred_element_type=F32)
        y = _ffn_apply(x, *ffn_refs, act_ref, side_work=side_work)
        out_ref[...] = _rms(y, fgain_ref[...]) if final else y

    @pl.when(step == 0)
    def _():
        for work in merge:
            work()

    @pl.when((step > 0) & (step < n_tiles))
    def _():
        project_and_ffn(merge)

    @pl.when(step == n_tiles)
    def _():
        project_and_ffn(())


def _merge_ffn(x, attn, sg, ga, gs, wo, ffn_w, fgain, layer, final):
    tokens = x.shape[0]
    n_tiles = tokens // TM
    lead = lambda s: (jnp.minimum(s, n_tiles - 1), 0)
    lag = lambda s: (jnp.maximum(s - 1, 0), 0)
    views = [pl.BlockSpec((TM // d, d * width), lead) for d in DILATIONS for width in (D_ATTN, LANES)]
    return pl.pallas_call(
        functools.partial(_merge_ffn_kernel, n_tiles=n_tiles, final=final),
        grid=(n_tiles + 1,),
        in_specs=[pl.BlockSpec((TM, D_MODEL), lag)] + views + [pl.BlockSpec((TM, D_SGU), lead)]
                 + [_resident((1, D_ATTN), layer), _resident((1, D_SGU), layer),
                    _resident((D_ATTN + D_SGU, D_MODEL), layer)]
                 + _ffn_specs(layer) + [pl.BlockSpec((1, D_MODEL), lambda s: (0, 0))],
        out_specs=pl.BlockSpec((TM, D_MODEL), lag),
        out_shape=jax.ShapeDtypeStruct((tokens, D_MODEL), F32),
        scratch_shapes=[pltpu.VMEM((TM, D_FF), BF16), pltpu.VMEM((TM, D_ATTN + D_SGU), BF16)]
                       + [pltpu.VMEM((N_SLABS, TM, LANES), F32), pltpu.VMEM((TM, LANES), F32)] * (N_PAT - 1),
        compiler_params=_params(1),
        name="merge_ffn",
    )(x, *attn, sg, ga, gs, wo, *ffn_w, fgain)


def kernel(x, norm_ffn1, ffn1_w_gate, ffn1_w_up, ffn1_w_down, norm_mix, w_in, sgu_ln_g, sgu_ln_b,
           sgu_w, sgu_b, out_norm_attn, out_norm_sgu, w_out, norm_ffn2, ffn2_w_gate, ffn2_w_up,
           ffn2_w_down, final_norm):
    batch, seq, _ = x.shape
    tokens = batch * seq
    assert x.shape[2] == D_MODEL and tokens % TM == 0 and seq % TM == 0 and TM % CHUNK == 0

    row = lambda p: p[:, None, :]
    bf = lambda w: w.astype(BF16)
    ffn1 = (row(norm_ffn1), bf(ffn1_w_gate), bf(ffn1_w_up), bf(ffn1_w_down))
    ffn2 = (row(norm_ffn2), bf(ffn2_w_gate), bf(ffn2_w_up), bf(ffn2_w_down))
    w_in_b, w_out_b, sgu_w_b = bf(w_in), bf(w_out), bf(sgu_w)
    nm = row(norm_mix)
    lng, lnb = row(sgu_ln_g), row(sgu_ln_b)
    ga, gs = row(out_norm_attn), row(out_norm_sgu)
    sgu_b_col = sgu_b[..., None]
    fgain = final_norm[None, :]
    rope = _rope_tables(seq)

    xt = x.reshape(tokens, D_MODEL)
    for layer in range(DEPTH):
        xt, qs, ks, vs, sg = _ffn_inproj(xt, ffn1, nm, w_in_b, rope, lng, lnb, sgu_w_b, sgu_b_col,
                                         layer, seq)
        attn = []
        for q, k, v, dil in zip(qs, ks, vs, DILATIONS):
            attn.extend(_attention(q, k, v, dil, seq))
        xt = _merge_ffn(xt, attn, sg, ga, gs, w_out_b, ffn2, fgain, layer, layer == DEPTH - 1)
    return xt.reshape(batch, seq, D_MODEL)
```

```python
import functools
import math

import numpy as np
import jax
import jax.numpy as jnp
from jax import lax
from jax.experimental import pallas as pl
from jax.experimental.pallas import tpu as pltpu

F32 = jnp.float32
BF16 = jnp.bfloat16

D_MODEL = 1024
DEPTH = 4
HEAD_DIM = 64
D_ATTN = 512
D_SGU = 512
N_SGU_GROUPS = 4
CHUNK = 128
D_IN = 3 * D_ATTN + 2 * D_SGU
D_FF = 2816
DILATIONS = (1, 4, 16)
N_PAT = len(DILATIONS)
BAND = 64
ROPE_THETA = 500000.0
ROPE_DIM = HEAD_DIM // 4
EPS = 1e-6
NEG_INF = -1e30
LOG2_E = math.log2(math.e)
LN_2 = math.log(2.0)

LANES = 128
SUBLANES = 8
V7X_VMEM_BYTES = 64 * 1024 * 1024
VMEM_LIMIT = V7X_VMEM_BYTES * 7 // 8

TM = 512
FF_CHUNK = 256
N_FF_CHUNKS = D_FF // FF_CHUNK
N_SLABS = D_ATTN // LANES
MAX_ATT_ROWS = 2048
QB = 2 * BAND
KW = QB + 2 * BAND


def _rms(x, gain):
    return x * lax.rsqrt(jnp.mean(x * x, axis=-1, keepdims=True) + EPS) * gain


def _params(n_axes):
    return pltpu.CompilerParams(dimension_semantics=("arbitrary",) * n_axes,
                                vmem_limit_bytes=VMEM_LIMIT)


def _resident(shape, layer):
    nd = len(shape)
    return pl.BlockSpec((None,) + shape, lambda *_: (layer,) + (0,) * nd,
                        pipeline_mode=pl.Buffered(1))


def _view_shape(tokens, dil, dtype):
    return jax.ShapeDtypeStruct((tokens // dil, dil * D_ATTN), dtype)


def _ordering_zero(values):
    acc = None
    for v in values:
        bits = pltpu.bitcast(v[0:SUBLANES, 0:LANES], jnp.uint32)
        acc = bits if acc is None else acc | bits
    return ((acc >> 16) >> 16)[0:1, :].astype(F32)


def _ffn_apply(x, gain_ref, wg_ref, wu_ref, wd_ref, act_ref, side_work=()):
    h = _rms(x, gain_ref[...]).astype(BF16)
    for c in range(N_FF_CHUNKS):
        cols = slice(c * FF_CHUNK, (c + 1) * FF_CHUNK)
        g = jnp.dot(h, wg_ref[:, cols], preferred_element_type=F32)
        u = jnp.dot(h, wu_ref[:, cols], preferred_element_type=F32)
        a = g * jax.nn.sigmoid(g) * u
        todo = side_work[c * len(side_work) // N_FF_CHUNKS:(c + 1) * len(side_work) // N_FF_CHUNKS]
        if todo:
            zero = _ordering_zero([v for work in todo for v in work()])
            a = a + jnp.concatenate([zero] * (FF_CHUNK // LANES), axis=1)
        act_ref[:, cols] = a.astype(BF16)
    return x + 0.5 * jnp.dot(act_ref[...], wd_ref[...], preferred_element_type=F32)


def _ffn_specs(layer):
    return [_resident((1, D_MODEL), layer), _resident((D_MODEL, D_FF), layer),
            _resident((D_MODEL, D_FF), layer), _resident((D_FF, D_MODEL), layer)]


def _gelu(x):
    return 0.5 * x * (1.0 + lax.erf(x * (1.0 / math.sqrt(2.0))))


def _inproj_matmul(x, gain_ref, w_ref, raw_ref):
    h = _rms(x, gain_ref[...]).astype(BF16)
    for col0 in range(0, D_IN, D_ATTN):
        raw_ref[:, col0:col0 + D_ATTN] = jnp.dot(h, w_ref[:, col0:col0 + D_ATTN],
                                                 preferred_element_type=F32)


def _inproj_finish_work(raw_ref, cos_ref, sina_ref, sinb_ref, lng_ref, lnb_ref, sw_ref, sb_ref,
                        q_refs, k_refs, v_refs, sg_ref, stage_ref):
    def qkv_slab(col0, dst_refs, rotary, scale, s):
        def work():
            t = raw_ref[:, col0 + s * LANES:col0 + (s + 1) * LANES]
            if rotary:
                half = ROPE_DIM // 2
                t = (t * cos_ref[...] + pltpu.roll(t, half, 1) * sina_ref[...]
                     + pltpu.roll(t, LANES - half, 1) * sinb_ref[...])
            if scale != 1.0:
                t = t * scale
            dst_refs[0][:, s * LANES:(s + 1) * LANES] = t.astype(BF16)
            stage_ref[0, s] = t
            produced = [t]
            for k in range(1, N_PAT):
                d_prev, dil = DILATIONS[k - 1], DILATIONS[k]
                ratio, n_prev, n = dil // d_prev, TM // d_prev, TM // dil
                for e in range(d_prev):
                    for c in range(ratio):
                        r = c * d_prev + e
                        blk = stage_ref[k - 1, s, pl.ds(e * n_prev + c, n, stride=ratio), :]
                        dst_refs[k][:, r * D_ATTN + s * LANES:r * D_ATTN + (s + 1) * LANES] = (
                            blk.astype(BF16))
                        if k + 1 < N_PAT:
                            stage_ref[k, s, r * n:(r + 1) * n, :] = blk
                        produced.append(blk)
            return produced
        return work

    def sgu_chunk(c):
        def work():
            rows = slice(c * CHUNK, (c + 1) * CHUNK)
            u = _gelu(raw_ref[rows, 3 * D_ATTN:3 * D_ATTN + D_SGU])
            g = _gelu(raw_ref[rows, 3 * D_ATTN + D_SGU:])
            gc = g - jnp.mean(g, axis=-1, keepdims=True)
            gn = (gc * lax.rsqrt(jnp.mean(gc * gc, axis=-1, keepdims=True) + EPS) * lng_ref[...]
                  + lnb_ref[...]).astype(BF16)
            group = D_SGU // N_SGU_GROUPS
            produced = []
            for gi in range(N_SGU_GROUPS):
                cols = slice(gi * group, (gi + 1) * group)
                mixed = jnp.dot(sw_ref[gi], gn[:, cols], preferred_element_type=F32) + sb_ref[gi]
                gated = u[:, cols] * mixed
                sg_ref[rows, cols] = gated.astype(BF16)
                produced.append(gated)
            return produced
        return work

    q_scale = LOG2_E / math.sqrt(HEAD_DIM)
    pieces = [qkv_slab(0, q_refs, True, q_scale, s) for s in range(N_SLABS)]
    pieces += [qkv_slab(D_ATTN, k_refs, True, 1.0, s) for s in range(N_SLABS)]
    pieces += [qkv_slab(2 * D_ATTN, v_refs, False, 1.0, s) for s in range(N_SLABS)]
    pieces += [sgu_chunk(c) for c in range(TM // CHUNK)]
    return pieces


def _ffn_inproj_kernel(x_ref, *refs, n_tiles):
    ffn_refs, gain_ref, w_ref, finish_refs = refs[:4], refs[4], refs[5], refs[6:13]
    xo_ref = refs[13]
    q_refs, k_refs, v_refs = (refs[14 + i * N_PAT:14 + (i + 1) * N_PAT] for i in range(3))
    sg_ref, act_ref, stage_ref, raw_ref = refs[14 + 3 * N_PAT:]
    step = pl.program_id(0)
    finish = _inproj_finish_work(raw_ref, *finish_refs, q_refs, k_refs, v_refs, sg_ref, stage_ref)

    @pl.when(step == 0)
    def _():
        raw_ref[...] = jnp.zeros_like(raw_ref)

    @pl.when(step < n_tiles)
    def _():
        x = _ffn_apply(x_ref[...], *ffn_refs, act_ref, side_work=finish)
        xo_ref[...] = x
        _inproj_matmul(x, gain_ref, w_ref, raw_ref)

    @pl.when(step == n_tiles)
    def _():
        for work in finish:
            work()


def _ffn_inproj(x, ffn_w, gain, w_in, rope, lng, lnb, sw, sb, layer, seq):
    tokens = x.shape[0]
    n_tiles = tokens // TM
    lead = lambda s: (jnp.minimum(s, n_tiles - 1), 0)
    lag = lambda s: (jnp.maximum(s - 1, 0), 0)
    table = pl.BlockSpec((TM, LANES), lambda s: (jnp.maximum(s - 1, 0) % (seq // TM), 0))
    views = [pl.BlockSpec((TM // d, d * D_ATTN), lag) for d in DILATIONS]
    view_shapes = [_view_shape(tokens, d, BF16) for d in DILATIONS]
    outs = pl.pallas_call(
        functools.partial(_ffn_inproj_kernel, n_tiles=n_tiles),
        grid=(n_tiles + 1,),
        in_specs=[pl.BlockSpec((TM, D_MODEL), lead)] + _ffn_specs(layer)
                 + [_resident((1, D_MODEL), layer), _resident((D_MODEL, D_IN), layer),
                    table, table, table,
                    _resident((1, D_SGU), layer), _resident((1, D_SGU), layer),
                    _resident((N_SGU_GROUPS, CHUNK, CHUNK), layer),
                    _resident((N_SGU_GROUPS, CHUNK, 1), layer)],
        out_specs=[pl.BlockSpec((TM, D_MODEL), lead)] + views * 3 + [pl.BlockSpec((TM, D_SGU), lag)],
        out_shape=[jax.ShapeDtypeStruct((tokens, D_MODEL), F32)] + view_shapes * 3
                  + [jax.ShapeDtypeStruct((tokens, D_SGU), BF16)],
        scratch_shapes=[pltpu.VMEM((TM, D_FF), BF16), pltpu.VMEM((N_PAT - 1, N_SLABS, TM, LANES), F32),
                        pltpu.VMEM((TM, D_IN), F32)],
        compiler_params=_params(1),
        name="ffn_inproj",
    )(x, *ffn_w, gain, w_in, *rope, lng, lnb, sw, sb)
    return (outs[0], outs[1:1 + N_PAT], outs[1 + N_PAT:1 + 2 * N_PAT],
            outs[1 + 2 * N_PAT:1 + 3 * N_PAT], outs[-1])


def _rope_tables(seq):
    half = ROPE_DIM // 2
    inv_freq = ROPE_THETA ** (-np.arange(0, ROPE_DIM, 2, dtype=np.float64) / ROPE_DIM)
    ang = np.arange(seq, dtype=np.float64)[:, None] * inv_freq[None, :]
    cos = np.ones((seq, LANES))
    sina = np.zeros((seq, LANES))
    sinb = np.zeros((seq, LANES))
    for head0 in range(0, LANES, HEAD_DIM):
        cos[:, head0:head0 + half] = np.cos(ang)
        cos[:, head0 + half:head0 + 2 * half] = np.cos(ang)
        sinb[:, head0:head0 + half] = -np.sin(ang)
        sina[:, head0 + half:head0 + 2 * half] = np.sin(ang)
    return tuple(jnp.asarray(t, F32) for t in (cos, sina, sinb))


def _attn_kernel(q_ref, kc_ref, kp_ref, kn_ref, vc_ref, vp_ref, vn_ref, o_ref, lse_ref,
                 kbuf, vbuf, ktbuf, *, sub_len, rows, n_res):
    ones = jnp.ones((rows + 2 * BAND, LANES), BF16)
    for res in range(n_res):
        c0 = res * D_ATTN
        kbuf[res, 0:BAND] = kp_ref[:, c0:c0 + D_ATTN]
        kbuf[res, BAND:BAND + rows] = kc_ref[:, c0:c0 + D_ATTN]
        kbuf[res, BAND + rows:] = kn_ref[:, c0:c0 + D_ATTN]
        ktbuf[res] = kbuf[res].T
        for pair in range(N_SLABS):
            src = slice(c0 + pair * LANES, c0 + (pair + 1) * LANES)
            dst = slice(2 * pair * LANES, (2 * pair + 1) * LANES)
            vbuf[res, 0:BAND, dst] = vp_ref[:, src]
            vbuf[res, BAND:BAND + rows, dst] = vc_ref[:, src]
            vbuf[res, BAND + rows:, dst] = vn_ref[:, src]
            vbuf[res, :, (2 * pair + 1) * LANES:(2 * pair + 2) * LANES] = ones

    first_row = (pl.program_id(0) % (sub_len // rows)) * rows
    delta = (lax.broadcasted_iota(jnp.int32, (QB, KW), 1)
             - lax.broadcasted_iota(jnp.int32, (QB, KW), 0))
    band_bias = jnp.where((delta >= 0) & (delta <= 2 * BAND), 0.0, NEG_INF)
    key_col = lax.broadcasted_iota(jnp.int32, (1, KW), 1)
    lane = lax.broadcasted_iota(jnp.int32, (1, LANES), 1)
    low_head = lane < HEAD_DIM

    for j in range(rows // QB):
        row0 = j * QB
        key_pos = key_col + (first_row + row0 - BAND)
        bias = band_bias + jnp.where((key_pos >= 0) & (key_pos < sub_len), 0.0, NEG_INF)
        bias = jnp.concatenate([bias, bias], axis=0)
        for res in range(n_res):
            lse_tile = jnp.zeros((QB, LANES), F32)
            for pair in range(N_SLABS):
                cols = slice(res * D_ATTN + pair * LANES, res * D_ATTN + (pair + 1) * LANES)
                q2 = q_ref[row0:row0 + QB, cols]
                zero = jnp.zeros_like(q2)
                qs = jnp.concatenate([jnp.where(low_head, q2, zero), jnp.where(low_head, zero, q2)],
                                     axis=0)
                k2t = ktbuf[res, pair * LANES:(pair + 1) * LANES, row0:row0 + KW]
                s = jnp.dot(qs, k2t, preferred_element_type=F32) + bias
                m = jnp.max(s, axis=-1, keepdims=True)
                p = jnp.exp2(s - m).astype(BF16)
                pv = jnp.dot(p, vbuf[res, row0:row0 + KW, 2 * pair * LANES:(2 * pair + 2) * LANES],
                             preferred_element_type=F32)
                num = jnp.where(low_head, pv[:QB, :LANES], pv[QB:, :LANES])
                den = jnp.where(low_head, pv[:QB, LANES:], pv[QB:, LANES:])
                o_ref[row0:row0 + QB, cols] = (num / den).astype(BF16)
                lse_pair = jnp.where(low_head, m[:QB], m[QB:]) * LN_2 + jnp.log(den)
                lse_tile = jnp.where((lane == pair) | (lane == HEAD_DIM + pair), lse_pair, lse_tile)
            lse_ref[row0:row0 + QB, res * LANES:(res + 1) * LANES] = lse_tile


def _attention(q, k, v, dil, seq):
    total_rows = q.shape[0]
    sub_len = seq // dil
    rows = min(MAX_ATT_ROWS, sub_len)
    n_res = min(MAX_ATT_ROWS // rows, dil)
    assert sub_len % rows == 0 and rows % QB == 0 and dil % n_res == 0
    halos_per_step = rows // BAND
    last_halo = total_rows // BAND - 1
    width = n_res * D_ATTN
    cur = pl.BlockSpec((rows, width), lambda i, r: (i, r))
    prev = pl.BlockSpec((BAND, width), lambda i, r: (jnp.maximum(i * halos_per_step - 1, 0), r))
    nxt = pl.BlockSpec((BAND, width),
                       lambda i, r: (jnp.minimum((i + 1) * halos_per_step, last_halo), r))
    return pl.pallas_call(
        functools.partial(_attn_kernel, sub_len=sub_len, rows=rows, n_res=n_res),
        grid=(total_rows // rows, dil // n_res),
        in_specs=[cur, cur, prev, nxt, cur, prev, nxt],
        out_specs=[cur, pl.BlockSpec((rows, n_res * LANES), lambda i, r: (i, r))],
        out_shape=[jax.ShapeDtypeStruct(q.shape, BF16),
                   jax.ShapeDtypeStruct((total_rows, dil * LANES), F32)],
        scratch_shapes=[pltpu.VMEM((n_res, rows + 2 * BAND, D_ATTN), BF16),
                        pltpu.VMEM((n_res, rows + 2 * BAND, 2 * D_ATTN), BF16),
                        pltpu.VMEM((n_res, D_ATTN, rows + 2 * BAND), BF16)],
        compiler_params=_params(2),
        name=f"attn_d{dil}",
    )(q, k, k, k, v, v, v)


def _merge_work(o_refs, l_refs, sg_ref, ga_ref, gs_ref, norm_ref, stage_refs):
    def unview_slab(s):
        def work():
            produced = []
            for p, dil in enumerate(DILATIONS[1:]):
                for r in range(dil):
                    blk = o_refs[p + 1][:, r * D_ATTN + s * LANES:r * D_ATTN + (s + 1) * LANES]
                    blk = blk.astype(F32)
                    stage_refs[2 * p][s, pl.ds(r, TM // dil, stride=dil), :] = blk
                    produced.append(blk)
            return produced
        return work

    def unview_lse():
        produced = []
        for p, dil in enumerate(DILATIONS[1:]):
            for r in range(dil):
                blk = l_refs[p + 1][:, r * LANES:(r + 1) * LANES]
                stage_refs[2 * p + 1][pl.ds(r, TM // dil, stride=dil), :] = blk
                produced.append(blk)
        return produced

    def merge_rows(c):
        def work():
            rows = slice(c * CHUNK, (c + 1) * CHUNK)
            low_head = lax.broadcasted_iota(jnp.int32, (1, LANES), 1) < HEAD_DIM
            lses = [l_refs[0][rows, :]] + [stage_refs[2 * p + 1][rows, :] for p in range(N_PAT - 1)]
            m = functools.reduce(jnp.maximum, lses)
            es = [jnp.exp(l - m) for l in lses]
            inv = 1.0 / sum(es)
            slabs = []
            for s in range(N_SLABS):
                cols = slice(s * LANES, (s + 1) * LANES)
                outs = [o_refs[0][rows, cols].astype(F32)]
                outs += [stage_refs[2 * p][s, rows, :] for p in range(N_PAT - 1)]
                acc = None
                for e, o in zip(es, outs):
                    w = e * inv
                    w = jnp.where(low_head, w[:, s:s + 1], w[:, HEAD_DIM + s:HEAD_DIM + s + 1])
                    acc = w * o if acc is None else acc + w * o
                slabs.append(acc)
            na = _rms(jnp.concatenate(slabs, axis=1), ga_ref[...])
            ns = _rms(sg_ref[rows, :].astype(F32), gs_ref[...])
            norm_ref[rows, 0:D_ATTN] = na.astype(BF16)
            norm_ref[rows, D_ATTN:] = ns.astype(BF16)
            return [na, ns]
        return work

    return ([unview_slab(s) for s in range(N_SLABS)] + [unview_lse]
            + [merge_rows(c) for c in range(TM // CHUNK)])


def _merge_ffn_kernel(x_ref, *refs, n_tiles, final):
    o_refs, l_refs = refs[0:2 * N_PAT:2], refs[1:2 * N_PAT:2]
    sg_ref, ga_ref, gs_ref, wo_ref = refs[2 * N_PAT:2 * N_PAT + 4]
    ffn_refs = refs[2 * N_PAT + 4:2 * N_PAT + 8]
    fgain_ref, out_ref, act_ref, norm_ref = refs[2 * N_PAT + 8:2 * N_PAT + 12]
    stage_refs = refs[2 * N_PAT + 12:]
    step = pl.program_id(0)
    merge = _merge_work(o_refs, l_refs, sg_ref, ga_ref, gs_ref, norm_ref, stage_refs)

    def project_and_ffn(side_work):
        x = x_ref[...] + jnp.dot(norm_ref[...], wo_ref[...], preferred_element_type=F32)
        y = _ffn_apply(x, *ffn_refs, act_ref, side_work=side_work)
        out_ref[...] = _rms(y, fgain_ref[...]) if final else y

    @pl.when(step == 0)
    def _():
        for work in merge:
            work()

    @pl.when((step > 0) & (step < n_tiles))
    def _():
        project_and_ffn(merge)

    @pl.when(step == n_tiles)
    def _():
        project_and_ffn(())


def _merge_ffn(x, attn, sg, ga, gs, wo, ffn_w, fgain, layer, final):
    tokens = x.shape[0]
    n_tiles = tokens // TM
    lead = lambda s: (jnp.minimum(s, n_tiles - 1), 0)
    lag = lambda s: (jnp.maximum(s - 1, 0), 0)
    views = [pl.BlockSpec((TM // d, d * width), lead) for d in DILATIONS for width in (D_ATTN, LANES)]
    return pl.pallas_call(
        functools.partial(_merge_ffn_kernel, n_tiles=n_tiles, final=final),
        grid=(n_tiles + 1,),
        in_specs=[pl.BlockSpec((TM, D_MODEL), lag)] + views + [pl.BlockSpec((TM, D_SGU), lead)]
                 + [_resident((1, D_ATTN), layer), _resident((1, D_SGU), layer),
                    _resident((D_ATTN + D_SGU, D_MODEL), layer)]
                 + _ffn_specs(layer) + [pl.BlockSpec((1, D_MODEL), lambda s: (0, 0))],
        out_specs=pl.BlockSpec((TM, D_MODEL), lag),
        out_shape=jax.ShapeDtypeStruct((tokens, D_MODEL), F32),
        scratch_shapes=[pltpu.VMEM((TM, D_FF), BF16), pltpu.VMEM((TM, D_ATTN + D_SGU), BF16)]
                       + [pltpu.VMEM((N_SLABS, TM, LANES), F32), pltpu.VMEM((TM, LANES), F32)] * (N_PAT - 1),
        compiler_params=_params(1),
        name="merge_ffn",
    )(x, *attn, sg, ga, gs, wo, *ffn_w, fgain)


def kernel(x, norm_ffn1, ffn1_w_gate, ffn1_w_up, ffn1_w_down, norm_mix, w_in, sgu_ln_g, sgu_ln_b,
           sgu_w, sgu_b, out_norm_attn, out_norm_sgu, w_out, norm_ffn2, ffn2_w_gate, ffn2_w_up,
           ffn2_w_down, final_norm):
    batch, seq, _ = x.shape
    tokens = batch * seq
    assert x.shape[2] == D_MODEL and tokens % TM == 0 and seq % TM == 0 and TM % CHUNK == 0

    row = lambda p: p[:, None, :]
    bf = lambda w: w.astype(BF16)
    ffn1 = (row(norm_ffn1), bf(ffn1_w_gate), bf(ffn1_w_up), bf(ffn1_w_down))
    ffn2 = (row(norm_ffn2), bf(ffn2_w_gate), bf(ffn2_w_up), bf(ffn2_w_down))
    w_in_b, w_out_b, sgu_w_b = bf(w_in), bf(w_out), bf(sgu_w)
    nm = row(norm_mix)
    lng, lnb = row(sgu_ln_g), row(sgu_ln_b)
    ga, gs = row(out_norm_attn), row(out_norm_sgu)
    sgu_b_col = sgu_b[..., None]
    fgain = final_norm[None, :]
    rope = _rope_tables(seq)

    xt = x.reshape(tokens, D_MODEL)
    for layer in range(DEPTH):
        xt, qs, ks, vs, sg = _ffn_inproj(xt, ffn1, nm, w_in_b, rope, lng, lnb, sgu_w_b, sgu_b_col,
                                         layer, seq)
        attn = []
        for q, k, v, dil in zip(qs, ks, vs, DILATIONS):
            attn.extend(_attention(q, k, v, dil, seq))
        xt = _merge_ffn(xt, attn, sg, ga, gs, w_out_b, ffn2, fgain, layer, layer == DEPTH - 1)
    return xt.reshape(batch, seq, D_MODEL)
```

```python
import functools
import math

import numpy as np
import jax
import jax.numpy as jnp
from jax import lax
from jax.experimental import pallas as pl
from jax.experimental.pallas import tpu as pltpu

F32 = jnp.float32
BF16 = jnp.bfloat16

D_MODEL = 1024
DEPTH = 4
HEAD_DIM = 64
D_ATTN = 512
D_SGU = 512
N_SGU_GROUPS = 4
CHUNK = 128
D_IN = 3 * D_ATTN + 2 * D_SGU
D_FF = 2816
DILATIONS = (1, 4, 16)
N_PAT = len(DILATIONS)
BAND = 64
ROPE_THETA = 500000.0
ROPE_DIM = HEAD_DIM // 4
EPS = 1e-6
NEG_INF = -1e30
LOG2_E = math.log2(math.e)
LN_2 = math.log(2.0)

LANES = 128
SUBLANES = 8
V7X_VMEM_BYTES = 64 * 1024 * 1024
VMEM_LIMIT = V7X_VMEM_BYTES * 7 // 8

TM = 512
FF_CHUNK = 256
N_FF_CHUNKS = D_FF // FF_CHUNK
N_SLABS = D_ATTN // LANES
MAX_ATT_ROWS = 2048
QB = 2 * BAND
KW = QB + 2 * BAND


def _rms(x, gain):
    return x * lax.rsqrt(jnp.mean(x * x, axis=-1, keepdims=True) + EPS) * gain


def _params(n_axes):
    return pltpu.CompilerParams(dimension_semantics=("arbitrary",) * n_axes,
                                vmem_limit_bytes=VMEM_LIMIT)


def _resident(shape, layer):
    nd = len(shape)
    return pl.BlockSpec((None,) + shape, lambda *_: (layer,) + (0,) * nd,
                        pipeline_mode=pl.Buffered(1))


def _resident_weight(shape):
    return pl.BlockSpec(shape, lambda *_: (0,) * len(shape), pipeline_mode=pl.Buffered(1))


def _view_shape(tokens, dil, dtype):
    return jax.ShapeDtypeStruct((tokens // dil, dil * D_ATTN), dtype)


def _ordering_zero(values):
    acc = None
    for v in values:
        bits = pltpu.bitcast(v[0:SUBLANES, 0:LANES], jnp.uint32)
        acc = bits if acc is None else acc | bits
    return ((acc >> 16) >> 16)[0:1, :].astype(F32)


def _ffn_apply(x, gain_ref, wg_ref, wu_ref, wd_ref, act_ref, side_work=()):
    h = _rms(x, gain_ref[...]).astype(BF16)
    for c in range(N_FF_CHUNKS):
        cols = slice(c * FF_CHUNK, (c + 1) * FF_CHUNK)
        g = jnp.dot(h, wg_ref[:, cols], preferred_element_type=F32)
        u = jnp.dot(h, wu_ref[:, cols], preferred_element_type=F32)
        a = g * jax.nn.sigmoid(g) * u
        todo = side_work[c * len(side_work) // N_FF_CHUNKS:(c + 1) * len(side_work) // N_FF_CHUNKS]
        if todo:
            zero = _ordering_zero([v for work in todo for v in work()])
            a = a + jnp.concatenate([zero] * (FF_CHUNK // LANES), axis=1)
        act_ref[:, cols] = a.astype(BF16)
    return x + 0.5 * jnp.dot(act_ref[...], wd_ref[...], preferred_element_type=F32)


def _ffn_specs(layer):
    return [_resident((1, D_MODEL), layer), _resident_weight((D_MODEL, D_FF)),
            _resident_weight((D_MODEL, D_FF)), _resident_weight((D_FF, D_MODEL))]


def _gelu(x):
    return 0.5 * x * (1.0 + lax.erf(x * (1.0 / math.sqrt(2.0))))


def _inproj_matmul(x, gain_ref, w_ref, raw_ref):
    h = _rms(x, gain_ref[...]).astype(BF16)
    for col0 in range(0, D_IN, D_ATTN):
        raw_ref[:, col0:col0 + D_ATTN] = jnp.dot(h, w_ref[:, col0:col0 + D_ATTN],
                                                 preferred_element_type=F32)


def _inproj_finish_work(raw_ref, cos_ref, sina_ref, sinb_ref, lng_ref, lnb_ref, sw_ref, sb_ref,
                        q_refs, k_refs, v_refs, sg_ref, stage_ref):
    def qkv_slab(col0, dst_refs, rotary, scale, s):
        def work():
            t = raw_ref[:, col0 + s * LANES:col0 + (s + 1) * LANES]
            if rotary:
                half = ROPE_DIM // 2
                t = (t * cos_ref[...] + pltpu.roll(t, half, 1) * sina_ref[...]
                     + pltpu.roll(t, LANES - half, 1) * sinb_ref[...])
            if scale != 1.0:
                t = t * scale
            dst_refs[0][:, s * LANES:(s + 1) * LANES] = t.astype(BF16)
            stage_ref[0, s] = t
            produced = [t]
            for k in range(1, N_PAT):
                d_prev, dil = DILATIONS[k - 1], DILATIONS[k]
                ratio, n_prev, n = dil // d_prev, TM // d_prev, TM // dil
                for e in range(d_prev):
                    for c in range(ratio):
                        r = c * d_prev + e
                        blk = stage_ref[k - 1, s, pl.ds(e * n_prev + c, n, stride=ratio), :]
                        dst_refs[k][:, r * D_ATTN + s * LANES:r * D_ATTN + (s + 1) * LANES] = (
                            blk.astype(BF16))
                        if k + 1 < N_PAT:
                            stage_ref[k, s, r * n:(r + 1) * n, :] = blk
                        produced.append(blk)
            return produced
        return work

    def sgu_chunk(c):
        def work():
            rows = slice(c * CHUNK, (c + 1) * CHUNK)
            u = _gelu(raw_ref[rows, 3 * D_ATTN:3 * D_ATTN + D_SGU])
            g = _gelu(raw_ref[rows, 3 * D_ATTN + D_SGU:])
            gc = g - jnp.mean(g, axis=-1, keepdims=True)
            gn = (gc * lax.rsqrt(jnp.mean(gc * gc, axis=-1, keepdims=True) + EPS) * lng_ref[...]
                  + lnb_ref[...]).astype(BF16)
            group = D_SGU // N_SGU_GROUPS
            produced = []
            for gi in range(N_SGU_GROUPS):
                cols = slice(gi * group, (gi + 1) * group)
                mixed = jnp.dot(sw_ref[gi], gn[:, cols], preferred_element_type=F32) + sb_ref[gi]
                gated = u[:, cols] * mixed
                sg_ref[rows, cols] = gated.astype(BF16)
                produced.append(gated)
            return produced
        return work

    q_scale = LOG2_E / math.sqrt(HEAD_DIM)
    pieces = [qkv_slab(0, q_refs, True, q_scale, s) for s in range(N_SLABS)]
    pieces += [qkv_slab(D_ATTN, k_refs, True, 1.0, s) for s in range(N_SLABS)]
    pieces += [qkv_slab(2 * D_ATTN, v_refs, False, 1.0, s) for s in range(N_SLABS)]
    pieces += [sgu_chunk(c) for c in range(TM // CHUNK)]
    return pieces


def _ffn_inproj_kernel(x_ref, *refs, n_tiles):
    ffn_refs, gain_ref, w_ref, finish_refs = refs[:4], refs[4], refs[5], refs[6:13]
    xo_ref = refs[13]
    q_refs, k_refs, v_refs = (refs[14 + i * N_PAT:14 + (i + 1) * N_PAT] for i in range(3))
    sg_ref, act_ref, stage_ref, raw_ref = refs[14 + 3 * N_PAT:]
    step = pl.program_id(0)
    finish = _inproj_finish_work(raw_ref, *finish_refs, q_refs, k_refs, v_refs, sg_ref, stage_ref)

    @pl.when(step == 0)
    def _():
        raw_ref[...] = jnp.zeros_like(raw_ref)

    @pl.when(step < n_tiles)
    def _():
        x = _ffn_apply(x_ref[...], *ffn_refs, act_ref, side_work=finish)
        xo_ref[...] = x
        _inproj_matmul(x, gain_ref, w_ref, raw_ref)

    @pl.when(step == n_tiles)
    def _():
        for work in finish:
            work()


def _ffn_inproj(x, ffn_w, gain, w_in, rope, lng, lnb, sw, sb, layer, seq):
    tokens = x.shape[0]
    n_tiles = tokens // TM
    lead = lambda s: (jnp.minimum(s, n_tiles - 1), 0)
    lag = lambda s: (jnp.maximum(s - 1, 0), 0)
    table = pl.BlockSpec((TM, LANES), lambda s: (jnp.maximum(s - 1, 0) % (seq // TM), 0))
    views = [pl.BlockSpec((TM // d, d * D_ATTN), lag) for d in DILATIONS]
    view_shapes = [_view_shape(tokens, d, BF16) for d in DILATIONS]
    outs = pl.pallas_call(
        functools.partial(_ffn_inproj_kernel, n_tiles=n_tiles),
        grid=(n_tiles + 1,),
        in_specs=[pl.BlockSpec((TM, D_MODEL), lead)] + _ffn_specs(layer)
                 + [_resident((1, D_MODEL), layer), _resident_weight((D_MODEL, D_IN)),
                    table, table, table,
                    _resident((1, D_SGU), layer), _resident((1, D_SGU), layer),
                    _resident((N_SGU_GROUPS, CHUNK, CHUNK), layer),
                    _resident((N_SGU_GROUPS, CHUNK, 1), layer)],
        out_specs=[pl.BlockSpec((TM, D_MODEL), lead)] + views * 3 + [pl.BlockSpec((TM, D_SGU), lag)],
        out_shape=[jax.ShapeDtypeStruct((tokens, D_MODEL), F32)] + view_shapes * 3
                  + [jax.ShapeDtypeStruct((tokens, D_SGU), BF16)],
        scratch_shapes=[pltpu.VMEM((TM, D_FF), BF16), pltpu.VMEM((N_PAT - 1, N_SLABS, TM, LANES), F32),
                        pltpu.VMEM((TM, D_IN), F32)],
        compiler_params=_params(1),
        name="ffn_inproj",
    )(x, *ffn_w, gain, w_in, *rope, lng, lnb, sw, sb)
    return (outs[0], outs[1:1 + N_PAT], outs[1 + N_PAT:1 + 2 * N_PAT],
            outs[1 + 2 * N_PAT:1 + 3 * N_PAT], outs[-1])


def _rope_tables(seq):
    half = ROPE_DIM // 2
    inv_freq = ROPE_THETA ** (-np.arange(0, ROPE_DIM, 2, dtype=np.float64) / ROPE_DIM)
    ang = np.arange(seq, dtype=np.float64)[:, None] * inv_freq[None, :]
    cos = np.ones((seq, LANES))
    sina = np.zeros((seq, LANES))
    sinb = np.zeros((seq, LANES))
    for head0 in range(0, LANES, HEAD_DIM):
        cos[:, head0:head0 + half] = np.cos(ang)
        cos[:, head0 + half:head0 + 2 * half] = np.cos(ang)
        sinb[:, head0:head0 + half] = -np.sin(ang)
        sina[:, head0 + half:head0 + 2 * half] = np.sin(ang)
    return tuple(jnp.asarray(t, F32) for t in (cos, sina, sinb))


def _attn_kernel(q_ref, kc_ref, kp_ref, kn_ref, vc_ref, vp_ref, vn_ref, *refs,
                 sub_len, rows, n_res, n_cast):
    cast_src, (o_ref, lse_ref) = refs[:n_cast], refs[n_cast:n_cast + 2]
    cast_dst, (kbuf, vbuf) = refs[n_cast + 2:2 * n_cast + 2], refs[2 * n_cast + 2:]
    for src, dst in zip(cast_src, cast_dst):
        dst[...] = src[...].astype(BF16)

    ones = jnp.ones((rows + 2 * BAND, LANES), BF16)
    for res in range(n_res):
        c0 = res * D_ATTN
        kbuf[res, 0:BAND] = kp_ref[:, c0:c0 + D_ATTN]
        kbuf[res, BAND:BAND + rows] = kc_ref[:, c0:c0 + D_ATTN]
        kbuf[res, BAND + rows:] = kn_ref[:, c0:c0 + D_ATTN]
        for pair in range(N_SLABS):
            src = slice(c0 + pair * LANES, c0 + (pair + 1) * LANES)
            dst = slice(2 * pair * LANES, (2 * pair + 1) * LANES)
            vbuf[res, 0:BAND, dst] = vp_ref[:, src]
            vbuf[res, BAND:BAND + rows, dst] = vc_ref[:, src]
            vbuf[res, BAND + rows:, dst] = vn_ref[:, src]
            vbuf[res, :, (2 * pair + 1) * LANES:(2 * pair + 2) * LANES] = ones

    first_row = (pl.program_id(0) % (sub_len // rows)) * rows
    delta = (lax.broadcasted_iota(jnp.int32, (QB, KW), 1)
             - lax.broadcasted_iota(jnp.int32, (QB, KW), 0))
    band_bias = jnp.where((delta >= 0) & (delta <= 2 * BAND), 0.0, NEG_INF)
    key_col = lax.broadcasted_iota(jnp.int32, (1, KW), 1)
    lane = lax.broadcasted_iota(jnp.int32, (1, LANES), 1)
    low_head = lane < HEAD_DIM

    for j in range(rows // QB):
        row0 = j * QB
        key_pos = key_col + (first_row + row0 - BAND)
        bias = band_bias + jnp.where((key_pos >= 0) & (key_pos < sub_len), 0.0, NEG_INF)
        bias = jnp.concatenate([bias, bias], axis=0)
        for res in range(n_res):
            lse_tile = jnp.zeros((QB, LANES), F32)
            for pair in range(N_SLABS):
                cols = slice(res * D_ATTN + pair * LANES, res * D_ATTN + (pair + 1) * LANES)
                q2 = q_ref[row0:row0 + QB, cols]
                zero = jnp.zeros_like(q2)
                qs = jnp.concatenate([jnp.where(low_head, q2, zero), jnp.where(low_head, zero, q2)],
                                     axis=0)
                k2 = kbuf[res, row0:row0 + KW, pair * LANES:(pair + 1) * LANES]
                s = lax.dot_general(qs, k2, (((1,), (1,)), ((), ())),
                                    preferred_element_type=F32) + bias
                m = jnp.max(s, axis=-1, keepdims=True)
                p = jnp.exp2(s - m).astype(BF16)
                pv = jnp.dot(p, vbuf[res, row0:row0 + KW, 2 * pair * LANES:(2 * pair + 2) * LANES],
                             preferred_element_type=F32)
                num = jnp.where(low_head, pv[:QB, :LANES], pv[QB:, :LANES])
                den = jnp.where(low_head, pv[:QB, LANES:], pv[QB:, LANES:])
                o_ref[row0:row0 + QB, cols] = (num / den).astype(BF16)
                lse_pair = jnp.where(low_head, m[:QB], m[QB:]) * LN_2 + jnp.log(den)
                lse_tile = jnp.where((lane == pair) | (lane == HEAD_DIM + pair), lse_pair, lse_tile)
            lse_ref[row0:row0 + QB, res * LANES:(res + 1) * LANES] = lse_tile


def _attention(q, k, v, dil, seq, cast_weights=(), cast_layer=0):
    total_rows = q.shape[0]
    sub_len = seq // dil
    rows = min(MAX_ATT_ROWS, sub_len)
    n_res = min(MAX_ATT_ROWS // rows, dil)
    assert sub_len % rows == 0 and rows % QB == 0 and dil % n_res == 0
    halos_per_step = rows // BAND
    last_halo = total_rows // BAND - 1
    width = n_res * D_ATTN
    grid = (total_rows // rows, dil // n_res)
    cur = pl.BlockSpec((rows, width), lambda i, r: (i, r))
    prev = pl.BlockSpec((BAND, width), lambda i, r: (jnp.maximum(i * halos_per_step - 1, 0), r))
    nxt = pl.BlockSpec((BAND, width),
                       lambda i, r: (jnp.minimum((i + 1) * halos_per_step, last_halo), r))
    n_steps = grid[0] * grid[1]
    cast_in, cast_out, cast_shapes = [], [], []
    for w in cast_weights:
        _, w_rows, w_cols = w.shape
        chunk = w_rows // n_steps
        assert chunk * n_steps == w_rows and chunk % (2 * SUBLANES) == 0
        cast_in.append(pl.BlockSpec((None, chunk, w_cols),
                                    lambda i, r: (cast_layer, i * grid[1] + r, 0)))
        cast_out.append(pl.BlockSpec((chunk, w_cols), lambda i, r: (i * grid[1] + r, 0)))
        cast_shapes.append(jax.ShapeDtypeStruct((w_rows, w_cols), BF16))
    outs = pl.pallas_call(
        functools.partial(_attn_kernel, sub_len=sub_len, rows=rows, n_res=n_res,
                          n_cast=len(cast_weights)),
        grid=grid,
        in_specs=[cur, cur, prev, nxt, cur, prev, nxt] + cast_in,
        out_specs=[cur, pl.BlockSpec((rows, n_res * LANES), lambda i, r: (i, r))] + cast_out,
        out_shape=[jax.ShapeDtypeStruct(q.shape, BF16),
                   jax.ShapeDtypeStruct((total_rows, dil * LANES), F32)] + cast_shapes,
        scratch_shapes=[pltpu.VMEM((n_res, rows + 2 * BAND, D_ATTN), BF16),
                        pltpu.VMEM((n_res, rows + 2 * BAND, 2 * D_ATTN), BF16)],
        compiler_params=_params(2),
        name=f"attn_d{dil}",
    )(q, k, k, k, v, v, v, *cast_weights)
    return outs[0], outs[1], list(outs[2:])


def _merge_work(o_refs, l_refs, sg_ref, ga_ref, gs_ref, norm_ref, stage_refs):
    def unview_slab(s):
        def work():
            produced = []
            for p, dil in enumerate(DILATIONS[1:]):
                for r in range(dil):
                    blk = o_refs[p + 1][:, r * D_ATTN + s * LANES:r * D_ATTN + (s + 1) * LANES]
                    blk = blk.astype(F32)
                    stage_refs[2 * p][s, pl.ds(r, TM // dil, stride=dil), :] = blk
                    produced.append(blk)
            return produced
        return work

    def unview_lse():
        produced = []
        for p, dil in enumerate(DILATIONS[1:]):
            for r in range(dil):
                blk = l_refs[p + 1][:, r * LANES:(r + 1) * LANES]
                stage_refs[2 * p + 1][pl.ds(r, TM // dil, stride=dil), :] = blk
                produced.append(blk)
        return produced

    def merge_rows(c):
        def work():
            rows = slice(c * CHUNK, (c + 1) * CHUNK)
            low_head = lax.broadcasted_iota(jnp.int32, (1, LANES), 1) < HEAD_DIM
            lses = [l_refs[0][rows, :]] + [stage_refs[2 * p + 1][rows, :] for p in range(N_PAT - 1)]
            m = functools.reduce(jnp.maximum, lses)
            es = [jnp.exp(l - m) for l in lses]
            inv = 1.0 / sum(es)
            slabs = []
            for s in range(N_SLABS):
                cols = slice(s * LANES, (s + 1) * LANES)
                outs = [o_refs[0][rows, cols].astype(F32)]
                outs += [stage_refs[2 * p][s, rows, :] for p in range(N_PAT - 1)]
                acc = None
                for e, o in zip(es, outs):
                    w = e * inv
                    w = jnp.where(low_head, w[:, s:s + 1], w[:, HEAD_DIM + s:HEAD_DIM + s + 1])
                    acc = w * o if acc is None else acc + w * o
                slabs.append(acc)
            na = _rms(jnp.concatenate(slabs, axis=1), ga_ref[...])
            ns = _rms(sg_ref[rows, :].astype(F32), gs_ref[...])
            norm_ref[rows, 0:D_ATTN] = na.astype(BF16)
            norm_ref[rows, D_ATTN:] = ns.astype(BF16)
            return [na, ns]
        return work

    return ([unview_slab(s) for s in range(N_SLABS)] + [unview_lse]
            + [merge_rows(c) for c in range(TM // CHUNK)])


def _merge_ffn_kernel(x_ref, *refs, n_tiles, final):
    o_refs, l_refs = refs[0:2 * N_PAT:2], refs[1:2 * N_PAT:2]
    sg_ref, ga_ref, gs_ref, wo_ref = refs[2 * N_PAT:2 * N_PAT + 4]
    ffn_refs = refs[2 * N_PAT + 4:2 * N_PAT + 8]
    fgain_ref, out_ref, act_ref, norm_ref = refs[2 * N_PAT + 8:2 * N_PAT + 12]
    stage_refs = refs[2 * N_PAT + 12:]
    step = pl.program_id(0)
    merge = _merge_work(o_refs, l_refs, sg_ref, ga_ref, gs_ref, norm_ref, stage_refs)

    def project_and_ffn(side_work):
        x = x_ref[...] + jnp.dot(norm_ref[...], wo_ref[...], preferred_element_type=F32)
        y = _ffn_apply(x, *ffn_refs, act_ref, side_work=side_work)
        out_ref[...] = _rms(y, fgain_ref[...]) if final else y

    @pl.when(step == 0)
    def _():
        for work in merge:
            work()

    @pl.when((step > 0) & (step < n_tiles))
    def _():
        project_and_ffn(merge)

    @pl.when(step == n_tiles)
    def _():
        project_and_ffn(())


def _merge_ffn(x, attn, sg, ga, gs, wo, ffn_w, fgain, layer, final):
    tokens = x.shape[0]
    n_tiles = tokens // TM
    lead = lambda s: (jnp.minimum(s, n_tiles - 1), 0)
    lag = lambda s: (jnp.maximum(s - 1, 0), 0)
    views = [pl.BlockSpec((TM // d, d * width), lead) for d in DILATIONS for width in (D_ATTN, LANES)]
    return pl.pallas_call(
        functools.partial(_merge_ffn_kernel, n_tiles=n_tiles, final=final),
        grid=(n_tiles + 1,),
        in_specs=[pl.BlockSpec((TM, D_MODEL), lag)] + views + [pl.BlockSpec((TM, D_SGU), lead)]
                 + [_resident((1, D_ATTN), layer), _resident((1, D_SGU), layer),
                    _resident_weight((D_ATTN + D_SGU, D_MODEL))]
                 + _ffn_specs(layer) + [pl.BlockSpec((1, D_MODEL), lambda s: (0, 0))],
        out_specs=pl.BlockSpec((TM, D_MODEL), lag),
        out_shape=jax.ShapeDtypeStruct((tokens, D_MODEL), F32),
        scratch_shapes=[pltpu.VMEM((TM, D_FF), BF16), pltpu.VMEM((TM, D_ATTN + D_SGU), BF16)]
                       + [pltpu.VMEM((N_SLABS, TM, LANES), F32), pltpu.VMEM((TM, LANES), F32)] * (N_PAT - 1),
        compiler_params=_params(1),
        name="merge_ffn",
    )(x, *attn, sg, ga, gs, wo, *ffn_w, fgain)


def kernel(x, norm_ffn1, ffn1_w_gate, ffn1_w_up, ffn1_w_down, norm_mix, w_in, sgu_ln_g, sgu_ln_b,
           sgu_w, sgu_b, out_norm_attn, out_norm_sgu, w_out, norm_ffn2, ffn2_w_gate, ffn2_w_up,
           ffn2_w_down, final_norm):
    batch, seq, _ = x.shape
    tokens = batch * seq
    assert x.shape[2] == D_MODEL and tokens % TM == 0 and seq % TM == 0 and TM % CHUNK == 0

    row = lambda p: p[:, None, :]
    n1, n2, nm = row(norm_ffn1), row(norm_ffn2), row(norm_mix)
    lng, lnb = row(sgu_ln_g), row(sgu_ln_b)
    ga, gs = row(out_norm_attn), row(out_norm_sgu)
    sgu_w_b = sgu_w.astype(BF16)
    sgu_b_col = sgu_b[..., None]
    fgain = final_norm[None, :]
    rope = _rope_tables(seq)

    cast_groups = ((ffn1_w_gate, ffn1_w_up, ffn1_w_down), (ffn2_w_gate, ffn2_w_up, ffn2_w_down),
                   (w_in, w_out))
    weights = [[w[0].astype(BF16) for w in group] for group in cast_groups]

    xt = x.reshape(tokens, D_MODEL)
    for layer in range(DEPTH):
        ffn1_w, ffn2_w, (w_in_b, w_out_b) = weights
        xt, qs, ks, vs, sg = _ffn_inproj(xt, (n1, *ffn1_w), nm, w_in_b, rope, lng, lnb, sgu_w_b,
                                         sgu_b_col, layer, seq)
        attn, weights = [], []
        for q, k, v, dil, group in zip(qs, ks, vs, DILATIONS, cast_groups):
            o, lse, cast = _attention(q, k, v, dil, seq, group if layer + 1 < DEPTH else (), layer + 1)
            attn += [o, lse]
            weights.append(cast)
        xt = _merge_ffn(xt, attn, sg, ga, gs, w_out_b, (n2, *ffn2_w), fgain, layer,
                        layer == DEPTH - 1)
    return xt.reshape(batch, seq, D_MODEL)
```

```python
import functools
import math

import numpy as np
import jax
import jax.numpy as jnp
from jax import lax
from jax.experimental import pallas as pl
from jax.experimental.pallas import tpu as pltpu

F32 = jnp.float32
BF16 = jnp.bfloat16

D_MODEL = 1024
DEPTH = 4
HEAD_DIM = 64
D_ATTN = 512
D_SGU = 512
N_SGU_GROUPS = 4
CHUNK = 128
D_IN = 3 * D_ATTN + 2 * D_SGU
D_FF = 2816
DILATIONS = (1, 4, 16)
N_PAT = len(DILATIONS)
BAND = 64
ROPE_THETA = 500000.0
ROPE_DIM = HEAD_DIM // 4
EPS = 1e-6
NEG_INF = -1e30
LOG2_E = math.log2(math.e)
LN_2 = math.log(2.0)

LANES = 128
SUBLANES = 8
V7X_VMEM_BYTES = 64 * 1024 * 1024
VMEM_LIMIT = V7X_VMEM_BYTES * 7 // 8

TM = 512
FF_CHUNK = 256
N_FF_CHUNKS = D_FF // FF_CHUNK
N_SLABS = D_ATTN // LANES
MAX_ATT_ROWS = 2048
QB = BAND
KW = QB + 2 * BAND


def _rms(x, gain):
    return x * lax.rsqrt(jnp.mean(x * x, axis=-1, keepdims=True) + EPS) * gain


def _params(n_axes):
    return pltpu.CompilerParams(dimension_semantics=("arbitrary",) * n_axes,
                                vmem_limit_bytes=VMEM_LIMIT)


def _resident(shape, layer):
    nd = len(shape)
    return pl.BlockSpec((None,) + shape, lambda *_: (layer,) + (0,) * nd,
                        pipeline_mode=pl.Buffered(1))


def _resident_weight(shape):
    return pl.BlockSpec(shape, lambda *_: (0,) * len(shape), pipeline_mode=pl.Buffered(1))


def _view_shape(tokens, dil, dtype):
    return jax.ShapeDtypeStruct((tokens // dil, dil * D_ATTN), dtype)


def _ordering_zero(values):
    acc = None
    for v in values:
        bits = pltpu.bitcast(v[0:SUBLANES, 0:LANES], jnp.uint32)
        acc = bits if acc is None else acc | bits
    return ((acc >> 16) >> 16)[0:1, :].astype(F32)


def _ffn_apply(x, gain_ref, wg_ref, wu_ref, wd_ref, act_ref, side_work=()):
    h = _rms(x, gain_ref[...]).astype(BF16)
    for c in range(N_FF_CHUNKS):
        cols = slice(c * FF_CHUNK, (c + 1) * FF_CHUNK)
        g = jnp.dot(h, wg_ref[:, cols], preferred_element_type=F32)
        u = jnp.dot(h, wu_ref[:, cols], preferred_element_type=F32)
        a = g * jax.nn.sigmoid(g) * u
        todo = side_work[c * len(side_work) // N_FF_CHUNKS:(c + 1) * len(side_work) // N_FF_CHUNKS]
        if todo:
            zero = _ordering_zero([v for work in todo for v in work()])
            a = a + jnp.concatenate([zero] * (FF_CHUNK // LANES), axis=1)
        act_ref[:, cols] = a.astype(BF16)
    return x + 0.5 * jnp.dot(act_ref[...], wd_ref[...], preferred_element_type=F32)


def _ffn_specs(layer):
    return [_resident((1, D_MODEL), layer), _resident_weight((D_MODEL, D_FF)),
            _resident_weight((D_MODEL, D_FF)), _resident_weight((D_FF, D_MODEL))]


def _gelu(x):
    return 0.5 * x * (1.0 + lax.erf(x * (1.0 / math.sqrt(2.0))))


def _inproj_matmul(x, gain_ref, w_ref, raw_ref):
    h = _rms(x, gain_ref[...]).astype(BF16)
    for col0 in range(0, D_IN, D_ATTN):
        raw_ref[:, col0:col0 + D_ATTN] = jnp.dot(h, w_ref[:, col0:col0 + D_ATTN],
                                                 preferred_element_type=F32)


def _inproj_finish_work(raw_ref, cos_ref, sina_ref, sinb_ref, lng_ref, lnb_ref, sw_ref, sb_ref,
                        q_refs, k_refs, v_refs, sg_ref, stage_ref):
    def qkv_slab(col0, dst_refs, rotary, scale, s):
        def work():
            t = raw_ref[:, col0 + s * LANES:col0 + (s + 1) * LANES]
            if rotary:
                half = ROPE_DIM // 2
                t = (t * cos_ref[...] + pltpu.roll(t, half, 1) * sina_ref[...]
                     + pltpu.roll(t, LANES - half, 1) * sinb_ref[...])
            if scale != 1.0:
                t = t * scale
            dst_refs[0][:, s * LANES:(s + 1) * LANES] = t.astype(BF16)
            stage_ref[0, s] = t
            produced = [t]
            for k in range(1, N_PAT):
                d_prev, dil = DILATIONS[k - 1], DILATIONS[k]
                ratio, n_prev, n = dil // d_prev, TM // d_prev, TM // dil
                for e in range(d_prev):
                    for c in range(ratio):
                        r = c * d_prev + e
                        blk = stage_ref[k - 1, s, pl.ds(e * n_prev + c, n, stride=ratio), :]
                        dst_refs[k][:, r * D_ATTN + s * LANES:r * D_ATTN + (s + 1) * LANES] = (
                            blk.astype(BF16))
                        if k + 1 < N_PAT:
                            stage_ref[k, s, r * n:(r + 1) * n, :] = blk
                        produced.append(blk)
            return produced
        return work

    def sgu_chunk(c):
        def work():
            rows = slice(c * CHUNK, (c + 1) * CHUNK)
            u = _gelu(raw_ref[rows, 3 * D_ATTN:3 * D_ATTN + D_SGU])
            g = _gelu(raw_ref[rows, 3 * D_ATTN + D_SGU:])
            gc = g - jnp.mean(g, axis=-1, keepdims=True)
            gn = (gc * lax.rsqrt(jnp.mean(gc * gc, axis=-1, keepdims=True) + EPS) * lng_ref[...]
                  + lnb_ref[...]).astype(BF16)
            group = D_SGU // N_SGU_GROUPS
            produced = []
            for gi in range(N_SGU_GROUPS):
                cols = slice(gi * group, (gi + 1) * group)
                mixed = jnp.dot(sw_ref[gi], gn[:, cols], preferred_element_type=F32) + sb_ref[gi]
                gated = u[:, cols] * mixed
                sg_ref[rows, cols] = gated.astype(BF16)
                produced.append(gated)
            return produced
        return work

    q_scale = LOG2_E / math.sqrt(HEAD_DIM)
    pieces = [qkv_slab(0, q_refs, True, q_scale, s) for s in range(N_SLABS)]
    pieces += [qkv_slab(D_ATTN, k_refs, True, 1.0, s) for s in range(N_SLABS)]
    pieces += [qkv_slab(2 * D_ATTN, v_refs, False, 1.0, s) for s in range(N_SLABS)]
    pieces += [sgu_chunk(c) for c in range(TM // CHUNK)]
    return pieces


def _ffn_inproj_kernel(x_ref, *refs, n_tiles):
    ffn_refs, gain_ref, w_ref, finish_refs = refs[:4], refs[4], refs[5], refs[6:13]
    xo_ref = refs[13]
    q_refs, k_refs, v_refs = (refs[14 + i * N_PAT:14 + (i + 1) * N_PAT] for i in range(3))
    sg_ref, act_ref, stage_ref, raw_ref = refs[14 + 3 * N_PAT:]
    step = pl.program_id(0)
    finish = _inproj_finish_work(raw_ref, *finish_refs, q_refs, k_refs, v_refs, sg_ref, stage_ref)

    @pl.when(step == 0)
    def _():
        raw_ref[...] = jnp.zeros_like(raw_ref)

    @pl.when(step < n_tiles)
    def _():
        x = _ffn_apply(x_ref[...], *ffn_refs, act_ref, side_work=finish)
        xo_ref[...] = x
        _inproj_matmul(x, gain_ref, w_ref, raw_ref)

    @pl.when(step == n_tiles)
    def _():
        for work in finish:
            work()


def _ffn_inproj(x, ffn_w, gain, w_in, rope, lng, lnb, sw, sb, layer, seq):
    tokens = x.shape[0]
    n_tiles = tokens // TM
    lead = lambda s: (jnp.minimum(s, n_tiles - 1), 0)
    lag = lambda s: (jnp.maximum(s - 1, 0), 0)
    table = pl.BlockSpec((TM, LANES), lambda s: (jnp.maximum(s - 1, 0) % (seq // TM), 0))
    views = [pl.BlockSpec((TM // d, d * D_ATTN), lag) for d in DILATIONS]
    view_shapes = [_view_shape(tokens, d, BF16) for d in DILATIONS]
    outs = pl.pallas_call(
        functools.partial(_ffn_inproj_kernel, n_tiles=n_tiles),
        grid=(n_tiles + 1,),
        in_specs=[pl.BlockSpec((TM, D_MODEL), lead)] + _ffn_specs(layer)
                 + [_resident((1, D_MODEL), layer), _resident_weight((D_MODEL, D_IN)),
                    table, table, table,
                    _resident((1, D_SGU), layer), _resident((1, D_SGU), layer),
                    _resident((N_SGU_GROUPS, CHUNK, CHUNK), layer),
                    _resident((N_SGU_GROUPS, CHUNK, 1), layer)],
        out_specs=[pl.BlockSpec((TM, D_MODEL), lead)] + views * 3 + [pl.BlockSpec((TM, D_SGU), lag)],
        out_shape=[jax.ShapeDtypeStruct((tokens, D_MODEL), F32)] + view_shapes * 3
                  + [jax.ShapeDtypeStruct((tokens, D_SGU), BF16)],
        scratch_shapes=[pltpu.VMEM((TM, D_FF), BF16), pltpu.VMEM((N_PAT - 1, N_SLABS, TM, LANES), F32),
                        pltpu.VMEM((TM, D_IN), F32)],
        compiler_params=_params(1),
        name="ffn_inproj",
    )(x, *ffn_w, gain, w_in, *rope, lng, lnb, sw, sb)
    return (outs[0], outs[1:1 + N_PAT], outs[1 + N_PAT:1 + 2 * N_PAT],
            outs[1 + 2 * N_PAT:1 + 3 * N_PAT], outs[-1])


def _rope_tables(seq):
    half = ROPE_DIM // 2
    inv_freq = ROPE_THETA ** (-np.arange(0, ROPE_DIM, 2, dtype=np.float64) / ROPE_DIM)
    ang = np.arange(seq, dtype=np.float64)[:, None] * inv_freq[None, :]
    cos = np.ones((seq, LANES))
    sina = np.zeros((seq, LANES))
    sinb = np.zeros((seq, LANES))
    for head0 in range(0, LANES, HEAD_DIM):
        cos[:, head0:head0 + half] = np.cos(ang)
        cos[:, head0 + half:head0 + 2 * half] = np.cos(ang)
        sinb[:, head0:head0 + half] = -np.sin(ang)
        sina[:, head0 + half:head0 + 2 * half] = np.sin(ang)
    return tuple(jnp.asarray(t, F32) for t in (cos, sina, sinb))


def _attn_kernel(q_ref, kc_ref, kp_ref, kn_ref, vc_ref, vp_ref, vn_ref, *refs,
                 sub_len, rows, n_res, n_cast):
    cast_src, (o_ref, lse_ref) = refs[:n_cast], refs[n_cast:n_cast + 2]
    cast_dst, (kbuf, vbuf) = refs[n_cast + 2:2 * n_cast + 2], refs[2 * n_cast + 2:]
    for src, dst in zip(cast_src, cast_dst):
        dst[...] = src[...].astype(BF16)

    ones = jnp.ones((rows + 2 * BAND, LANES), BF16)
    for res in range(n_res):
        c0 = res * D_ATTN
        kbuf[res, 0:BAND] = kp_ref[:, c0:c0 + D_ATTN]
        kbuf[res, BAND:BAND + rows] = kc_ref[:, c0:c0 + D_ATTN]
        kbuf[res, BAND + rows:] = kn_ref[:, c0:c0 + D_ATTN]
        for pair in range(N_SLABS):
            src = slice(c0 + pair * LANES, c0 + (pair + 1) * LANES)
            dst = slice(2 * pair * LANES, (2 * pair + 1) * LANES)
            vbuf[res, 0:BAND, dst] = vp_ref[:, src]
            vbuf[res, BAND:BAND + rows, dst] = vc_ref[:, src]
            vbuf[res, BAND + rows:, dst] = vn_ref[:, src]
            vbuf[res, :, (2 * pair + 1) * LANES:(2 * pair + 2) * LANES] = ones

    first_row = (pl.program_id(0) % (sub_len // rows)) * rows
    delta = (lax.broadcasted_iota(jnp.int32, (QB, KW), 1)
             - lax.broadcasted_iota(jnp.int32, (QB, KW), 0))
    band_bias = jnp.where((delta >= 0) & (delta <= 2 * BAND), 0.0, NEG_INF)
    key_col = lax.broadcasted_iota(jnp.int32, (1, KW), 1)
    lane = lax.broadcasted_iota(jnp.int32, (1, LANES), 1)
    low_head = lane < HEAD_DIM

    for j in range(rows // QB):
        row0 = j * QB
        key_pos = key_col + (first_row + row0 - BAND)
        bias = band_bias + jnp.where((key_pos >= 0) & (key_pos < sub_len), 0.0, NEG_INF)
        bias = jnp.concatenate([bias, bias], axis=0)
        for res in range(n_res):
            lse_tile = jnp.zeros((QB, LANES), F32)
            for pair in range(N_SLABS):
                cols = slice(res * D_ATTN + pair * LANES, res * D_ATTN + (pair + 1) * LANES)
                q2 = q_ref[row0:row0 + QB, cols]
                zero = jnp.zeros_like(q2)
                qs = jnp.concatenate([jnp.where(low_head, q2, zero), jnp.where(low_head, zero, q2)],
                                     axis=0)
                k2 = kbuf[res, row0:row0 + KW, pair * LANES:(pair + 1) * LANES]
                s = lax.dot_general(qs, k2, (((1,), (1,)), ((), ())),
                                    preferred_element_type=F32) + bias
                m = jnp.max(s, axis=-1, keepdims=True)
                p = jnp.exp2(s - m).astype(BF16)
                pv = jnp.dot(p, vbuf[res, row0:row0 + KW, 2 * pair * LANES:(2 * pair + 2) * LANES],
                             preferred_element_type=F32)
                num = jnp.where(low_head, pv[:QB, :LANES], pv[QB:, :LANES])
                den = jnp.where(low_head, pv[:QB, LANES:], pv[QB:, LANES:])
                o_ref[row0:row0 + QB, cols] = (num / den).astype(BF16)
                lse_pair = jnp.where(low_head, m[:QB], m[QB:]) * LN_2 + jnp.log(den)
                lse_tile = jnp.where((lane == pair) | (lane == HEAD_DIM + pair), lse_pair, lse_tile)
            lse_ref[row0:row0 + QB, res * LANES:(res + 1) * LANES] = lse_tile


def _attention(q, k, v, dil, seq, cast_weights=(), cast_layer=0):
    total_rows = q.shape[0]
    sub_len = seq // dil
    rows = min(MAX_ATT_ROWS, sub_len)
    n_res = min(MAX_ATT_ROWS // rows, dil)
    assert sub_len % rows == 0 and rows % QB == 0 and dil % n_res == 0
    halos_per_step = rows // BAND
    last_halo = total_rows // BAND - 1
    width = n_res * D_ATTN
    grid = (total_rows // rows, dil // n_res)
    cur = pl.BlockSpec((rows, width), lambda i, r: (i, r))
    prev = pl.BlockSpec((BAND, width), lambda i, r: (jnp.maximum(i * halos_per_step - 1, 0), r))
    nxt = pl.BlockSpec((BAND, width),
                       lambda i, r: (jnp.minimum((i + 1) * halos_per_step, last_halo), r))
    n_steps = grid[0] * grid[1]
    cast_in, cast_out, cast_shapes = [], [], []
    for w in cast_weights:
        _, w_rows, w_cols = w.shape
        chunk = w_rows // n_steps
        assert chunk * n_steps == w_rows and chunk % (2 * SUBLANES) == 0
        cast_in.append(pl.BlockSpec((None, chunk, w_cols),
                                    lambda i, r: (cast_layer, i * grid[1] + r, 0)))
        cast_out.append(pl.BlockSpec((chunk, w_cols), lambda i, r: (i * grid[1] + r, 0)))
        cast_shapes.append(jax.ShapeDtypeStruct((w_rows, w_cols), BF16))
    outs = pl.pallas_call(
        functools.partial(_attn_kernel, sub_len=sub_len, rows=rows, n_res=n_res,
                          n_cast=len(cast_weights)),
        grid=grid,
        in_specs=[cur, cur, prev, nxt, cur, prev, nxt] + cast_in,
        out_specs=[cur, pl.BlockSpec((rows, n_res * LANES), lambda i, r: (i, r))] + cast_out,
        out_shape=[jax.ShapeDtypeStruct(q.shape, BF16),
                   jax.ShapeDtypeStruct((total_rows, dil * LANES), F32)] + cast_shapes,
        scratch_shapes=[pltpu.VMEM((n_res, rows + 2 * BAND, D_ATTN), BF16),
                        pltpu.VMEM((n_res, rows + 2 * BAND, 2 * D_ATTN), BF16)],
        compiler_params=_params(2),
        name=f"attn_d{dil}",
    )(q, k, k, k, v, v, v, *cast_weights)
    return outs[0], outs[1], list(outs[2:])


def _merge_work(o_refs, l_refs, sg_ref, ga_ref, gs_ref, norm_ref, stage_refs):
    def unview_slab(s):
        def work():
            produced = []
            for p, dil in enumerate(DILATIONS[1:]):
                for r in range(dil):
                    blk = o_refs[p + 1][:, r * D_ATTN + s * LANES:r * D_ATTN + (s + 1) * LANES]
                    blk = blk.astype(F32)
                    stage_refs[2 * p][s, pl.ds(r, TM // dil, stride=dil), :] = blk
                    produced.append(blk)
            return produced
        return work

    def unview_lse():
        produced = []
        for p, dil in enumerate(DILATIONS[1:]):
            for r in range(dil):
                blk = l_refs[p + 1][:, r * LANES:(r + 1) * LANES]
                stage_refs[2 * p + 1][pl.ds(r, TM // dil, stride=dil), :] = blk
                produced.append(blk)
        return produced

    def merge_rows(c):
        def work():
            rows = slice(c * CHUNK, (c + 1) * CHUNK)
            low_head = lax.broadcasted_iota(jnp.int32, (1, LANES), 1) < HEAD_DIM
            lses = [l_refs[0][rows, :]] + [stage_refs[2 * p + 1][rows, :] for p in range(N_PAT - 1)]
            m = functools.reduce(jnp.maximum, lses)
            es = [jnp.exp(l - m) for l in lses]
            inv = 1.0 / sum(es)
            slabs = []
            for s in range(N_SLABS):
                cols = slice(s * LANES, (s + 1) * LANES)
                outs = [o_refs[0][rows, cols].astype(F32)]
                outs += [stage_refs[2 * p][s, rows, :] for p in range(N_PAT - 1)]
                acc = None
                for e, o in zip(es, outs):
                    w = e * inv
                    w = jnp.where(low_head, w[:, s:s + 1], w[:, HEAD_DIM + s:HEAD_DIM + s + 1])
                    acc = w * o if acc is None else acc + w * o
                slabs.append(acc)
            na = _rms(jnp.concatenate(slabs, axis=1), ga_ref[...])
            ns = _rms(sg_ref[rows, :].astype(F32), gs_ref[...])
            norm_ref[rows, 0:D_ATTN] = na.astype(BF16)
            norm_ref[rows, D_ATTN:] = ns.astype(BF16)
            return [na, ns]
        return work

    return ([unview_slab(s) for s in range(N_SLABS)] + [unview_lse]
            + [merge_rows(c) for c in range(TM // CHUNK)])


def _merge_ffn_kernel(x_ref, *refs, n_tiles, final):
    o_refs, l_refs = refs[0:2 * N_PAT:2], refs[1:2 * N_PAT:2]
    sg_ref, ga_ref, gs_ref, wo_ref = refs[2 * N_PAT:2 * N_PAT + 4]
    ffn_refs = refs[2 * N_PAT + 4:2 * N_PAT + 8]
    fgain_ref, out_ref, act_ref, norm_ref = refs[2 * N_PAT + 8:2 * N_PAT + 12]
    stage_refs = refs[2 * N_PAT + 12:]
    step = pl.program_id(0)
    merge = _merge_work(o_refs, l_refs, sg_ref, ga_ref, gs_ref, norm_ref, stage_refs)

    def project_and_ffn(side_work):
        x = x_ref[...] + jnp.dot(norm_ref[...], wo_ref[...], preferred_element_type=F32)
        y = _ffn_apply(x, *ffn_refs, act_ref, side_work=side_work)
        out_ref[...] = _rms(y, fgain_ref[...]) if final else y

    @pl.when(step == 0)
    def _():
        for work in merge:
            work()

    @pl.when((step > 0) & (step < n_tiles))
    def _():
        project_and_ffn(merge)

    @pl.when(step == n_tiles)
    def _():
        project_and_ffn(())


def _merge_ffn(x, attn, sg, ga, gs, wo, ffn_w, fgain, layer, final):
    tokens = x.shape[0]
    n_tiles = tokens // TM
    lead = lambda s: (jnp.minimum(s, n_tiles - 1), 0)
    lag = lambda s: (jnp.maximum(s - 1, 0), 0)
    views = [pl.BlockSpec((TM // d, d * width), lead) for d in DILATIONS for width in (D_ATTN, LANES)]
    return pl.pallas_call(
        functools.partial(_merge_ffn_kernel, n_tiles=n_tiles, final=final),
        grid=(n_tiles + 1,),
        in_specs=[pl.BlockSpec((TM, D_MODEL), lag)] + views + [pl.BlockSpec((TM, D_SGU), lead)]
                 + [_resident((1, D_ATTN), layer), _resident((1, D_SGU), layer),
                    _resident_weight((D_ATTN + D_SGU, D_MODEL))]
                 + _ffn_specs(layer) + [pl.BlockSpec((1, D_MODEL), lambda s: (0, 0))],
        out_specs=pl.BlockSpec((TM, D_MODEL), lag),
        out_shape=jax.ShapeDtypeStruct((tokens, D_MODEL), F32),
        scratch_shapes=[pltpu.VMEM((TM, D_FF), BF16), pltpu.VMEM((TM, D_ATTN + D_SGU), BF16)]
                       + [pltpu.VMEM((N_SLABS, TM, LANES), F32), pltpu.VMEM((TM, LANES), F32)] * (N_PAT - 1),
        compiler_params=_params(1),
        name="merge_ffn",
    )(x, *attn, sg, ga, gs, wo, *ffn_w, fgain)


def kernel(x, norm_ffn1, ffn1_w_gate, ffn1_w_up, ffn1_w_down, norm_mix, w_in, sgu_ln_g, sgu_ln_b,
           sgu_w, sgu_b, out_norm_attn, out_norm_sgu, w_out, norm_ffn2, ffn2_w_gate, ffn2_w_up,
           ffn2_w_down, final_norm):
    batch, seq, _ = x.shape
    tokens = batch * seq
    assert x.shape[2] == D_MODEL and tokens % TM == 0 and seq % TM == 0 and TM % CHUNK == 0

    row = lambda p: p[:, None, :]
    n1, n2, nm = row(norm_ffn1), row(norm_ffn2), row(norm_mix)
    lng, lnb = row(sgu_ln_g), row(sgu_ln_b)
    ga, gs = row(out_norm_attn), row(out_norm_sgu)
    sgu_w_b = sgu_w.astype(BF16)
    sgu_b_col = sgu_b[..., None]
    fgain = final_norm[None, :]
    rope = _rope_tables(seq)

    cast_groups = ((ffn1_w_gate, ffn1_w_up, ffn1_w_down), (ffn2_w_gate, ffn2_w_up, ffn2_w_down),
                   (w_in, w_out))
    weights = [[w[0].astype(BF16) for w in group] for group in cast_groups]

    xt = x.reshape(tokens, D_MODEL)
    for layer in range(DEPTH):
        ffn1_w, ffn2_w, (w_in_b, w_out_b) = weights
        xt, qs, ks, vs, sg = _ffn_inproj(xt, (n1, *ffn1_w), nm, w_in_b, rope, lng, lnb, sgu_w_b,
                                         sgu_b_col, layer, seq)
        attn, weights = [], []
        for q, k, v, dil, group in zip(qs, ks, vs, DILATIONS, cast_groups):
            o, lse, cast = _attention(q, k, v, dil, seq, group if layer + 1 < DEPTH else (), layer + 1)
            attn += [o, lse]
            weights.append(cast)
        xt = _merge_ffn(xt, attn, sg, ga, gs, w_out_b, (n2, *ffn2_w), fgain, layer,
                        layer == DEPTH - 1)
    return xt.reshape(batch, seq, D_MODEL)
```

```python
import functools
import math

import numpy as np
import jax
import jax.numpy as jnp
from jax import lax
from jax.experimental import pallas as pl
from jax.experimental.pallas import tpu as pltpu

F32 = jnp.float32
BF16 = jnp.bfloat16

D_MODEL = 1024
DEPTH = 4
HEAD_DIM = 64
D_ATTN = 512
D_SGU = 512
N_SGU_GROUPS = 4
CHUNK = 128
D_IN = 3 * D_ATTN + 2 * D_SGU
D_FF = 2816
DILATIONS = (1, 4, 16)
N_PAT = len(DILATIONS)
BAND = 64
ROPE_THETA = 500000.0
ROPE_DIM = HEAD_DIM // 4
EPS = 1e-6
NEG_INF = -1e30
LOG2_E = math.log2(math.e)
LN_2 = math.log(2.0)

LANES = 128
SUBLANES = 8
V7X_VMEM_BYTES = 64 * 1024 * 1024
VMEM_LIMIT = V7X_VMEM_BYTES * 7 // 8

TM = 512
FF_CHUNK = 256
N_FF_CHUNKS = D_FF // FF_CHUNK
N_SLABS = D_ATTN // LANES
MAX_ATT_ROWS = 2048
QB = 2 * BAND
KW = QB + 2 * BAND


def _rms(x, gain):
    return x * lax.rsqrt(jnp.mean(x * x, axis=-1, keepdims=True) + EPS) * gain


def _params(n_axes):
    return pltpu.CompilerParams(dimension_semantics=("arbitrary",) * n_axes,
                                vmem_limit_bytes=VMEM_LIMIT)


def _resident(shape, layer):
    nd = len(shape)
    return pl.BlockSpec((None,) + shape, lambda *_: (layer,) + (0,) * nd,
                        pipeline_mode=pl.Buffered(1))


def _resident_weight(shape):
    return pl.BlockSpec(shape, lambda *_: (0,) * len(shape), pipeline_mode=pl.Buffered(1))


def _view_shape(tokens, dil, dtype):
    return jax.ShapeDtypeStruct((tokens // dil, dil * D_ATTN), dtype)


def _ordering_zero(values):
    acc = None
    for v in values:
        bits = pltpu.bitcast(v[0:SUBLANES, 0:LANES], jnp.uint32)
        acc = bits if acc is None else acc | bits
    return ((acc >> 16) >> 16)[0:1, :].astype(F32)


def _ffn_apply(x, gain_ref, wg_ref, wu_ref, wd_ref, act_ref, side_work=()):
    h = _rms(x, gain_ref[...]).astype(BF16)
    for c in range(N_FF_CHUNKS):
        cols = slice(c * FF_CHUNK, (c + 1) * FF_CHUNK)
        g = jnp.dot(h, wg_ref[:, cols], preferred_element_type=F32)
        u = jnp.dot(h, wu_ref[:, cols], preferred_element_type=F32)
        a = g * jax.nn.sigmoid(g) * u
        todo = side_work[c * len(side_work) // N_FF_CHUNKS:(c + 1) * len(side_work) // N_FF_CHUNKS]
        if todo:
            zero = _ordering_zero([v for work in todo for v in work()])
            a = a + jnp.concatenate([zero] * (FF_CHUNK // LANES), axis=1)
        act_ref[:, cols] = a.astype(BF16)
    return x + 0.5 * jnp.dot(act_ref[...], wd_ref[...], preferred_element_type=F32)


def _ffn_specs(layer):
    return [_resident((1, D_MODEL), layer), _resident_weight((D_MODEL, D_FF)),
            _resident_weight((D_MODEL, D_FF)), _resident_weight((D_FF, D_MODEL))]


def _gelu(x):
    return 0.5 * x * (1.0 + lax.erf(x * (1.0 / math.sqrt(2.0))))


def _inproj_matmul(x, gain_ref, w_ref, raw_ref):
    h = _rms(x, gain_ref[...]).astype(BF16)
    for col0 in range(0, D_IN, D_ATTN):
        raw_ref[:, col0:col0 + D_ATTN] = jnp.dot(h, w_ref[:, col0:col0 + D_ATTN],
                                                 preferred_element_type=F32)


def _inproj_finish_work(raw_ref, cos_ref, sina_ref, sinb_ref, lng_ref, lnb_ref, sw_ref, sb_ref,
                        q_refs, k_refs, v_refs, sg_ref, stage_ref):
    def qkv_slab(col0, dst_refs, rotary, scale, s):
        def work():
            t = raw_ref[:, col0 + s * LANES:col0 + (s + 1) * LANES]
            if rotary:
                half = ROPE_DIM // 2
                t = (t * cos_ref[...] + pltpu.roll(t, half, 1) * sina_ref[...]
                     + pltpu.roll(t, LANES - half, 1) * sinb_ref[...])
            if scale != 1.0:
                t = t * scale
            dst_refs[0][:, s * LANES:(s + 1) * LANES] = t.astype(BF16)
            stage_ref[0, s] = t
            produced = [t]
            for k in range(1, N_PAT):
                d_prev, dil = DILATIONS[k - 1], DILATIONS[k]
                ratio, n_prev, n = dil // d_prev, TM // d_prev, TM // dil
                for e in range(d_prev):
                    for c in range(ratio):
                        r = c * d_prev + e
                        blk = stage_ref[k - 1, s, pl.ds(e * n_prev + c, n, stride=ratio), :]
                        dst_refs[k][:, r * D_ATTN + s * LANES:r * D_ATTN + (s + 1) * LANES] = (
                            blk.astype(BF16))
                        if k + 1 < N_PAT:
                            stage_ref[k, s, r * n:(r + 1) * n, :] = blk
                        produced.append(blk)
            return produced
        return work

    def sgu_chunk(c):
        def work():
            rows = slice(c * CHUNK, (c + 1) * CHUNK)
            u = _gelu(raw_ref[rows, 3 * D_ATTN:3 * D_ATTN + D_SGU])
            g = _gelu(raw_ref[rows, 3 * D_ATTN + D_SGU:])
            gc = g - jnp.mean(g, axis=-1, keepdims=True)
            gn = (gc * lax.rsqrt(jnp.mean(gc * gc, axis=-1, keepdims=True) + EPS) * lng_ref[...]
                  + lnb_ref[...]).astype(BF16)
            group = D_SGU // N_SGU_GROUPS
            produced = []
            for gi in range(N_SGU_GROUPS):
                cols = slice(gi * group, (gi + 1) * group)
                mixed = jnp.dot(sw_ref[gi], gn[:, cols], preferred_element_type=F32) + sb_ref[gi]
                gated = u[:, cols] * mixed
                sg_ref[rows, cols] = gated.astype(BF16)
                produced.append(gated)
            return produced
        return work

    q_scale = LOG2_E / math.sqrt(HEAD_DIM)
    pieces = [qkv_slab(0, q_refs, True, q_scale, s) for s in range(N_SLABS)]
    pieces += [qkv_slab(D_ATTN, k_refs, True, 1.0, s) for s in range(N_SLABS)]
    pieces += [qkv_slab(2 * D_ATTN, v_refs, False, 1.0, s) for s in range(N_SLABS)]
    pieces += [sgu_chunk(c) for c in range(TM // CHUNK)]
    return pieces


def _ffn_inproj_kernel(x_ref, *refs, n_tiles):
    ffn_refs, gain_ref, w_ref, finish_refs = refs[:4], refs[4], refs[5], refs[6:13]
    xo_ref = refs[13]
    q_refs, k_refs, v_refs = (refs[14 + i * N_PAT:14 + (i + 1) * N_PAT] for i in range(3))
    sg_ref, act_ref, stage_ref, raw_ref = refs[14 + 3 * N_PAT:]
    step = pl.program_id(0)
    finish = _inproj_finish_work(raw_ref, *finish_refs, q_refs, k_refs, v_refs, sg_ref, stage_ref)

    @pl.when(step == 0)
    def _():
        raw_ref[...] = jnp.zeros_like(raw_ref)

    @pl.when(step < n_tiles)
    def _():
        x = _ffn_apply(x_ref[...], *ffn_refs, act_ref, side_work=finish)
        xo_ref[...] = x
        _inproj_matmul(x, gain_ref, w_ref, raw_ref)

    @pl.when(step == n_tiles)
    def _():
        for work in finish:
            work()


def _ffn_inproj(x, ffn_w, gain, w_in, rope, lng, lnb, sw, sb, layer, seq):
    tokens = x.shape[0]
    n_tiles = tokens // TM
    lead = lambda s: (jnp.minimum(s, n_tiles - 1), 0)
    lag = lambda s: (jnp.maximum(s - 1, 0), 0)
    table = pl.BlockSpec((TM, LANES), lambda s: (jnp.maximum(s - 1, 0) % (seq // TM), 0))
    views = [pl.BlockSpec((TM // d, d * D_ATTN), lag) for d in DILATIONS]
    view_shapes = [_view_shape(tokens, d, BF16) for d in DILATIONS]
    outs = pl.pallas_call(
        functools.partial(_ffn_inproj_kernel, n_tiles=n_tiles),
        grid=(n_tiles + 1,),
        in_specs=[pl.BlockSpec((TM, D_MODEL), lead)] + _ffn_specs(layer)
                 + [_resident((1, D_MODEL), layer), _resident_weight((D_MODEL, D_IN)),
                    table, table, table,
                    _resident((1, D_SGU), layer), _resident((1, D_SGU), layer),
                    _resident((N_SGU_GROUPS, CHUNK, CHUNK), layer),
                    _resident((N_SGU_GROUPS, CHUNK, 1), layer)],
        out_specs=[pl.BlockSpec((TM, D_MODEL), lead)] + views * 3 + [pl.BlockSpec((TM, D_SGU), lag)],
        out_shape=[jax.ShapeDtypeStruct((tokens, D_MODEL), F32)] + view_shapes * 3
                  + [jax.ShapeDtypeStruct((tokens, D_SGU), BF16)],
        scratch_shapes=[pltpu.VMEM((TM, D_FF), BF16), pltpu.VMEM((N_PAT - 1, N_SLABS, TM, LANES), F32),
                        pltpu.VMEM((TM, D_IN), F32)],
        compiler_params=_params(1),
        name="ffn_inproj",
    )(x, *ffn_w, gain, w_in, *rope, lng, lnb, sw, sb)
    return (outs[0], outs[1:1 + N_PAT], outs[1 + N_PAT:1 + 2 * N_PAT],
            outs[1 + 2 * N_PAT:1 + 3 * N_PAT], outs[-1])


def _rope_tables(seq):
    half = ROPE_DIM // 2
    inv_freq = ROPE_THETA ** (-np.arange(0, ROPE_DIM, 2, dtype=np.float64) / ROPE_DIM)
    ang = np.arange(seq, dtype=np.float64)[:, None] * inv_freq[None, :]
    cos = np.ones((seq, LANES))
    sina = np.zeros((seq, LANES))
    sinb = np.zeros((seq, LANES))
    for head0 in range(0, LANES, HEAD_DIM):
        cos[:, head0:head0 + half] = np.cos(ang)
        cos[:, head0 + half:head0 + 2 * half] = np.cos(ang)
        sinb[:, head0:head0 + half] = -np.sin(ang)
        sina[:, head0 + half:head0 + 2 * half] = np.sin(ang)
    return tuple(jnp.asarray(t, F32) for t in (cos, sina, sinb))


def _attn_kernel(q_ref, kc_ref, kp_ref, kn_ref, vc_ref, vp_ref, vn_ref, *refs,
                 sub_len, rows, n_res, n_cast):
    cast_src, (o_ref, lse_ref) = refs[:n_cast], refs[n_cast:n_cast + 2]
    cast_dst, (kbuf, vbuf) = refs[n_cast + 2:2 * n_cast + 2], refs[2 * n_cast + 2:]
    for src, dst in zip(cast_src, cast_dst):
        dst[...] = src[...].astype(BF16)

    ones = jnp.ones((rows + 2 * BAND, LANES), BF16)
    for res in range(n_res):
        c0 = res * D_ATTN
        kbuf[res, 0:BAND] = kp_ref[:, c0:c0 + D_ATTN]
        kbuf[res, BAND:BAND + rows] = kc_ref[:, c0:c0 + D_ATTN]
        kbuf[res, BAND + rows:] = kn_ref[:, c0:c0 + D_ATTN]
        for pair in range(N_SLABS):
            src = slice(c0 + pair * LANES, c0 + (pair + 1) * LANES)
            dst = slice(2 * pair * LANES, (2 * pair + 1) * LANES)
            vbuf[res, 0:BAND, dst] = vp_ref[:, src]
            vbuf[res, BAND:BAND + rows, dst] = vc_ref[:, src]
            vbuf[res, BAND + rows:, dst] = vn_ref[:, src]
            vbuf[res, :, (2 * pair + 1) * LANES:(2 * pair + 2) * LANES] = ones

    first_row = (pl.program_id(0) % (sub_len // rows)) * rows
    delta = (lax.broadcasted_iota(jnp.int32, (QB, KW), 1)
             - lax.broadcasted_iota(jnp.int32, (QB, KW), 0))
    band_bias = jnp.where((delta >= 0) & (delta <= 2 * BAND), 0.0, NEG_INF)
    key_col = lax.broadcasted_iota(jnp.int32, (1, KW), 1)
    lane = lax.broadcasted_iota(jnp.int32, (1, LANES), 1)
    low_head = lane < HEAD_DIM

    for j in range(rows // QB):
        row0 = j * QB
        key_pos = key_col + (first_row + row0 - BAND)
        bias = band_bias + jnp.where((key_pos >= 0) & (key_pos < sub_len), 0.0, NEG_INF)
        bias = jnp.concatenate([bias, bias], axis=0)
        for res in range(n_res):
            lse_tile = jnp.zeros((QB, LANES), F32)
            for pair in range(N_SLABS):
                cols = slice(res * D_ATTN + pair * LANES, res * D_ATTN + (pair + 1) * LANES)
                q2 = q_ref[row0:row0 + QB, cols]
                zero = jnp.zeros_like(q2)
                qs = jnp.concatenate([jnp.where(low_head, q2, zero), jnp.where(low_head, zero, q2)],
                                     axis=0)
                k2 = kbuf[res, row0:row0 + KW, pair * LANES:(pair + 1) * LANES]
                s = lax.dot_general(qs, k2, (((1,), (1,)), ((), ())),
                                    preferred_element_type=F32) + bias
                m = jnp.max(s, axis=-1, keepdims=True)
                p = jnp.exp2(s - m).astype(BF16)
                pv = jnp.dot(p, vbuf[res, row0:row0 + KW, 2 * pair * LANES:(2 * pair + 2) * LANES],
                             preferred_element_type=F32)
                num = jnp.where(low_head, pv[:QB, :LANES], pv[QB:, :LANES])
                den = jnp.where(low_head, pv[:QB, LANES:], pv[QB:, LANES:])
                o_ref[row0:row0 + QB, cols] = (num / den).astype(BF16)
                lse_pair = jnp.where(low_head, m[:QB], m[QB:]) * LN_2 + jnp.log(den)
                lse_tile = jnp.where((lane == pair) | (lane == HEAD_DIM + pair), lse_pair, lse_tile)
            lse_ref[row0:row0 + QB, res * LANES:(res + 1) * LANES] = lse_tile


def _attention(q, k, v, dil, seq, cast_jobs=()):
    total_rows = q.shape[0]
    sub_len = seq // dil
    rows = min(MAX_ATT_ROWS, sub_len)
    n_res = min(MAX_ATT_ROWS // rows, dil)
    assert sub_len % rows == 0 and rows % QB == 0 and dil % n_res == 0
    halos_per_step = rows // BAND
    last_halo = total_rows // BAND - 1
    width = n_res * D_ATTN
    grid = (total_rows // rows, dil // n_res)
    cur = pl.BlockSpec((rows, width), lambda i, r: (i, r))
    prev = pl.BlockSpec((BAND, width), lambda i, r: (jnp.maximum(i * halos_per_step - 1, 0), r))
    nxt = pl.BlockSpec((BAND, width),
                       lambda i, r: (jnp.minimum((i + 1) * halos_per_step, last_halo), r))
    n_steps = grid[0] * grid[1]
    cast_in, cast_out, cast_shapes = [], [], []
    for w, w_layer in cast_jobs:
        _, w_rows, w_cols = w.shape
        chunk = w_rows // n_steps
        assert chunk * n_steps == w_rows and chunk % (2 * SUBLANES) == 0
        cast_in.append(pl.BlockSpec((None, chunk, w_cols),
                                    lambda i, r, w_layer=w_layer: (w_layer, i * grid[1] + r, 0)))
        cast_out.append(pl.BlockSpec((chunk, w_cols), lambda i, r: (i * grid[1] + r, 0)))
        cast_shapes.append(jax.ShapeDtypeStruct((w_rows, w_cols), BF16))
    outs = pl.pallas_call(
        functools.partial(_attn_kernel, sub_len=sub_len, rows=rows, n_res=n_res,
                          n_cast=len(cast_jobs)),
        grid=grid,
        in_specs=[cur, cur, prev, nxt, cur, prev, nxt] + cast_in,
        out_specs=[cur, pl.BlockSpec((rows, n_res * LANES), lambda i, r: (i, r))] + cast_out,
        out_shape=[jax.ShapeDtypeStruct(q.shape, BF16),
                   jax.ShapeDtypeStruct((total_rows, dil * LANES), F32)] + cast_shapes,
        scratch_shapes=[pltpu.VMEM((n_res, rows + 2 * BAND, D_ATTN), BF16),
                        pltpu.VMEM((n_res, rows + 2 * BAND, 2 * D_ATTN), BF16)],
        compiler_params=_params(2),
        name=f"attn_d{dil}",
    )(q, k, k, k, v, v, v, *[w for w, _ in cast_jobs])
    return outs[0], outs[1], list(outs[2:])


def _merge_work(o_refs, l_refs, sg_ref, ga_ref, gs_ref, norm_ref, stage_refs):
    def unview_slab(s):
        def work():
            produced = []
            for p, dil in enumerate(DILATIONS[1:]):
                for r in range(dil):
                    blk = o_refs[p + 1][:, r * D_ATTN + s * LANES:r * D_ATTN + (s + 1) * LANES]
                    blk = blk.astype(F32)
                    stage_refs[2 * p][s, pl.ds(r, TM // dil, stride=dil), :] = blk
                    produced.append(blk)
            return produced
        return work

    def unview_lse():
        produced = []
        for p, dil in enumerate(DILATIONS[1:]):
            for r in range(dil):
                blk = l_refs[p + 1][:, r * LANES:(r + 1) * LANES]
                stage_refs[2 * p + 1][pl.ds(r, TM // dil, stride=dil), :] = blk
                produced.append(blk)
        return produced

    def merge_rows(c):
        def work():
            rows = slice(c * CHUNK, (c + 1) * CHUNK)
            low_head = lax.broadcasted_iota(jnp.int32, (1, LANES), 1) < HEAD_DIM
            lses = [l_refs[0][rows, :]] + [stage_refs[2 * p + 1][rows, :] for p in range(N_PAT - 1)]
            m = functools.reduce(jnp.maximum, lses)
            es = [jnp.exp(l - m) for l in lses]
            inv = 1.0 / sum(es)
            slabs = []
            for s in range(N_SLABS):
                cols = slice(s * LANES, (s + 1) * LANES)
                outs = [o_refs[0][rows, cols].astype(F32)]
                outs += [stage_refs[2 * p][s, rows, :] for p in range(N_PAT - 1)]
                acc = None
                for e, o in zip(es, outs):
                    w = e * inv
                    w = jnp.where(low_head, w[:, s:s + 1], w[:, HEAD_DIM + s:HEAD_DIM + s + 1])
                    acc = w * o if acc is None else acc + w * o
                slabs.append(acc)
            na = _rms(jnp.concatenate(slabs, axis=1), ga_ref[...])
            ns = _rms(sg_ref[rows, :].astype(F32), gs_ref[...])
            norm_ref[rows, 0:D_ATTN] = na.astype(BF16)
            norm_ref[rows, D_ATTN:] = ns.astype(BF16)
            return [na, ns]
        return work

    return ([unview_slab(s) for s in range(N_SLABS)] + [unview_lse]
            + [merge_rows(c) for c in range(TM // CHUNK)])


def _merge_ffn_kernel(x_ref, *refs, n_tiles, final):
    o_refs, l_refs = refs[0:2 * N_PAT:2], refs[1:2 * N_PAT:2]
    sg_ref, ga_ref, gs_ref, wo_ref = refs[2 * N_PAT:2 * N_PAT + 4]
    ffn_refs = refs[2 * N_PAT + 4:2 * N_PAT + 8]
    fgain_ref, out_ref, act_ref, norm_ref = refs[2 * N_PAT + 8:2 * N_PAT + 12]
    stage_refs = refs[2 * N_PAT + 12:]
    step = pl.program_id(0)
    merge = _merge_work(o_refs, l_refs, sg_ref, ga_ref, gs_ref, norm_ref, stage_refs)

    def project_and_ffn(side_work):
        x = x_ref[...] + jnp.dot(norm_ref[...], wo_ref[...], preferred_element_type=F32)
        y = _ffn_apply(x, *ffn_refs, act_ref, side_work=side_work)
        out_ref[...] = _rms(y, fgain_ref[...]) if final else y

    @pl.when(step == 0)
    def _():
        for work in merge:
            work()

    @pl.when((step > 0) & (step < n_tiles))
    def _():
        project_and_ffn(merge)

    @pl.when(step == n_tiles)
    def _():
        project_and_ffn(())


def _merge_ffn(x, attn, sg, ga, gs, wo, ffn_w, fgain, layer, final):
    tokens = x.shape[0]
    n_tiles = tokens // TM
    lead = lambda s: (jnp.minimum(s, n_tiles - 1), 0)
    lag = lambda s: (jnp.maximum(s - 1, 0), 0)
    views = [pl.BlockSpec((TM // d, d * width), lead) for d in DILATIONS for width in (D_ATTN, LANES)]
    return pl.pallas_call(
        functools.partial(_merge_ffn_kernel, n_tiles=n_tiles, final=final),
        grid=(n_tiles + 1,),
        in_specs=[pl.BlockSpec((TM, D_MODEL), lag)] + views + [pl.BlockSpec((TM, D_SGU), lead)]
                 + [_resident((1, D_ATTN), layer), _resident((1, D_SGU), layer),
                    _resident_weight((D_ATTN + D_SGU, D_MODEL))]
                 + _ffn_specs(layer) + [pl.BlockSpec((1, D_MODEL), lambda s: (0, 0))],
        out_specs=pl.BlockSpec((TM, D_MODEL), lag),
        out_shape=jax.ShapeDtypeStruct((tokens, D_MODEL), F32),
        scratch_shapes=[pltpu.VMEM((TM, D_FF), BF16), pltpu.VMEM((TM, D_ATTN + D_SGU), BF16)]
                       + [pltpu.VMEM((N_SLABS, TM, LANES), F32), pltpu.VMEM((TM, LANES), F32)] * (N_PAT - 1),
        compiler_params=_params(1),
        name="merge_ffn",
    )(x, *attn, sg, ga, gs, wo, *ffn_w, fgain)


def kernel(x, norm_ffn1, ffn1_w_gate, ffn1_w_up, ffn1_w_down, norm_mix, w_in, sgu_ln_g, sgu_ln_b,
           sgu_w, sgu_b, out_norm_attn, out_norm_sgu, w_out, norm_ffn2, ffn2_w_gate, ffn2_w_up,
           ffn2_w_down, final_norm):
    batch, seq, _ = x.shape
    tokens = batch * seq
    assert x.shape[2] == D_MODEL and tokens % TM == 0 and seq % TM == 0 and TM % CHUNK == 0

    row = lambda p: p[:, None, :]
    n1, n2, nm = row(norm_ffn1), row(norm_ffn2), row(norm_mix)
    lng, lnb = row(sgu_ln_g), row(sgu_ln_b)
    ga, gs = row(out_norm_attn), row(out_norm_sgu)
    sgu_w_b = sgu_w.astype(BF16)
    sgu_b_col = sgu_b[..., None]
    fgain = final_norm[None, :]
    rope = _rope_tables(seq)

    stacked = dict(gate1=ffn1_w_gate, up1=ffn1_w_up, down1=ffn1_w_down, w_in=w_in,
                   gate2=ffn2_w_gate, up2=ffn2_w_up, down2=ffn2_w_down, w_out=w_out)
    early = ("gate1", "up1", "down1", "w_in")
    next_layer_jobs = (("gate1", "up1"), ("gate2", "up2"), ("down1", "down2", "w_in", "w_out"))
    same_layer_jobs = (("gate2",), ("up2",), ("down2", "w_out"))
    bf16 = {(name, 0): stacked[name][0].astype(BF16) for name in early}

    xt = x.reshape(tokens, D_MODEL)
    for layer in range(DEPTH):
        ffn1_w = [bf16[(name, layer)] for name in ("gate1", "up1", "down1")]
        xt, qs, ks, vs, sg = _ffn_inproj(xt, (n1, *ffn1_w), nm, bf16[("w_in", layer)], rope, lng, lnb,
                                         sgu_w_b, sgu_b_col, layer, seq)
        attn = []
        for p, (q, k, v, dil) in enumerate(zip(qs, ks, vs, DILATIONS)):
            jobs = [(name, layer + 1) for name in next_layer_jobs[p]] if layer + 1 < DEPTH else []
            jobs += [(name, layer) for name in same_layer_jobs[p] if (name, layer) not in bf16]
            o, lse, cast = _attention(q, k, v, dil, seq, [(stacked[n], l) for n, l in jobs])
            attn += [o, lse]
            bf16.update(zip(jobs, cast))
        ffn2_w = [bf16[(name, layer)] for name in ("gate2", "up2", "down2")]
        xt = _merge_ffn(xt, attn, sg, ga, gs, bf16[("w_out", layer)], (n2, *ffn2_w), fgain, layer,
                        layer == DEPTH - 1)
    return xt.reshape(batch, seq, D_MODEL)
```

```python
import functools
import math

import numpy as np
import jax
import jax.numpy as jnp
from jax import lax
from jax.experimental import pallas as pl
from jax.experimental.pallas import tpu as pltpu

F32 = jnp.float32
BF16 = jnp.bfloat16

D_MODEL = 1024
DEPTH = 4
HEAD_DIM = 64
D_ATTN = 512
D_SGU = 512
N_SGU_GROUPS = 4
CHUNK = 128
D_IN = 3 * D_ATTN + 2 * D_SGU
D_FF = 2816
DILATIONS = (1, 4, 16)
N_PAT = len(DILATIONS)
BAND = 64
ROPE_THETA = 500000.0
ROPE_DIM = HEAD_DIM // 4
EPS = 1e-6
NEG_INF = -1e30
LOG2_E = math.log2(math.e)
LN_2 = math.log(2.0)

LANES = 128
SUBLANES = 8
V7X_VMEM_BYTES = 64 * 1024 * 1024
VMEM_LIMIT = V7X_VMEM_BYTES * 7 // 8

TM = 512
FF_CHUNK = 256
N_FF_CHUNKS = D_FF // FF_CHUNK
N_SLABS = D_ATTN // LANES
MAX_ATT_ROWS = 2048
QB = 2 * BAND
KW = QB + 2 * BAND


def _rms(x, gain):
    return x * lax.rsqrt(jnp.mean(x * x, axis=-1, keepdims=True) + EPS) * gain


def _params(n_axes):
    return pltpu.CompilerParams(dimension_semantics=("arbitrary",) * n_axes,
                                vmem_limit_bytes=VMEM_LIMIT)


def _resident(shape, layer):
    nd = len(shape)
    return pl.BlockSpec((None,) + shape, lambda *_: (layer,) + (0,) * nd,
                        pipeline_mode=pl.Buffered(1))


def _resident_weight(shape):
    return pl.BlockSpec(shape, lambda *_: (0,) * len(shape), pipeline_mode=pl.Buffered(1))


def _view_shape(tokens, dil, dtype):
    return jax.ShapeDtypeStruct((tokens // dil, dil * D_ATTN), dtype)


def _ordering_zero(values):
    acc = None
    for v in values:
        bits = pltpu.bitcast(v[0:SUBLANES, 0:LANES], jnp.uint32)
        acc = bits if acc is None else acc | bits
    return ((acc >> 16) >> 16)[0:1, :].astype(F32)


def _ffn_apply(x, gain_ref, wg_ref, wu_ref, wd_ref, act_ref, side_work=()):
    h = _rms(x, gain_ref[...]).astype(BF16)
    for c in range(N_FF_CHUNKS):
        cols = slice(c * FF_CHUNK, (c + 1) * FF_CHUNK)
        g = jnp.dot(h, wg_ref[:, cols], preferred_element_type=F32)
        u = jnp.dot(h, wu_ref[:, cols], preferred_element_type=F32)
        a = g * jax.nn.sigmoid(g) * u
        todo = side_work[c * len(side_work) // N_FF_CHUNKS:(c + 1) * len(side_work) // N_FF_CHUNKS]
        if todo:
            zero = _ordering_zero([v for work in todo for v in work()])
            a = a + jnp.concatenate([zero] * (FF_CHUNK // LANES), axis=1)
        act_ref[:, cols] = a.astype(BF16)
    return x + 0.5 * jnp.dot(act_ref[...], wd_ref[...], preferred_element_type=F32)


def _ffn_specs(layer):
    return [_resident((1, D_MODEL), layer), _resident_weight((D_MODEL, D_FF)),
            _resident_weight((D_MODEL, D_FF)), _resident_weight((D_FF, D_MODEL))]


def _gelu(x):
    return 0.5 * x * (1.0 + lax.erf(x * (1.0 / math.sqrt(2.0))))


def _inproj_matmul(x, gain_ref, w_ref, raw_ref):
    h = _rms(x, gain_ref[...]).astype(BF16)
    for col0 in range(0, D_IN, D_ATTN):
        raw_ref[:, col0:col0 + D_ATTN] = jnp.dot(h, w_ref[:, col0:col0 + D_ATTN],
                                                 preferred_element_type=F32)


def _inproj_finish_work(raw_ref, cos_ref, sina_ref, sinb_ref, lng_ref, lnb_ref, sw_ref, sb_ref,
                        q_refs, k_refs, v_refs, sg_ref, stage_ref):
    def qkv_slab(col0, dst_refs, rotary, scale, s):
        def work():
            t = raw_ref[:, col0 + s * LANES:col0 + (s + 1) * LANES]
            if rotary:
                half = ROPE_DIM // 2
                t = (t * cos_ref[...] + pltpu.roll(t, half, 1) * sina_ref[...]
                     + pltpu.roll(t, LANES - half, 1) * sinb_ref[...])
            if scale != 1.0:
                t = t * scale
            dst_refs[0][:, s * LANES:(s + 1) * LANES] = t.astype(BF16)
            stage_ref[0, s] = t
            produced = [t]
            for k in range(1, N_PAT):
                d_prev, dil = DILATIONS[k - 1], DILATIONS[k]
                ratio, n_prev, n = dil // d_prev, TM // d_prev, TM // dil
                for e in range(d_prev):
                    for c in range(ratio):
                        r = c * d_prev + e
                        blk = stage_ref[k - 1, s, pl.ds(e * n_prev + c, n, stride=ratio), :]
                        dst_refs[k][:, r * D_ATTN + s * LANES:r * D_ATTN + (s + 1) * LANES] = (
                            blk.astype(BF16))
                        if k + 1 < N_PAT:
                            stage_ref[k, s, r * n:(r + 1) * n, :] = blk
                        produced.append(blk)
            return produced
        return work

    def sgu_chunk(c):
        def work():
            rows = slice(c * CHUNK, (c + 1) * CHUNK)
            u = _gelu(raw_ref[rows, 3 * D_ATTN:3 * D_ATTN + D_SGU])
            g = _gelu(raw_ref[rows, 3 * D_ATTN + D_SGU:])
            gc = g - jnp.mean(g, axis=-1, keepdims=True)
            gn = (gc * lax.rsqrt(jnp.mean(gc * gc, axis=-1, keepdims=True) + EPS) * lng_ref[...]
                  + lnb_ref[...]).astype(BF16)
            group = D_SGU // N_SGU_GROUPS
            produced = []
            for gi in range(N_SGU_GROUPS):
                cols = slice(gi * group, (gi + 1) * group)
                mixed = jnp.dot(sw_ref[gi], gn[:, cols], preferred_element_type=F32) + sb_ref[gi]
                gated = u[:, cols] * mixed
                sg_ref[rows, cols] = gated.astype(BF16)
                produced.append(gated)
            return produced
        return work

    q_scale = LOG2_E / math.sqrt(HEAD_DIM)
    pieces = [qkv_slab(0, q_refs, True, q_scale, s) for s in range(N_SLABS)]
    pieces += [qkv_slab(D_ATTN, k_refs, True, 1.0, s) for s in range(N_SLABS)]
    pieces += [qkv_slab(2 * D_ATTN, v_refs, False, 1.0, s) for s in range(N_SLABS)]
    pieces += [sgu_chunk(c) for c in range(TM // CHUNK)]
    return pieces


def _ffn_inproj_kernel(x_ref, *refs):
    ffn_refs, gain_ref, w_ref, finish_refs = refs[:4], refs[4], refs[5], refs[6:13]
    xo_ref = refs[13]
    q_refs, k_refs, v_refs = (refs[14 + i * N_PAT:14 + (i + 1) * N_PAT] for i in range(3))
    sg_ref, act_ref, stage_ref = refs[14 + 3 * N_PAT:]
    x = _ffn_apply(x_ref[...], *ffn_refs, act_ref)
    xo_ref[...] = x
    h = _rms(x, gain_ref[...]).astype(BF16)
    raw = jnp.concatenate([jnp.dot(h, w_ref[:, c0:c0 + D_ATTN], preferred_element_type=F32)
                           for c0 in range(0, D_IN, D_ATTN)], axis=1)
    for work in _inproj_finish_work(raw, *finish_refs, q_refs, k_refs, v_refs, sg_ref, stage_ref):
        work()


def _ffn_inproj(x, ffn_w, gain, w_in, rope, lng, lnb, sw, sb, layer, seq):
    tokens = x.shape[0]
    rows = lambda s: (s, 0)
    table = pl.BlockSpec((TM, LANES), lambda s: (s % (seq // TM), 0))
    views = [pl.BlockSpec((TM // d, d * D_ATTN), rows) for d in DILATIONS]
    view_shapes = [_view_shape(tokens, d, BF16) for d in DILATIONS]
    outs = pl.pallas_call(
        _ffn_inproj_kernel,
        grid=(tokens // TM,),
        in_specs=[pl.BlockSpec((TM, D_MODEL), rows)] + _ffn_specs(layer)
                 + [_resident((1, D_MODEL), layer), _resident_weight((D_MODEL, D_IN)),
                    table, table, table,
                    _resident((1, D_SGU), layer), _resident((1, D_SGU), layer),
                    _resident((N_SGU_GROUPS, CHUNK, CHUNK), layer),
                    _resident((N_SGU_GROUPS, CHUNK, 1), layer)],
        out_specs=[pl.BlockSpec((TM, D_MODEL), rows)] + views * 3 + [pl.BlockSpec((TM, D_SGU), rows)],
        out_shape=[jax.ShapeDtypeStruct((tokens, D_MODEL), F32)] + view_shapes * 3
                  + [jax.ShapeDtypeStruct((tokens, D_SGU), BF16)],
        scratch_shapes=[pltpu.VMEM((TM, D_FF), BF16), pltpu.VMEM((N_PAT - 1, N_SLABS, TM, LANES), F32)],
        compiler_params=_params(1),
        name="ffn_inproj",
    )(x, *ffn_w, gain, w_in, *rope, lng, lnb, sw, sb)
    return (outs[0], outs[1:1 + N_PAT], outs[1 + N_PAT:1 + 2 * N_PAT],
            outs[1 + 2 * N_PAT:1 + 3 * N_PAT], outs[-1])


def _rope_tables(seq):
    half = ROPE_DIM // 2
    inv_freq = ROPE_THETA ** (-np.arange(0, ROPE_DIM, 2, dtype=np.float64) / ROPE_DIM)
    ang = np.arange(seq, dtype=np.float64)[:, None] * inv_freq[None, :]
    cos = np.ones((seq, LANES))
    sina = np.zeros((seq, LANES))
    sinb = np.zeros((seq, LANES))
    for head0 in range(0, LANES, HEAD_DIM):
        cos[:, head0:head0 + half] = np.cos(ang)
        cos[:, head0 + half:head0 + 2 * half] = np.cos(ang)
        sinb[:, head0:head0 + half] = -np.sin(ang)
        sina[:, head0 + half:head0 + 2 * half] = np.sin(ang)
    return tuple(jnp.asarray(t, F32) for t in (cos, sina, sinb))


def _attn_kernel(q_ref, kc_ref, kp_ref, kn_ref, vc_ref, vp_ref, vn_ref, *refs,
                 sub_len, rows, n_res, n_cast):
    cast_src, (o_ref, lse_ref) = refs[:n_cast], refs[n_cast:n_cast + 2]
    cast_dst, (kbuf, vbuf) = refs[n_cast + 2:2 * n_cast + 2], refs[2 * n_cast + 2:]
    for src, dst in zip(cast_src, cast_dst):
        dst[...] = src[...].astype(BF16)

    ones = jnp.ones((rows + 2 * BAND, LANES), BF16)
    for res in range(n_res):
        c0 = res * D_ATTN
        kbuf[res, 0:BAND] = kp_ref[:, c0:c0 + D_ATTN]
        kbuf[res, BAND:BAND + rows] = kc_ref[:, c0:c0 + D_ATTN]
        kbuf[res, BAND + rows:] = kn_ref[:, c0:c0 + D_ATTN]
        for pair in range(N_SLABS):
            src = slice(c0 + pair * LANES, c0 + (pair + 1) * LANES)
            dst = slice(2 * pair * LANES, (2 * pair + 1) * LANES)
            vbuf[res, 0:BAND, dst] = vp_ref[:, src]
            vbuf[res, BAND:BAND + rows, dst] = vc_ref[:, src]
            vbuf[res, BAND + rows:, dst] = vn_ref[:, src]
            vbuf[res, :, (2 * pair + 1) * LANES:(2 * pair + 2) * LANES] = ones

    first_row = (pl.program_id(0) % (sub_len // rows)) * rows
    delta = (lax.broadcasted_iota(jnp.int32, (QB, KW), 1)
             - lax.broadcasted_iota(jnp.int32, (QB, KW), 0))
    band_bias = jnp.where((delta >= 0) & (delta <= 2 * BAND), 0.0, NEG_INF)
    key_col = lax.broadcasted_iota(jnp.int32, (1, KW), 1)
    lane = lax.broadcasted_iota(jnp.int32, (1, LANES), 1)
    low_head = lane < HEAD_DIM

    for j in range(rows // QB):
        row0 = j * QB
        key_pos = key_col + (first_row + row0 - BAND)
        bias = band_bias + jnp.where((key_pos >= 0) & (key_pos < sub_len), 0.0, NEG_INF)
        bias = jnp.concatenate([bias, bias], axis=0)
        for res in range(n_res):
            lse_tile = jnp.zeros((QB, LANES), F32)
            for pair in range(N_SLABS):
                cols = slice(res * D_ATTN + pair * LANES, res * D_ATTN + (pair + 1) * LANES)
                q2 = q_ref[row0:row0 + QB, cols]
                zero = jnp.zeros_like(q2)
                qs = jnp.concatenate([jnp.where(low_head, q2, zero), jnp.where(low_head, zero, q2)],
                                     axis=0)
                k2 = kbuf[res, row0:row0 + KW, pair * LANES:(pair + 1) * LANES]
                s = lax.dot_general(qs, k2, (((1,), (1,)), ((), ())),
                                    preferred_element_type=F32) + bias
                m = jnp.max(s, axis=-1, keepdims=True)
                p = jnp.exp2(s - m).astype(BF16)
                pv = jnp.dot(p, vbuf[res, row0:row0 + KW, 2 * pair * LANES:(2 * pair + 2) * LANES],
                             preferred_element_type=F32)
                num = jnp.where(low_head, pv[:QB, :LANES], pv[QB:, :LANES])
                den = jnp.where(low_head, pv[:QB, LANES:], pv[QB:, LANES:])
                o_ref[row0:row0 + QB, cols] = (num / den).astype(BF16)
                lse_pair = jnp.where(low_head, m[:QB], m[QB:]) * LN_2 + jnp.log(den)
                lse_tile = jnp.where((lane == pair) | (lane == HEAD_DIM + pair), lse_pair, lse_tile)
            lse_ref[row0:row0 + QB, res * LANES:(res + 1) * LANES] = lse_tile


def _attention(q, k, v, dil, seq, cast_jobs=()):
    total_rows = q.shape[0]
    sub_len = seq // dil
    rows = min(MAX_ATT_ROWS, sub_len)
    n_res = min(MAX_ATT_ROWS // rows, dil)
    assert sub_len % rows == 0 and rows % QB == 0 and dil % n_res == 0
    halos_per_step = rows // BAND
    last_halo = total_rows // BAND - 1
    width = n_res * D_ATTN
    grid = (total_rows // rows, dil // n_res)
    cur = pl.BlockSpec((rows, width), lambda i, r: (i, r))
    prev = pl.BlockSpec((BAND, width), lambda i, r: (jnp.maximum(i * halos_per_step - 1, 0), r))
    nxt = pl.BlockSpec((BAND, width),
                       lambda i, r: (jnp.minimum((i + 1) * halos_per_step, last_halo), r))
    n_steps = grid[0] * grid[1]
    cast_in, cast_out, cast_shapes = [], [], []
    for w, w_layer in cast_jobs:
        _, w_rows, w_cols = w.shape
        chunk = w_rows // n_steps
        assert chunk * n_steps == w_rows and chunk % (2 * SUBLANES) == 0
        cast_in.append(pl.BlockSpec((None, chunk, w_cols),
                                    lambda i, r, w_layer=w_layer: (w_layer, i * grid[1] + r, 0)))
        cast_out.append(pl.BlockSpec((chunk, w_cols), lambda i, r: (i * grid[1] + r, 0)))
        cast_shapes.append(jax.ShapeDtypeStruct((w_rows, w_cols), BF16))
    outs = pl.pallas_call(
        functools.partial(_attn_kernel, sub_len=sub_len, rows=rows, n_res=n_res,
                          n_cast=len(cast_jobs)),
        grid=grid,
        in_specs=[cur, cur, prev, nxt, cur, prev, nxt] + cast_in,
        out_specs=[cur, pl.BlockSpec((rows, n_res * LANES), lambda i, r: (i, r))] + cast_out,
        out_shape=[jax.ShapeDtypeStruct(q.shape, BF16),
                   jax.ShapeDtypeStruct((total_rows, dil * LANES), F32)] + cast_shapes,
        scratch_shapes=[pltpu.VMEM((n_res, rows + 2 * BAND, D_ATTN), BF16),
                        pltpu.VMEM((n_res, rows + 2 * BAND, 2 * D_ATTN), BF16)],
        compiler_params=_params(2),
        name=f"attn_d{dil}",
    )(q, k, k, k, v, v, v, *[w for w, _ in cast_jobs])
    return outs[0], outs[1], list(outs[2:])


def _merge_work(o_refs, l_refs, sg_ref, ga_ref, gs_ref, norm_ref, stage_refs):
    def unview_slab(s):
        def work():
            produced = []
            for p, dil in enumerate(DILATIONS[1:]):
                for r in range(dil):
                    blk = o_refs[p + 1][:, r * D_ATTN + s * LANES:r * D_ATTN + (s + 1) * LANES]
                    blk = blk.astype(F32)
                    stage_refs[2 * p][s, pl.ds(r, TM // dil, stride=dil), :] = blk
                    produced.append(blk)
            return produced
        return work

    def unview_lse():
        produced = []
        for p, dil in enumerate(DILATIONS[1:]):
            for r in range(dil):
                blk = l_refs[p + 1][:, r * LANES:(r + 1) * LANES]
                stage_refs[2 * p + 1][pl.ds(r, TM // dil, stride=dil), :] = blk
                produced.append(blk)
        return produced

    def merge_rows(c):
        def work():
            rows = slice(c * CHUNK, (c + 1) * CHUNK)
            low_head = lax.broadcasted_iota(jnp.int32, (1, LANES), 1) < HEAD_DIM
            lses = [l_refs[0][rows, :]] + [stage_refs[2 * p + 1][rows, :] for p in range(N_PAT - 1)]
            m = functools.reduce(jnp.maximum, lses)
            es = [jnp.exp(l - m) for l in lses]
            inv = 1.0 / sum(es)
            slabs = []
            for s in range(N_SLABS):
                cols = slice(s * LANES, (s + 1) * LANES)
                outs = [o_refs[0][rows, cols].astype(F32)]
                outs += [stage_refs[2 * p][s, rows, :] for p in range(N_PAT - 1)]
                acc = None
                for e, o in zip(es, outs):
                    w = e * inv
                    w = jnp.where(low_head, w[:, s:s + 1], w[:, HEAD_DIM + s:HEAD_DIM + s + 1])
                    acc = w * o if acc is None else acc + w * o
                slabs.append(acc)
            na = _rms(jnp.concatenate(slabs, axis=1), ga_ref[...])
            ns = _rms(sg_ref[rows, :].astype(F32), gs_ref[...])
            norm_ref[rows, 0:D_ATTN] = na.astype(BF16)
            norm_ref[rows, D_ATTN:] = ns.astype(BF16)
            return [na, ns]
        return work

    return ([unview_slab(s) for s in range(N_SLABS)] + [unview_lse]
            + [merge_rows(c) for c in range(TM // CHUNK)])


def _merge_ffn_kernel(x_ref, *refs, n_tiles, final):
    o_refs, l_refs = refs[0:2 * N_PAT:2], refs[1:2 * N_PAT:2]
    sg_ref, ga_ref, gs_ref, wo_ref = refs[2 * N_PAT:2 * N_PAT + 4]
    ffn_refs = refs[2 * N_PAT + 4:2 * N_PAT + 8]
    fgain_ref, out_ref, act_ref, norm_ref = refs[2 * N_PAT + 8:2 * N_PAT + 12]
    stage_refs = refs[2 * N_PAT + 12:]
    step = pl.program_id(0)
    merge = _merge_work(o_refs, l_refs, sg_ref, ga_ref, gs_ref, norm_ref, stage_refs)

    def project_and_ffn(side_work):
        x = x_ref[...] + jnp.dot(norm_ref[...], wo_ref[...], preferred_element_type=F32)
        y = _ffn_apply(x, *ffn_refs, act_ref, side_work=side_work)
        out_ref[...] = _rms(y, fgain_ref[...]) if final else y

    @pl.when(step == 0)
    def _():
        for work in merge:
            work()

    @pl.when((step > 0) & (step < n_tiles))
    def _():
        project_and_ffn(merge)

    @pl.when(step == n_tiles)
    def _():
        project_and_ffn(())


def _merge_ffn(x, attn, sg, ga, gs, wo, ffn_w, fgain, layer, final):
    tokens = x.shape[0]
    n_tiles = tokens // TM
    lead = lambda s: (jnp.minimum(s, n_tiles - 1), 0)
    lag = lambda s: (jnp.maximum(s - 1, 0), 0)
    views = [pl.BlockSpec((TM // d, d * width), lead) for d in DILATIONS for width in (D_ATTN, LANES)]
    return pl.pallas_call(
        functools.partial(_merge_ffn_kernel, n_tiles=n_tiles, final=final),
        grid=(n_tiles + 1,),
        in_specs=[pl.BlockSpec((TM, D_MODEL), lag)] + views + [pl.BlockSpec((TM, D_SGU), lead)]
                 + [_resident((1, D_ATTN), layer), _resident((1, D_SGU), layer),
                    _resident_weight((D_ATTN + D_SGU, D_MODEL))]
                 + _ffn_specs(layer) + [pl.BlockSpec((1, D_MODEL), lambda s: (0, 0))],
        out_specs=pl.BlockSpec((TM, D_MODEL), lag),
        out_shape=jax.ShapeDtypeStruct((tokens, D_MODEL), F32),
        scratch_shapes=[pltpu.VMEM((TM, D_FF), BF16), pltpu.VMEM((TM, D_ATTN + D_SGU), BF16)]
                       + [pltpu.VMEM((N_SLABS, TM, LANES), F32), pltpu.VMEM((TM, LANES), F32)] * (N_PAT - 1),
        compiler_params=_params(1),
        name="merge_ffn",
    )(x, *attn, sg, ga, gs, wo, *ffn_w, fgain)


def kernel(x, norm_ffn1, ffn1_w_gate, ffn1_w_up, ffn1_w_down, norm_mix, w_in, sgu_ln_g, sgu_ln_b,
           sgu_w, sgu_b, out_norm_attn, out_norm_sgu, w_out, norm_ffn2, ffn2_w_gate, ffn2_w_up,
           ffn2_w_down, final_norm):
    batch, seq, _ = x.shape
    tokens = batch * seq
    assert x.shape[2] == D_MODEL and tokens % TM == 0 and seq % TM == 0 and TM % CHUNK == 0

    row = lambda p: p[:, None, :]
    n1, n2, nm = row(norm_ffn1), row(norm_ffn2), row(norm_mix)
    lng, lnb = row(sgu_ln_g), row(sgu_ln_b)
    ga, gs = row(out_norm_attn), row(out_norm_sgu)
    sgu_w_b = sgu_w.astype(BF16)
    sgu_b_col = sgu_b[..., None]
    fgain = final_norm[None, :]
    rope = _rope_tables(seq)

    stacked = dict(gate1=ffn1_w_gate, up1=ffn1_w_up, down1=ffn1_w_down, w_in=w_in,
                   gate2=ffn2_w_gate, up2=ffn2_w_up, down2=ffn2_w_down, w_out=w_out)
    early = ("gate1", "up1", "down1", "w_in")
    next_layer_jobs = (("gate1", "up1"), ("gate2", "up2"), ("down1", "down2", "w_in", "w_out"))
    same_layer_jobs = (("gate2",), ("up2",), ("down2", "w_out"))
    bf16 = {(name, 0): stacked[name][0].astype(BF16) for name in early}

    xt = x.reshape(tokens, D_MODEL)
    for layer in range(DEPTH):
        ffn1_w = [bf16[(name, layer)] for name in ("gate1", "up1", "down1")]
        xt, qs, ks, vs, sg = _ffn_inproj(xt, (n1, *ffn1_w), nm, bf16[("w_in", layer)], rope, lng, lnb,
                                         sgu_w_b, sgu_b_col, layer, seq)
        attn = []
        for p, (q, k, v, dil) in enumerate(zip(qs, ks, vs, DILATIONS)):
            jobs = [(name, layer + 1) for name in next_layer_jobs[p]] if layer + 1 < DEPTH else []
            jobs += [(name, layer) for name in same_layer_jobs[p] if (name, layer) not in bf16]
            o, lse, cast = _attention(q, k, v, dil, seq, [(stacked[n], l) for n, l in jobs])
            attn += [o, lse]
            bf16.update(zip(jobs, cast))
        ffn2_w = [bf16[(name, layer)] for name in ("gate2", "up2", "down2")]
        xt = _merge_ffn(xt, attn, sg, ga, gs, bf16[("w_out", layer)], (n2, *ffn2_w), fgain, layer,
                        layer == DEPTH - 1)
    return xt.reshape(batch, seq, D_MODEL)
```

```python
import functools
import math

import numpy as np
import jax
import jax.numpy as jnp
from jax import lax
from jax.experimental import pallas as pl
from jax.experimental.pallas import tpu as pltpu

F32 = jnp.float32
BF16 = jnp.bfloat16

D_MODEL = 1024
DEPTH = 4
HEAD_DIM = 64
D_ATTN = 512
D_SGU = 512
N_SGU_GROUPS = 4
CHUNK = 128
D_IN = 3 * D_ATTN + 2 * D_SGU
D_FF = 2816
DILATIONS = (1, 4, 16)
N_PAT = len(DILATIONS)
BAND = 64
ROPE_THETA = 500000.0
ROPE_DIM = HEAD_DIM // 4
EPS = 1e-6
NEG_INF = -1e30
LOG2_E = math.log2(math.e)
LN_2 = math.log(2.0)

LANES = 128
SUBLANES = 8
V7X_VMEM_BYTES = 64 * 1024 * 1024
VMEM_LIMIT = V7X_VMEM_BYTES * 7 // 8

TM = 512
FF_CHUNK = 256
N_FF_CHUNKS = D_FF // FF_CHUNK
N_SLABS = D_ATTN // LANES
MAX_ATT_ROWS = 2048
QB = 2 * BAND
KW = QB + 2 * BAND


def _rms(x, gain):
    return x * lax.rsqrt(jnp.mean(x * x, axis=-1, keepdims=True) + EPS) * gain


def _params(n_axes):
    return pltpu.CompilerParams(dimension_semantics=("arbitrary",) * n_axes,
                                vmem_limit_bytes=VMEM_LIMIT)


def _resident(shape, layer):
    nd = len(shape)
    return pl.BlockSpec((None,) + shape, lambda *_: (layer,) + (0,) * nd,
                        pipeline_mode=pl.Buffered(1))


def _resident_weight(shape):
    return pl.BlockSpec(shape, lambda *_: (0,) * len(shape), pipeline_mode=pl.Buffered(1))


def _view_shape(tokens, dil, dtype):
    return jax.ShapeDtypeStruct((tokens // dil, dil * D_ATTN), dtype)


def _ordering_zero(values):
    acc = None
    for v in values:
        bits = pltpu.bitcast(v[0:SUBLANES, 0:LANES], jnp.uint32)
        acc = bits if acc is None else acc | bits
    return ((acc >> 16) >> 16)[0:1, :].astype(F32)


def _ffn_apply(x, gain_ref, wg_ref, wu_ref, wd_ref, act_ref, side_work=()):
    h = _rms(x, gain_ref[...]).astype(BF16)
    for c in range(N_FF_CHUNKS):
        cols = slice(c * FF_CHUNK, (c + 1) * FF_CHUNK)
        g = jnp.dot(h, wg_ref[:, cols], preferred_element_type=F32)
        u = jnp.dot(h, wu_ref[:, cols], preferred_element_type=F32)
        a = g * jax.nn.sigmoid(g) * u
        todo = side_work[c * len(side_work) // N_FF_CHUNKS:(c + 1) * len(side_work) // N_FF_CHUNKS]
        if todo:
            zero = _ordering_zero([v for work in todo for v in work()])
            a = a + jnp.concatenate([zero] * (FF_CHUNK // LANES), axis=1)
        act_ref[:, cols] = a.astype(BF16)
    return x + 0.5 * jnp.dot(act_ref[...], wd_ref[...], preferred_element_type=F32)


def _ffn_specs(layer):
    return [_resident((1, D_MODEL), layer), _resident_weight((D_MODEL, D_FF)),
            _resident_weight((D_MODEL, D_FF)), _resident_weight((D_FF, D_MODEL))]


def _gelu(x):
    return 0.5 * x * (1.0 + lax.erf(x * (1.0 / math.sqrt(2.0))))


def _inproj_finish_work(raw, cos_ref, sina_ref, sinb_ref, lng_ref, lnb_ref, sw_ref, sb_ref,
                        q_refs, k_refs, v_refs, sg_ref, stage_ref):
    def qkv_slab(col0, dst_refs, rotary, scale, s):
        def work():
            t = raw[col0 // D_ATTN][:, s * LANES:(s + 1) * LANES]
            if rotary:
                half = ROPE_DIM // 2
                t = (t * cos_ref[...] + pltpu.roll(t, half, 1) * sina_ref[...]
                     + pltpu.roll(t, LANES - half, 1) * sinb_ref[...])
            if scale != 1.0:
                t = t * scale
            dst_refs[0][:, s * LANES:(s + 1) * LANES] = t.astype(BF16)
            stage_ref[0, s] = t
            produced = [t]
            for k in range(1, N_PAT):
                d_prev, dil = DILATIONS[k - 1], DILATIONS[k]
                ratio, n_prev, n = dil // d_prev, TM // d_prev, TM // dil
                for e in range(d_prev):
                    for c in range(ratio):
                        r = c * d_prev + e
                        blk = stage_ref[k - 1, s, pl.ds(e * n_prev + c, n, stride=ratio), :]
                        dst_refs[k][:, r * D_ATTN + s * LANES:r * D_ATTN + (s + 1) * LANES] = (
                            blk.astype(BF16))
                        if k + 1 < N_PAT:
                            stage_ref[k, s, r * n:(r + 1) * n, :] = blk
                        produced.append(blk)
            return produced
        return work

    def sgu_chunk(c):
        def work():
            rows = slice(c * CHUNK, (c + 1) * CHUNK)
            u = _gelu(raw[3][rows, :])
            g = _gelu(raw[4][rows, :])
            gc = g - jnp.mean(g, axis=-1, keepdims=True)
            gn = (gc * lax.rsqrt(jnp.mean(gc * gc, axis=-1, keepdims=True) + EPS) * lng_ref[...]
                  + lnb_ref[...]).astype(BF16)
            group = D_SGU // N_SGU_GROUPS
            produced = []
            for gi in range(N_SGU_GROUPS):
                cols = slice(gi * group, (gi + 1) * group)
                mixed = jnp.dot(sw_ref[gi], gn[:, cols], preferred_element_type=F32) + sb_ref[gi]
                gated = u[:, cols] * mixed
                sg_ref[rows, cols] = gated.astype(BF16)
                produced.append(gated)
            return produced
        return work

    q_scale = LOG2_E / math.sqrt(HEAD_DIM)
    pieces = [qkv_slab(0, q_refs, True, q_scale, s) for s in range(N_SLABS)]
    pieces += [qkv_slab(D_ATTN, k_refs, True, 1.0, s) for s in range(N_SLABS)]
    pieces += [qkv_slab(2 * D_ATTN, v_refs, False, 1.0, s) for s in range(N_SLABS)]
    pieces += [sgu_chunk(c) for c in range(TM // CHUNK)]
    return pieces


def _ffn_inproj_kernel(x_ref, *refs):
    ffn_refs, gain_ref, w_ref, finish_refs = refs[:4], refs[4], refs[5], refs[6:13]
    xo_ref = refs[13]
    q_refs, k_refs, v_refs = (refs[14 + i * N_PAT:14 + (i + 1) * N_PAT] for i in range(3))
    sg_ref, act_ref, stage_ref = refs[14 + 3 * N_PAT:]
    x = _ffn_apply(x_ref[...], *ffn_refs, act_ref)
    xo_ref[...] = x
    h = _rms(x, gain_ref[...]).astype(BF16)
    raw = []
    pieces = _inproj_finish_work(raw, *finish_refs, q_refs, k_refs, v_refs, sg_ref, stage_ref)
    h_next = h
    for b in range(D_IN // D_ATTN):
        raw.append(jnp.dot(h if b < 2 else h_next, w_ref[:, b * D_ATTN:(b + 1) * D_ATTN],
                           preferred_element_type=F32))
        if 1 <= b <= 3:
            produced = [v for work in pieces[(b - 1) * N_SLABS:b * N_SLABS] for v in work()]
            zero = _ordering_zero(produced).astype(BF16)
            h_next = h + jnp.concatenate([zero] * (D_MODEL // LANES), axis=1)
    for work in pieces[3 * N_SLABS:]:
        work()


def _ffn_inproj(x, ffn_w, gain, w_in, rope, lng, lnb, sw, sb, layer, seq):
    tokens = x.shape[0]
    rows = lambda s: (s, 0)
    table = pl.BlockSpec((TM, LANES), lambda s: (s % (seq // TM), 0))
    views = [pl.BlockSpec((TM // d, d * D_ATTN), rows) for d in DILATIONS]
    view_shapes = [_view_shape(tokens, d, BF16) for d in DILATIONS]
    outs = pl.pallas_call(
        _ffn_inproj_kernel,
        grid=(tokens // TM,),
        in_specs=[pl.BlockSpec((TM, D_MODEL), rows)] + _ffn_specs(layer)
                 + [_resident((1, D_MODEL), layer), _resident_weight((D_MODEL, D_IN)),
                    table, table, table,
                    _resident((1, D_SGU), layer), _resident((1, D_SGU), layer),
                    _resident((N_SGU_GROUPS, CHUNK, CHUNK), layer),
                    _resident((N_SGU_GROUPS, CHUNK, 1), layer)],
        out_specs=[pl.BlockSpec((TM, D_MODEL), rows)] + views * 3 + [pl.BlockSpec((TM, D_SGU), rows)],
        out_shape=[jax.ShapeDtypeStruct((tokens, D_MODEL), F32)] + view_shapes * 3
                  + [jax.ShapeDtypeStruct((tokens, D_SGU), BF16)],
        scratch_shapes=[pltpu.VMEM((TM, D_FF), BF16), pltpu.VMEM((N_PAT - 1, N_SLABS, TM, LANES), F32)],
        compiler_params=_params(1),
        name="ffn_inproj",
    )(x, *ffn_w, gain, w_in, *rope, lng, lnb, sw, sb)
    return (outs[0], outs[1:1 + N_PAT], outs[1 + N_PAT:1 + 2 * N_PAT],
            outs[1 + 2 * N_PAT:1 + 3 * N_PAT], outs[-1])


def _rope_tables(seq):
    half = ROPE_DIM // 2
    inv_freq = ROPE_THETA ** (-np.arange(0, ROPE_DIM, 2, dtype=np.float64) / ROPE_DIM)
    ang = np.arange(seq, dtype=np.float64)[:, None] * inv_freq[None, :]
    cos = np.ones((seq, LANES))
    sina = np.zeros((seq, LANES))
    sinb = np.zeros((seq, LANES))
    for head0 in range(0, LANES, HEAD_DIM):
        cos[:, head0:head0 + half] = np.cos(ang)
        cos[:, head0 + half:head0 + 2 * half] = np.cos(ang)
        sinb[:, head0:head0 + half] = -np.sin(ang)
        sina[:, head0 + half:head0 + 2 * half] = np.sin(ang)
    return tuple(jnp.asarray(t, F32) for t in (cos, sina, sinb))


def _attn_kernel(q_ref, kc_ref, kp_ref, kn_ref, vc_ref, vp_ref, vn_ref, *refs,
                 sub_len, rows, n_res, n_cast):
    cast_src, (o_ref, lse_ref) = refs[:n_cast], refs[n_cast:n_cast + 2]
    cast_dst, (kbuf, vbuf) = refs[n_cast + 2:2 * n_cast + 2], refs[2 * n_cast + 2:]
    for src, dst in zip(cast_src, cast_dst):
        dst[...] = src[...].astype(BF16)

    ones = jnp.ones((rows + 2 * BAND, LANES), BF16)
    for res in range(n_res):
        c0 = res * D_ATTN
        kbuf[res, 0:BAND] = kp_ref[:, c0:c0 + D_ATTN]
        kbuf[res, BAND:BAND + rows] = kc_ref[:, c0:c0 + D_ATTN]
        kbuf[res, BAND + rows:] = kn_ref[:, c0:c0 + D_ATTN]
        for pair in range(N_SLABS):
            src = slice(c0 + pair * LANES, c0 + (pair + 1) * LANES)
            dst = slice(2 * pair * LANES, (2 * pair + 1) * LANES)
            vbuf[res, 0:BAND, dst] = vp_ref[:, src]
            vbuf[res, BAND:BAND + rows, dst] = vc_ref[:, src]
            vbuf[res, BAND + rows:, dst] = vn_ref[:, src]
            vbuf[res, :, (2 * pair + 1) * LANES:(2 * pair + 2) * LANES] = ones

    first_row = (pl.program_id(0) % (sub_len // rows)) * rows
    delta = (lax.broadcasted_iota(jnp.int32, (QB, KW), 1)
             - lax.broadcasted_iota(jnp.int32, (QB, KW), 0))
    band_bias = jnp.where((delta >= 0) & (delta <= 2 * BAND), 0.0, NEG_INF)
    key_col = lax.broadcasted_iota(jnp.int32, (1, KW), 1)
    lane = lax.broadcasted_iota(jnp.int32, (1, LANES), 1)
    low_head = lane < HEAD_DIM

    for j in range(rows // QB):
        row0 = j * QB
        key_pos = key_col + (first_row + row0 - BAND)
        bias = band_bias + jnp.where((key_pos >= 0) & (key_pos < sub_len), 0.0, NEG_INF)
        bias = jnp.concatenate([bias, bias], axis=0)
        for res in range(n_res):
            lse_tile = jnp.zeros((QB, LANES), F32)
            for pair in range(N_SLABS):
                cols = slice(res * D_ATTN + pair * LANES, res * D_ATTN + (pair + 1) * LANES)
                q2 = q_ref[row0:row0 + QB, cols]
                zero = jnp.zeros_like(q2)
                qs = jnp.concatenate([jnp.where(low_head, q2, zero), jnp.where(low_head, zero, q2)],
                                     axis=0)
                k2 = kbuf[res, row0:row0 + KW, pair * LANES:(pair + 1) * LANES]
                s = lax.dot_general(qs, k2, (((1,), (1,)), ((), ())),
                                    preferred_element_type=F32) + bias
                m = jnp.max(s, axis=-1, keepdims=True)
                p = jnp.exp2(s - m).astype(BF16)
                pv = jnp.dot(p, vbuf[res, row0:row0 + KW, 2 * pair * LANES:(2 * pair + 2) * LANES],
                             preferred_element_type=F32)
                num = jnp.where(low_head, pv[:QB, :LANES], pv[QB:, :LANES])
                den = jnp.where(low_head, pv[:QB, LANES:], pv[QB:, LANES:])
                o_ref[row0:row0 + QB, cols] = (num / den).astype(BF16)
                lse_pair = jnp.where(low_head, m[:QB], m[QB:]) * LN_2 + jnp.log(den)
                lse_tile = jnp.where((lane == pair) | (lane == HEAD_DIM + pair), lse_pair, lse_tile)
            lse_ref[row0:row0 + QB, res * LANES:(res + 1) * LANES] = lse_tile


def _attention(q, k, v, dil, seq, cast_jobs=()):
    total_rows = q.shape[0]
    sub_len = seq // dil
    rows = min(MAX_ATT_ROWS, sub_len)
    n_res = min(MAX_ATT_ROWS // rows, dil)
    assert sub_len % rows == 0 and rows % QB == 0 and dil % n_res == 0
    halos_per_step = rows // BAND
    last_halo = total_rows // BAND - 1
    width = n_res * D_ATTN
    grid = (total_rows // rows, dil // n_res)
    cur = pl.BlockSpec((rows, width), lambda i, r: (i, r))
    prev = pl.BlockSpec((BAND, width), lambda i, r: (jnp.maximum(i * halos_per_step - 1, 0), r))
    nxt = pl.BlockSpec((BAND, width),
                       lambda i, r: (jnp.minimum((i + 1) * halos_per_step, last_halo), r))
    n_steps = grid[0] * grid[1]
    cast_in, cast_out, cast_shapes = [], [], []
    for w, w_layer in cast_jobs:
        _, w_rows, w_cols = w.shape
        chunk = w_rows // n_steps
        assert chunk * n_steps == w_rows and chunk % (2 * SUBLANES) == 0
        cast_in.append(pl.BlockSpec((None, chunk, w_cols),
                                    lambda i, r, w_layer=w_layer: (w_layer, i * grid[1] + r, 0)))
        cast_out.append(pl.BlockSpec((chunk, w_cols), lambda i, r: (i * grid[1] + r, 0)))
        cast_shapes.append(jax.ShapeDtypeStruct((w_rows, w_cols), BF16))
    outs = pl.pallas_call(
        functools.partial(_attn_kernel, sub_len=sub_len, rows=rows, n_res=n_res,
                          n_cast=len(cast_jobs)),
        grid=grid,
        in_specs=[cur, cur, prev, nxt, cur, prev, nxt] + cast_in,
        out_specs=[cur, pl.BlockSpec((rows, n_res * LANES), lambda i, r: (i, r))] + cast_out,
        out_shape=[jax.ShapeDtypeStruct(q.shape, BF16),
                   jax.ShapeDtypeStruct((total_rows, dil * LANES), F32)] + cast_shapes,
        scratch_shapes=[pltpu.VMEM((n_res, rows + 2 * BAND, D_ATTN), BF16),
                        pltpu.VMEM((n_res, rows + 2 * BAND, 2 * D_ATTN), BF16)],
        compiler_params=_params(2),
        name=f"attn_d{dil}",
    )(q, k, k, k, v, v, v, *[w for w, _ in cast_jobs])
    return outs[0], outs[1], list(outs[2:])


def _merge_work(o_refs, l_refs, sg_ref, ga_ref, gs_ref, norm_ref, stage_refs):
    def unview_slab(s):
        def work():
            produced = []
            for p, dil in enumerate(DILATIONS[1:]):
                for r in range(dil):
                    blk = o_refs[p + 1][:, r * D_ATTN + s * LANES:r * D_ATTN + (s + 1) * LANES]
                    blk = blk.astype(F32)
                    stage_refs[2 * p][s, pl.ds(r, TM // dil, stride=dil), :] = blk
                    produced.append(blk)
            return produced
        return work

    def unview_lse():
        produced = []
        for p, dil in enumerate(DILATIONS[1:]):
            for r in range(dil):
                blk = l_refs[p + 1][:, r * LANES:(r + 1) * LANES]
                stage_refs[2 * p + 1][pl.ds(r, TM // dil, stride=dil), :] = blk
                produced.append(blk)
        return produced

    def merge_rows(c):
        def work():
            rows = slice(c * CHUNK, (c + 1) * CHUNK)
            low_head = lax.broadcasted_iota(jnp.int32, (1, LANES), 1) < HEAD_DIM
            lses = [l_refs[0][rows, :]] + [stage_refs[2 * p + 1][rows, :] for p in range(N_PAT - 1)]
            m = functools.reduce(jnp.maximum, lses)
            es = [jnp.exp(l - m) for l in lses]
            inv = 1.0 / sum(es)
            slabs = []
            for s in range(N_SLABS):
                cols = slice(s * LANES, (s + 1) * LANES)
                outs = [o_refs[0][rows, cols].astype(F32)]
                outs += [stage_refs[2 * p][s, rows, :] for p in range(N_PAT - 1)]
                acc = None
                for e, o in zip(es, outs):
                    w = e * inv
                    w = jnp.where(low_head, w[:, s:s + 1], w[:, HEAD_DIM + s:HEAD_DIM + s + 1])
                    acc = w * o if acc is None else acc + w * o
                slabs.append(acc)
            na = _rms(jnp.concatenate(slabs, axis=1), ga_ref[...])
            ns = _rms(sg_ref[rows, :].astype(F32), gs_ref[...])
            norm_ref[rows, 0:D_ATTN] = na.astype(BF16)
            norm_ref[rows, D_ATTN:] = ns.astype(BF16)
            return [na, ns]
        return work

    return ([unview_slab(s) for s in range(N_SLABS)] + [unview_lse]
            + [merge_rows(c) for c in range(TM // CHUNK)])


def _merge_ffn_kernel(x_ref, *refs, n_tiles, final):
    o_refs, l_refs = refs[0:2 * N_PAT:2], refs[1:2 * N_PAT:2]
    sg_ref, ga_ref, gs_ref, wo_ref = refs[2 * N_PAT:2 * N_PAT + 4]
    ffn_refs = refs[2 * N_PAT + 4:2 * N_PAT + 8]
    fgain_ref, out_ref, act_ref, norm_ref = refs[2 * N_PAT + 8:2 * N_PAT + 12]
    stage_refs = refs[2 * N_PAT + 12:]
    step = pl.program_id(0)
    merge = _merge_work(o_refs, l_refs, sg_ref, ga_ref, gs_ref, norm_ref, stage_refs)

    def project_and_ffn(side_work):
        x = x_ref[...] + jnp.dot(norm_ref[...], wo_ref[...], preferred_element_type=F32)
        y = _ffn_apply(x, *ffn_refs, act_ref, side_work=side_work)
        out_ref[...] = _rms(y, fgain_ref[...]) if final else y

    @pl.when(step == 0)
    def _():
        for work in merge:
            work()

    @pl.when((step > 0) & (step < n_tiles))
    def _():
        project_and_ffn(merge)

    @pl.when(step == n_tiles)
    def _():
        project_and_ffn(())


def _merge_ffn(x, attn, sg, ga, gs, wo, ffn_w, fgain, layer, final):
    tokens = x.shape[0]
    n_tiles = tokens // TM
    lead = lambda s: (jnp.minimum(s, n_tiles - 1), 0)
    lag = lambda s: (jnp.maximum(s - 1, 0), 0)
    views = [pl.BlockSpec((TM // d, d * width), lead) for d in DILATIONS for width in (D_ATTN, LANES)]
    return pl.pallas_call(
        functools.partial(_merge_ffn_kernel, n_tiles=n_tiles, final=final),
        grid=(n_tiles + 1,),
        in_specs=[pl.BlockSpec((TM, D_MODEL), lag)] + views + [pl.BlockSpec((TM, D_SGU), lead)]
                 + [_resident((1, D_ATTN), layer), _resident((1, D_SGU), layer),
                    _resident_weight((D_ATTN + D_SGU, D_MODEL))]
                 + _ffn_specs(layer) + [pl.BlockSpec((1, D_MODEL), lambda s: (0, 0))],
        out_specs=pl.BlockSpec((TM, D_MODEL), lag),
        out_shape=jax.ShapeDtypeStruct((tokens, D_MODEL), F32),
        scratch_shapes=[pltpu.VMEM((TM, D_FF), BF16), pltpu.VMEM((TM, D_ATTN + D_SGU), BF16)]
                       + [pltpu.VMEM((N_SLABS, TM, LANES), F32), pltpu.VMEM((TM, LANES), F32)] * (N_PAT - 1),
        compiler_params=_params(1),
        name="merge_ffn",
    )(x, *attn, sg, ga, gs, wo, *ffn_w, fgain)


def kernel(x, norm_ffn1, ffn1_w_gate, ffn1_w_up, ffn1_w_down, norm_mix, w_in, sgu_ln_g, sgu_ln_b,
           sgu_w, sgu_b, out_norm_attn, out_norm_sgu, w_out, norm_ffn2, ffn2_w_gate, ffn2_w_up,
           ffn2_w_down, final_norm):
    batch, seq, _ = x.shape
    tokens = batch * seq
    assert x.shape[2] == D_MODEL and tokens % TM == 0 and seq % TM == 0 and TM % CHUNK == 0

    row = lambda p: p[:, None, :]
    n1, n2, nm = row(norm_ffn1), row(norm_ffn2), row(norm_mix)
    lng, lnb = row(sgu_ln_g), row(sgu_ln_b)
    ga, gs = row(out_norm_attn), row(out_norm_sgu)
    sgu_w_b = sgu_w.astype(BF16)
    sgu_b_col = sgu_b[..., None]
    fgain = final_norm[None, :]
    rope = _rope_tables(seq)

    stacked = dict(gate1=ffn1_w_gate, up1=ffn1_w_up, down1=ffn1_w_down, w_in=w_in,
                   gate2=ffn2_w_gate, up2=ffn2_w_up, down2=ffn2_w_down, w_out=w_out)
    early = ("gate1", "up1", "down1", "w_in")
    next_layer_jobs = (("gate1", "up1"), ("gate2", "up2"), ("down1", "down2", "w_in", "w_out"))
    same_layer_jobs = (("gate2",), ("up2",), ("down2", "w_out"))
    bf16 = {(name, 0): stacked[name][0].astype(BF16) for name in early}

    xt = x.reshape(tokens, D_MODEL)
    for layer in range(DEPTH):
        ffn1_w = [bf16[(name, layer)] for name in ("gate1", "up1", "down1")]
        xt, qs, ks, vs, sg = _ffn_inproj(xt, (n1, *ffn1_w), nm, bf16[("w_in", layer)], rope, lng, lnb,
                                         sgu_w_b, sgu_b_col, layer, seq)
        attn = []
        for p, (q, k, v, dil) in enumerate(zip(qs, ks, vs, DILATIONS)):
            jobs = [(name, layer + 1) for name in next_layer_jobs[p]] if layer + 1 < DEPTH else []
            jobs += [(name, layer) for name in same_layer_jobs[p] if (name, layer) not in bf16]
            o, lse, cast = _attention(q, k, v, dil, seq, [(stacked[n], l) for n, l in jobs])
            attn += [o, lse]
            bf16.update(zip(jobs, cast))
        ffn2_w = [bf16[(name, layer)] for name in ("gate2", "up2", "down2")]
        xt = _merge_ffn(xt, attn, sg, ga, gs, bf16[("w_out", layer)], (n2, *ffn2_w), fgain, layer,
                        layer == DEPTH - 1)
    return xt.reshape(batch, seq, D_MODEL)
```

```python
import functools
import math

import numpy as np
import jax
import jax.numpy as jnp
from jax import lax
from jax.experimental import pallas as pl
from jax.experimental.pallas import tpu as pltpu

F32 = jnp.float32
BF16 = jnp.bfloat16

D_MODEL = 1024
DEPTH = 4
HEAD_DIM = 64
D_ATTN = 512
D_SGU = 512
N_SGU_GROUPS = 4
CHUNK = 128
D_IN = 3 * D_ATTN + 2 * D_SGU
D_FF = 2816
DILATIONS = (1, 4, 16)
N_PAT = len(DILATIONS)
BAND = 64
ROPE_THETA = 500000.0
ROPE_DIM = HEAD_DIM // 4
EPS = 1e-6
NEG_INF = -1e30
LOG2_E = math.log2(math.e)
LN_2 = math.log(2.0)

LANES = 128
SUBLANES = 8
V7X_VMEM_BYTES = 64 * 1024 * 1024
VMEM_LIMIT = V7X_VMEM_BYTES * 7 // 8

TM = 512
FF_CHUNK = 256
N_FF_CHUNKS = D_FF // FF_CHUNK
N_SLABS = D_ATTN // LANES
MAX_ATT_ROWS = 2048
QB = 2 * BAND
KW = QB + 2 * BAND


def _rms(x, gain):
    return x * lax.rsqrt(jnp.mean(x * x, axis=-1, keepdims=True) + EPS) * gain


def _params(n_axes):
    return pltpu.CompilerParams(dimension_semantics=("arbitrary",) * n_axes,
                                vmem_limit_bytes=VMEM_LIMIT)


def _resident(shape, layer):
    nd = len(shape)
    return pl.BlockSpec((None,) + shape, lambda *_: (layer,) + (0,) * nd,
                        pipeline_mode=pl.Buffered(1))


def _resident_weight(shape):
    return pl.BlockSpec(shape, lambda *_: (0,) * len(shape), pipeline_mode=pl.Buffered(1))


def _view_shape(tokens, dil, dtype):
    return jax.ShapeDtypeStruct((tokens // dil, dil * D_ATTN), dtype)


def _ordering_zero(values):
    acc = None
    for v in values:
        bits = pltpu.bitcast(v[0:SUBLANES, 0:LANES], jnp.uint32)
        acc = bits if acc is None else acc | bits
    return ((acc >> 16) >> 16)[0:1, :].astype(F32)


def _ffn_apply(x, gain_ref, wg_ref, wu_ref, wd_ref, act_ref, side_work=()):
    h = _rms(x, gain_ref[...]).astype(BF16)
    for c in range(N_FF_CHUNKS):
        cols = slice(c * FF_CHUNK, (c + 1) * FF_CHUNK)
        g = jnp.dot(h, wg_ref[:, cols], preferred_element_type=F32)
        u = jnp.dot(h, wu_ref[:, cols], preferred_element_type=F32)
        a = g * jax.nn.sigmoid(g) * u
        todo = side_work[c * len(side_work) // N_FF_CHUNKS:(c + 1) * len(side_work) // N_FF_CHUNKS]
        if todo:
            zero = _ordering_zero([v for work in todo for v in work()])
            a = a + jnp.concatenate([zero] * (FF_CHUNK // LANES), axis=1)
        act_ref[:, cols] = a.astype(BF16)
    return x + 0.5 * jnp.dot(act_ref[...], wd_ref[...], preferred_element_type=F32)


def _ffn_specs(layer):
    return [_resident((1, D_MODEL), layer), _resident_weight((D_MODEL, D_FF)),
            _resident_weight((D_MODEL, D_FF)), _resident_weight((D_FF, D_MODEL))]


def _gelu(x):
    return 0.5 * x * (1.0 + lax.erf(x * (1.0 / math.sqrt(2.0))))


def _inproj_finish_work(raw, cos_ref, sina_ref, sinb_ref, lng_ref, lnb_ref, sw_ref, sb_ref,
                        q_refs, k_refs, v_refs, sg_ref, stage_ref):
    def qkv_slab(col0, dst_refs, rotary, scale, s):
        def work():
            t = raw[col0 // D_ATTN][:, s * LANES:(s + 1) * LANES]
            if rotary:
                half = ROPE_DIM // 2
                t = (t * cos_ref[...] + pltpu.roll(t, half, 1) * sina_ref[...]
                     + pltpu.roll(t, LANES - half, 1) * sinb_ref[...])
            if scale != 1.0:
                t = t * scale
            dst_refs[0][:, s * LANES:(s + 1) * LANES] = t.astype(BF16)
            stage_ref[0, s] = t
            produced = [t]
            for k in range(1, N_PAT):
                d_prev, dil = DILATIONS[k - 1], DILATIONS[k]
                ratio, n_prev, n = dil // d_prev, TM // d_prev, TM // dil
                for e in range(d_prev):
                    for c in range(ratio):
                        r = c * d_prev + e
                        blk = stage_ref[k - 1, s, pl.ds(e * n_prev + c, n, stride=ratio), :]
                        dst_refs[k][:, r * D_ATTN + s * LANES:r * D_ATTN + (s + 1) * LANES] = (
                            blk.astype(BF16))
                        if k + 1 < N_PAT:
                            stage_ref[k, s, r * n:(r + 1) * n, :] = blk
                        produced.append(blk)
            return produced
        return work

    def sgu_chunk(c):
        def work():
            rows = slice(c * CHUNK, (c + 1) * CHUNK)
            u = _gelu(raw[3][rows, :])
            g = _gelu(raw[4][rows, :])
            gc = g - jnp.mean(g, axis=-1, keepdims=True)
            gn = (gc * lax.rsqrt(jnp.mean(gc * gc, axis=-1, keepdims=True) + EPS) * lng_ref[...]
                  + lnb_ref[...]).astype(BF16)
            group = D_SGU // N_SGU_GROUPS
            produced = []
            for gi in range(N_SGU_GROUPS):
                cols = slice(gi * group, (gi + 1) * group)
                mixed = jnp.dot(sw_ref[gi], gn[:, cols], preferred_element_type=F32) + sb_ref[gi]
                gated = u[:, cols] * mixed
                sg_ref[rows, cols] = gated.astype(BF16)
                produced.append(gated)
            return produced
        return work

    q_scale = LOG2_E / math.sqrt(HEAD_DIM)
    pieces = [qkv_slab(0, q_refs, True, q_scale, s) for s in range(N_SLABS)]
    pieces += [qkv_slab(D_ATTN, k_refs, True, 1.0, s) for s in range(N_SLABS)]
    pieces += [qkv_slab(2 * D_ATTN, v_refs, False, 1.0, s) for s in range(N_SLABS)]
    pieces += [sgu_chunk(c) for c in range(TM // CHUNK)]
    return pieces


def _ffn_inproj_kernel(x_ref, *refs):
    ffn_refs, gain_ref, w_ref, finish_refs = refs[:4], refs[4], refs[5], refs[6:13]
    xo_ref = refs[13]
    q_refs, k_refs, v_refs = (refs[14 + i * N_PAT:14 + (i + 1) * N_PAT] for i in range(3))
    sg_ref, act_ref, stage_ref = refs[14 + 3 * N_PAT:]
    x = _ffn_apply(x_ref[...], *ffn_refs, act_ref)
    xo_ref[...] = x
    h = _rms(x, gain_ref[...]).astype(BF16)
    raw = [jnp.dot(h, w_ref[:, c0:c0 + D_ATTN], preferred_element_type=F32)
           for c0 in range(0, D_IN, D_ATTN)]
    for work in _inproj_finish_work(raw, *finish_refs, q_refs, k_refs, v_refs, sg_ref, stage_ref):
        work()


def _ffn_inproj(x, ffn_w, gain, w_in, rope, lng, lnb, sw, sb, layer, seq):
    tokens = x.shape[0]
    rows = lambda s: (s, 0)
    table = pl.BlockSpec((TM, LANES), lambda s: (s % (seq // TM), 0))
    views = [pl.BlockSpec((TM // d, d * D_ATTN), rows) for d in DILATIONS]
    view_shapes = [_view_shape(tokens, d, BF16) for d in DILATIONS]
    outs = pl.pallas_call(
        _ffn_inproj_kernel,
        grid=(tokens // TM,),
        in_specs=[pl.BlockSpec((TM, D_MODEL), rows)] + _ffn_specs(layer)
                 + [_resident((1, D_MODEL), layer), _resident_weight((D_MODEL, D_IN)),
                    table, table, table,
                    _resident((1, D_SGU), layer), _resident((1, D_SGU), layer),
                    _resident((N_SGU_GROUPS, CHUNK, CHUNK), layer),
                    _resident((N_SGU_GROUPS, CHUNK, 1), layer)],
        out_specs=[pl.BlockSpec((TM, D_MODEL), rows)] + views * 3 + [pl.BlockSpec((TM, D_SGU), rows)],
        out_shape=[jax.ShapeDtypeStruct((tokens, D_MODEL), F32)] + view_shapes * 3
                  + [jax.ShapeDtypeStruct((tokens, D_SGU), BF16)],
        scratch_shapes=[pltpu.VMEM((TM, D_FF), BF16), pltpu.VMEM((N_PAT - 1, N_SLABS, TM, LANES), F32)],
        compiler_params=_params(1),
        name="ffn_inproj",
    )(x, *ffn_w, gain, w_in, *rope, lng, lnb, sw, sb)
    return (outs[0], outs[1:1 + N_PAT], outs[1 + N_PAT:1 + 2 * N_PAT],
            outs[1 + 2 * N_PAT:1 + 3 * N_PAT], outs[-1])


def _rope_tables(seq):
    half = ROPE_DIM // 2
    inv_freq = ROPE_THETA ** (-np.arange(0, ROPE_DIM, 2, dtype=np.float64) / ROPE_DIM)
    ang = np.arange(seq, dtype=np.float64)[:, None] * inv_freq[None, :]
    cos = np.ones((seq, LANES))
    sina = np.zeros((seq, LANES))
    sinb = np.zeros((seq, LANES))
    for head0 in range(0, LANES, HEAD_DIM):
        cos[:, head0:head0 + half] = np.cos(ang)
        cos[:, head0 + half:head0 + 2 * half] = np.cos(ang)
        sinb[:, head0:head0 + half] = -np.sin(ang)
        sina[:, head0 + half:head0 + 2 * half] = np.sin(ang)
    return tuple(jnp.asarray(t, F32) for t in (cos, sina, sinb))


def _attn_kernel(q_ref, kc_ref, kp_ref, kn_ref, vc_ref, vp_ref, vn_ref, *refs,
                 sub_len, rows, n_res, n_cast):
    cast_src, (o_ref, lse_ref) = refs[:n_cast], refs[n_cast:n_cast + 2]
    cast_dst, (kbuf, vbuf) = refs[n_cast + 2:2 * n_cast + 2], refs[2 * n_cast + 2:]
    for src, dst in zip(cast_src, cast_dst):
        dst[...] = src[...].astype(BF16)

    ones = jnp.ones((rows + 2 * BAND, LANES), BF16)
    for res in range(n_res):
        c0 = res * D_ATTN
        kbuf[res, 0:BAND] = kp_ref[:, c0:c0 + D_ATTN]
        kbuf[res, BAND:BAND + rows] = kc_ref[:, c0:c0 + D_ATTN]
        kbuf[res, BAND + rows:] = kn_ref[:, c0:c0 + D_ATTN]
        for pair in range(N_SLABS):
            src = slice(c0 + pair * LANES, c0 + (pair + 1) * LANES)
            dst = slice(2 * pair * LANES, (2 * pair + 1) * LANES)
            vbuf[res, 0:BAND, dst] = vp_ref[:, src]
            vbuf[res, BAND:BAND + rows, dst] = vc_ref[:, src]
            vbuf[res, BAND + rows:, dst] = vn_ref[:, src]
            vbuf[res, :, (2 * pair + 1) * LANES:(2 * pair + 2) * LANES] = ones

    first_row = (pl.program_id(0) % (sub_len // rows)) * rows
    delta = (lax.broadcasted_iota(jnp.int32, (QB, KW), 1)
             - lax.broadcasted_iota(jnp.int32, (QB, KW), 0))
    band_bias = jnp.where((delta >= 0) & (delta <= 2 * BAND), 0.0, NEG_INF)
    key_col = lax.broadcasted_iota(jnp.int32, (1, KW), 1)
    lane = lax.broadcasted_iota(jnp.int32, (1, LANES), 1)
    low_head = lane < HEAD_DIM

    for j in range(rows // QB):
        row0 = j * QB
        key_pos = key_col + (first_row + row0 - BAND)
        bias = band_bias + jnp.where((key_pos >= 0) & (key_pos < sub_len), 0.0, NEG_INF)
        bias = jnp.concatenate([bias, bias], axis=0)
        for res in range(n_res):
            lse_tile = jnp.zeros((QB, LANES), F32)
            for pair in range(N_SLABS):
                cols = slice(res * D_ATTN + pair * LANES, res * D_ATTN + (pair + 1) * LANES)
                q2 = q_ref[row0:row0 + QB, cols]
                zero = jnp.zeros_like(q2)
                qs = jnp.concatenate([jnp.where(low_head, q2, zero), jnp.where(low_head, zero, q2)],
                                     axis=0)
                k2 = kbuf[res, row0:row0 + KW, pair * LANES:(pair + 1) * LANES]
                s = lax.dot_general(qs, k2, (((1,), (1,)), ((), ())),
                                    preferred_element_type=F32) + bias
                m = jnp.max(s, axis=-1, keepdims=True)
                p = jnp.exp2(s - m).astype(BF16)
                pv = jnp.dot(p, vbuf[res, row0:row0 + KW, 2 * pair * LANES:(2 * pair + 2) * LANES],
                             preferred_element_type=F32)
                num = jnp.where(low_head, pv[:QB, :LANES], pv[QB:, :LANES])
                den = jnp.where(low_head, pv[:QB, LANES:], pv[QB:, LANES:])
                o_ref[row0:row0 + QB, cols] = (num / den).astype(BF16)
                lse_pair = jnp.where(low_head, m[:QB], m[QB:]) * LN_2 + jnp.log(den)
                lse_tile = jnp.where((lane == pair) | (lane == HEAD_DIM + pair), lse_pair, lse_tile)
            lse_ref[row0:row0 + QB, res * LANES:(res + 1) * LANES] = lse_tile


def _attention(q, k, v, dil, seq, cast_jobs=()):
    total_rows = q.shape[0]
    sub_len = seq // dil
    rows = min(MAX_ATT_ROWS, sub_len)
    n_res = min(MAX_ATT_ROWS // rows, dil)
    assert sub_len % rows == 0 and rows % QB == 0 and dil % n_res == 0
    halos_per_step = rows // BAND
    last_halo = total_rows // BAND - 1
    width = n_res * D_ATTN
    grid = (total_rows // rows, dil // n_res)
    cur = pl.BlockSpec((rows, width), lambda i, r: (i, r))
    prev = pl.BlockSpec((BAND, width), lambda i, r: (jnp.maximum(i * halos_per_step - 1, 0), r))
    nxt = pl.BlockSpec((BAND, width),
                       lambda i, r: (jnp.minimum((i + 1) * halos_per_step, last_halo), r))
    n_steps = grid[0] * grid[1]
    cast_in, cast_out, cast_shapes = [], [], []
    for w, w_layer in cast_jobs:
        _, w_rows, w_cols = w.shape
        chunk = w_rows // n_steps
        assert chunk * n_steps == w_rows and chunk % (2 * SUBLANES) == 0
        cast_in.append(pl.BlockSpec((None, chunk, w_cols),
                                    lambda i, r, w_layer=w_layer: (w_layer, i * grid[1] + r, 0)))
        cast_out.append(pl.BlockSpec((chunk, w_cols), lambda i, r: (i * grid[1] + r, 0)))
        cast_shapes.append(jax.ShapeDtypeStruct((w_rows, w_cols), BF16))
    outs = pl.pallas_call(
        functools.partial(_attn_kernel, sub_len=sub_len, rows=rows, n_res=n_res,
                          n_cast=len(cast_jobs)),
        grid=grid,
        in_specs=[cur, cur, prev, nxt, cur, prev, nxt] + cast_in,
        out_specs=[cur, pl.BlockSpec((rows, n_res * LANES), lambda i, r: (i, r))] + cast_out,
        out_shape=[jax.ShapeDtypeStruct(q.shape, BF16),
                   jax.ShapeDtypeStruct((total_rows, dil * LANES), F32)] + cast_shapes,
        scratch_shapes=[pltpu.VMEM((n_res, rows + 2 * BAND, D_ATTN), BF16),
                        pltpu.VMEM((n_res, rows + 2 * BAND, 2 * D_ATTN), BF16)],
        compiler_params=_params(2),
        name=f"attn_d{dil}",
    )(q, k, k, k, v, v, v, *[w for w, _ in cast_jobs])
    return outs[0], outs[1], list(outs[2:])


def _merge_work(o_refs, l_refs, sg_ref, ga_ref, gs_ref, norm_ref, stage_refs):
    def unview_slab(s):
        def work():
            produced = []
            for p, dil in enumerate(DILATIONS[1:]):
                for r in range(dil):
                    blk = o_refs[p + 1][:, r * D_ATTN + s * LANES:r * D_ATTN + (s + 1) * LANES]
                    blk = blk.astype(F32)
                    stage_refs[2 * p][s, pl.ds(r, TM // dil, stride=dil), :] = blk
                    produced.append(blk)
            return produced
        return work

    def unview_lse():
        produced = []
        for p, dil in enumerate(DILATIONS[1:]):
            for r in range(dil):
                blk = l_refs[p + 1][:, r * LANES:(r + 1) * LANES]
                stage_refs[2 * p + 1][pl.ds(r, TM // dil, stride=dil), :] = blk
                produced.append(blk)
        return produced

    def merge_rows(c):
        def work():
            rows = slice(c * CHUNK, (c + 1) * CHUNK)
            low_head = lax.broadcasted_iota(jnp.int32, (1, LANES), 1) < HEAD_DIM
            lses = [l_refs[0][rows, :]] + [stage_refs[2 * p + 1][rows, :] for p in range(N_PAT - 1)]
            m = functools.reduce(jnp.maximum, lses)
            es = [jnp.exp(l - m) for l in lses]
            inv = 1.0 / sum(es)
            slabs = []
            for s in range(N_SLABS):
                cols = slice(s * LANES, (s + 1) * LANES)
                outs = [o_refs[0][rows, cols].astype(F32)]
                outs += [stage_refs[2 * p][s, rows, :] for p in range(N_PAT - 1)]
                acc = None
                for e, o in zip(es, outs):
                    w = e * inv
                    w = jnp.where(low_head, w[:, s:s + 1], w[:, HEAD_DIM + s:HEAD_DIM + s + 1])
                    acc = w * o if acc is None else acc + w * o
                slabs.append(acc)
            na = _rms(jnp.concatenate(slabs, axis=1), ga_ref[...])
            ns = _rms(sg_ref[rows, :].astype(F32), gs_ref[...])
            norm_ref[rows, 0:D_ATTN] = na.astype(BF16)
            norm_ref[rows, D_ATTN:] = ns.astype(BF16)
            return [na, ns]
        return work

    return ([unview_slab(s) for s in range(N_SLABS)] + [unview_lse]
            + [merge_rows(c) for c in range(TM // CHUNK)])


def _merge_ffn_kernel(x_ref, *refs, n_tiles, final):
    o_refs, l_refs = refs[0:2 * N_PAT:2], refs[1:2 * N_PAT:2]
    sg_ref, ga_ref, gs_ref, wo_ref = refs[2 * N_PAT:2 * N_PAT + 4]
    ffn_refs = refs[2 * N_PAT + 4:2 * N_PAT + 8]
    fgain_ref, out_ref, act_ref, norm_ref = refs[2 * N_PAT + 8:2 * N_PAT + 12]
    stage_refs = refs[2 * N_PAT + 12:]
    step = pl.program_id(0)
    merge = _merge_work(o_refs, l_refs, sg_ref, ga_ref, gs_ref, norm_ref, stage_refs)

    def project_and_ffn(side_work):
        x = x_ref[...] + jnp.dot(norm_ref[...], wo_ref[...], preferred_element_type=F32)
        y = _ffn_apply(x, *ffn_refs, act_ref, side_work=side_work)
        out_ref[...] = _rms(y, fgain_ref[...]) if final else y

    @pl.when(step == 0)
    def _():
        for work in merge:
            work()

    @pl.when((step > 0) & (step < n_tiles))
    def _():
        project_and_ffn(merge)

    @pl.when(step == n_tiles)
    def _():
        project_and_ffn(())


def _merge_ffn(x, attn, sg, ga, gs, wo, ffn_w, fgain, layer, final):
    tokens = x.shape[0]
    n_tiles = tokens // TM
    lead = lambda s: (jnp.minimum(s, n_tiles - 1), 0)
    lag = lambda s: (jnp.maximum(s - 1, 0), 0)
    views = [pl.BlockSpec((TM // d, d * width), lead) for d in DILATIONS for width in (D_ATTN, LANES)]
    return pl.pallas_call(
        functools.partial(_merge_ffn_kernel, n_tiles=n_tiles, final=final),
        grid=(n_tiles + 1,),
        in_specs=[pl.BlockSpec((TM, D_MODEL), lag)] + views + [pl.BlockSpec((TM, D_SGU), lead)]
                 + [_resident((1, D_ATTN), layer), _resident((1, D_SGU), layer),
                    _resident_weight((D_ATTN + D_SGU, D_MODEL))]
                 + _ffn_specs(layer) + [pl.BlockSpec((1, D_MODEL), lambda s: (0, 0))],
        out_specs=pl.BlockSpec((TM, D_MODEL), lag),
        out_shape=jax.ShapeDtypeStruct((tokens, D_MODEL), F32),
        scratch_shapes=[pltpu.VMEM((TM, D_FF), BF16), pltpu.VMEM((TM, D_ATTN + D_SGU), BF16)]
                       + [pltpu.VMEM((N_SLABS, TM, LANES), F32), pltpu.VMEM((TM, LANES), F32)] * (N_PAT - 1),
        compiler_params=_params(1),
        name="merge_ffn",
    )(x, *attn, sg, ga, gs, wo, *ffn_w, fgain)


def kernel(x, norm_ffn1, ffn1_w_gate, ffn1_w_up, ffn1_w_down, norm_mix, w_in, sgu_ln_g, sgu_ln_b,
           sgu_w, sgu_b, out_norm_attn, out_norm_sgu, w_out, norm_ffn2, ffn2_w_gate, ffn2_w_up,
           ffn2_w_down, final_norm):
    batch, seq, _ = x.shape
    tokens = batch * seq
    assert x.shape[2] == D_MODEL and tokens % TM == 0 and seq % TM == 0 and TM % CHUNK == 0

    row = lambda p: p[:, None, :]
    n1, n2, nm = row(norm_ffn1), row(norm_ffn2), row(norm_mix)
    lng, lnb = row(sgu_ln_g), row(sgu_ln_b)
    ga, gs = row(out_norm_attn), row(out_norm_sgu)
    sgu_w_b = sgu_w.astype(BF16)
    sgu_b_col = sgu_b[..., None]
    fgain = final_norm[None, :]
    rope = _rope_tables(seq)

    stacked = dict(gate1=ffn1_w_gate, up1=ffn1_w_up, down1=ffn1_w_down, w_in=w_in,
                   gate2=ffn2_w_gate, up2=ffn2_w_up, down2=ffn2_w_down, w_out=w_out)
    early = ("gate1", "up1", "down1", "w_in")
    next_layer_jobs = (("gate1", "up1"), ("gate2", "up2"), ("down1", "down2", "w_in", "w_out"))
    same_layer_jobs = (("gate2",), ("up2",), ("down2", "w_out"))
    bf16 = {(name, 0): stacked[name][0].astype(BF16) for name in early}

    xt = x.reshape(tokens, D_MODEL)
    for layer in range(DEPTH):
        ffn1_w = [bf16[(name, layer)] for name in ("gate1", "up1", "down1")]
        xt, qs, ks, vs, sg = _ffn_inproj(xt, (n1, *ffn1_w), nm, bf16[("w_in", layer)], rope, lng, lnb,
                                         sgu_w_b, sgu_b_col, layer, seq)
        attn = []
        for p, (q, k, v, dil) in enumerate(zip(qs, ks, vs, DILATIONS)):
            jobs = [(name, layer + 1) for name in next_layer_jobs[p]] if layer + 1 < DEPTH else []
            jobs += [(name, layer) for name in same_layer_jobs[p] if (name, layer) not in bf16]
            o, lse, cast = _attention(q, k, v, dil, seq, [(stacked[n], l) for n, l in jobs])
            attn += [o, lse]
            bf16.update(zip(jobs, cast))
        ffn2_w = [bf16[(name, layer)] for name in ("gate2", "up2", "down2")]
        xt = _merge_ffn(xt, attn, sg, ga, gs, bf16[("w_out", layer)], (n2, *ffn2_w), fgain, layer,
                        layer == DEPTH - 1)
    return xt.reshape(batch, seq, D_MODEL)
```

```python
import functools
import math

import numpy as np
import jax
import jax.numpy as jnp
from jax import lax
from jax.experimental import pallas as pl
from jax.experimental.pallas import tpu as pltpu

F32 = jnp.float32
BF16 = jnp.bfloat16

D_MODEL = 1024
DEPTH = 4
HEAD_DIM = 64
D_ATTN = 512
D_SGU = 512
N_SGU_GROUPS = 4
CHUNK = 128
D_IN = 3 * D_ATTN + 2 * D_SGU
D_FF = 2816
DILATIONS = (1, 4, 16)
N_PAT = len(DILATIONS)
BAND = 64
ROPE_THETA = 500000.0
ROPE_DIM = HEAD_DIM // 4
EPS = 1e-6
NEG_INF = -1e30
LOG2_E = math.log2(math.e)
LN_2 = math.log(2.0)

LANES = 128
SUBLANES = 8
V7X_VMEM_BYTES = 64 * 1024 * 1024
VMEM_LIMIT = V7X_VMEM_BYTES * 7 // 8

TM = 512
FF_CHUNK = 256
N_FF_CHUNKS = D_FF // FF_CHUNK
N_SLABS = D_ATTN // LANES
MAX_ATT_ROWS = 2048
QB = 2 * BAND
KW = QB + 2 * BAND


def _rms(x, gain):
    return x * lax.rsqrt(jnp.mean(x * x, axis=-1, keepdims=True) + EPS) * gain


def _params(n_axes):
    return pltpu.CompilerParams(dimension_semantics=("arbitrary",) * n_axes,
                                vmem_limit_bytes=VMEM_LIMIT)


def _resident(shape, layer):
    nd = len(shape)
    return pl.BlockSpec((None,) + shape, lambda *_: (layer,) + (0,) * nd,
                        pipeline_mode=pl.Buffered(1))


def _resident_weight(shape):
    return pl.BlockSpec(shape, lambda *_: (0,) * len(shape), pipeline_mode=pl.Buffered(1))


def _view_shape(tokens, dil, dtype):
    return jax.ShapeDtypeStruct((tokens // dil, dil * D_ATTN), dtype)


def _ordering_zero(values):
    acc = None
    for v in values:
        bits = pltpu.bitcast(v[0:SUBLANES, 0:LANES], jnp.uint32)
        acc = bits if acc is None else acc | bits
    return ((acc >> 16) >> 16)[0:1, :].astype(F32)


def _ffn_apply(x, gain_ref, wg_ref, wu_ref, wd_ref, act_ref, side_work=()):
    h = _rms(x, gain_ref[...]).astype(BF16)
    for c in range(N_FF_CHUNKS):
        cols = slice(c * FF_CHUNK, (c + 1) * FF_CHUNK)
        g = jnp.dot(h, wg_ref[:, cols], preferred_element_type=F32)
        u = jnp.dot(h, wu_ref[:, cols], preferred_element_type=F32)
        a = g * jax.nn.sigmoid(g) * u
        todo = side_work[c * len(side_work) // N_FF_CHUNKS:(c + 1) * len(side_work) // N_FF_CHUNKS]
        if todo:
            zero = _ordering_zero([v for work in todo for v in work()])
            a = a + jnp.concatenate([zero] * (FF_CHUNK // LANES), axis=1)
        act_ref[:, cols] = a.astype(BF16)
    return x + 0.5 * jnp.dot(act_ref[...], wd_ref[...], preferred_element_type=F32)


def _ffn_specs(layer):
    return [_resident((1, D_MODEL), layer), _resident_weight((D_MODEL, D_FF)),
            _resident_weight((D_MODEL, D_FF)), _resident_weight((D_FF, D_MODEL))]


def _gelu(x):
    return 0.5 * x * (1.0 + lax.erf(x * (1.0 / math.sqrt(2.0))))


def _inproj_finish_work(raw, cos_ref, sina_ref, sinb_ref, lng_ref, lnb_ref, sw_ref, sb_ref,
                        q_refs, k_refs, v_refs, sg_ref, stage_ref):
    def qkv_slab(col0, dst_refs, rotary, scale, s):
        def work():
            t = raw[col0 // D_ATTN][:, s * LANES:(s + 1) * LANES]
            if rotary:
                half = ROPE_DIM // 2
                t = (t * cos_ref[...] + pltpu.roll(t, half, 1) * sina_ref[...]
                     + pltpu.roll(t, LANES - half, 1) * sinb_ref[...])
            if scale != 1.0:
                t = t * scale
            dst_refs[0][:, s * LANES:(s + 1) * LANES] = t.astype(BF16)
            stage_ref[0, s] = t
            produced = [t]
            for k in range(1, N_PAT):
                d_prev, dil = DILATIONS[k - 1], DILATIONS[k]
                ratio, n_prev, n = dil // d_prev, TM // d_prev, TM // dil
                for e in range(d_prev):
                    for c in range(ratio):
                        r = c * d_prev + e
                        blk = stage_ref[k - 1, s, pl.ds(e * n_prev + c, n, stride=ratio), :]
                        dst_refs[k][:, r * D_ATTN + s * LANES:r * D_ATTN + (s + 1) * LANES] = (
                            blk.astype(BF16))
                        if k + 1 < N_PAT:
                            stage_ref[k, s, r * n:(r + 1) * n, :] = blk
                        produced.append(blk)
            return produced
        return work

    def sgu_chunk(c):
        def work():
            rows = slice(c * CHUNK, (c + 1) * CHUNK)
            u = _gelu(raw[3][rows, :])
            g = _gelu(raw[4][rows, :])
            gc = g - jnp.mean(g, axis=-1, keepdims=True)
            gn = (gc * lax.rsqrt(jnp.mean(gc * gc, axis=-1, keepdims=True) + EPS) * lng_ref[...]
                  + lnb_ref[...]).astype(BF16)
            group = D_SGU // N_SGU_GROUPS
            produced = []
            for gi in range(N_SGU_GROUPS):
                cols = slice(gi * group, (gi + 1) * group)
                mixed = jnp.dot(sw_ref[gi], gn[:, cols], preferred_element_type=F32) + sb_ref[gi]
                gated = u[:, cols] * mixed
                sg_ref[rows, cols] = gated.astype(BF16)
                produced.append(gated)
            return produced
        return work

    q_scale = LOG2_E / math.sqrt(HEAD_DIM)
    pieces = [qkv_slab(0, q_refs, True, q_scale, s) for s in range(N_SLABS)]
    pieces += [qkv_slab(D_ATTN, k_refs, True, 1.0, s) for s in range(N_SLABS)]
    pieces += [qkv_slab(2 * D_ATTN, v_refs, False, 1.0, s) for s in range(N_SLABS)]
    pieces += [sgu_chunk(c) for c in range(TM // CHUNK)]
    return pieces


def _ffn_inproj_kernel(x_ref, *refs):
    ffn_refs, gain_ref, w_ref, finish_refs = refs[:4], refs[4], refs[5], refs[6:13]
    xo_ref = refs[13]
    q_refs, k_refs, v_refs = (refs[14 + i * N_PAT:14 + (i + 1) * N_PAT] for i in range(3))
    sg_ref, act_ref, stage_ref = refs[14 + 3 * N_PAT:]
    x = _ffn_apply(x_ref[...], *ffn_refs, act_ref)
    xo_ref[...] = x
    h = _rms(x, gain_ref[...]).astype(BF16)
    raw = [jnp.dot(h, w_ref[:, c0:c0 + D_ATTN], preferred_element_type=F32)
           for c0 in range(0, D_IN, D_ATTN)]
    for work in _inproj_finish_work(raw, *finish_refs, q_refs, k_refs, v_refs, sg_ref, stage_ref):
        work()


def _ffn_inproj(x, ffn_w, gain, w_in, rope, lng, lnb, sw, sb, layer, seq):
    tokens = x.shape[0]
    rows = lambda s: (s, 0)
    table = pl.BlockSpec((TM, LANES), lambda s: (s % (seq // TM), 0))
    views = [pl.BlockSpec((TM // d, d * D_ATTN), rows) for d in DILATIONS]
    view_shapes = [_view_shape(tokens, d, BF16) for d in DILATIONS]
    outs = pl.pallas_call(
        _ffn_inproj_kernel,
        grid=(tokens // TM,),
        in_specs=[pl.BlockSpec((TM, D_MODEL), rows)] + _ffn_specs(layer)
                 + [_resident((1, D_MODEL), layer), _resident_weight((D_MODEL, D_IN)),
                    table, table, table,
                    _resident((1, D_SGU), layer), _resident((1, D_SGU), layer),
                    _resident((N_SGU_GROUPS, CHUNK, CHUNK), layer),
                    _resident((N_SGU_GROUPS, CHUNK, 1), layer)],
        out_specs=[pl.BlockSpec((TM, D_MODEL), rows)] + views * 3 + [pl.BlockSpec((TM, D_SGU), rows)],
        out_shape=[jax.ShapeDtypeStruct((tokens, D_MODEL), F32)] + view_shapes * 3
                  + [jax.ShapeDtypeStruct((tokens, D_SGU), BF16)],
        scratch_shapes=[pltpu.VMEM((TM, D_FF), BF16), pltpu.VMEM((N_PAT - 1, N_SLABS, TM, LANES), F32)],
        compiler_params=_params(1),
        name="ffn_inproj",
    )(x, *ffn_w, gain, w_in, *rope, lng, lnb, sw, sb)
    return (outs[0], outs[1:1 + N_PAT], outs[1 + N_PAT:1 + 2 * N_PAT],
            outs[1 + 2 * N_PAT:1 + 3 * N_PAT], outs[-1])


def _rope_tables(seq):
    half = ROPE_DIM // 2
    inv_freq = ROPE_THETA ** (-np.arange(0, ROPE_DIM, 2, dtype=np.float64) / ROPE_DIM)
    ang = np.arange(seq, dtype=np.float64)[:, None] * inv_freq[None, :]
    cos = np.ones((seq, LANES))
    sina = np.zeros((seq, LANES))
    sinb = np.zeros((seq, LANES))
    for head0 in range(0, LANES, HEAD_DIM):
        cos[:, head0:head0 + half] = np.cos(ang)
        cos[:, head0 + half:head0 + 2 * half] = np.cos(ang)
        sinb[:, head0:head0 + half] = -np.sin(ang)
        sina[:, head0 + half:head0 + 2 * half] = np.sin(ang)
    return tuple(jnp.asarray(t, F32) for t in (cos, sina, sinb))


def _attn_kernel(q_ref, kc_ref, kp_ref, kn_ref, vc_ref, vp_ref, vn_ref, *refs,
                 sub_len, rows, n_res, n_cast):
    cast_src, (o_ref, lse_ref) = refs[:n_cast], refs[n_cast:n_cast + 2]
    cast_dst, (kbuf, vbuf) = refs[n_cast + 2:2 * n_cast + 2], refs[2 * n_cast + 2:]
    for src, dst in zip(cast_src, cast_dst):
        dst[...] = src[...].astype(BF16)

    ones = jnp.ones((rows + 2 * BAND, LANES), BF16)
    for res in range(n_res):
        c0 = res * D_ATTN
        kbuf[res, 0:BAND] = kp_ref[:, c0:c0 + D_ATTN]
        kbuf[res, BAND:BAND + rows] = kc_ref[:, c0:c0 + D_ATTN]
        kbuf[res, BAND + rows:] = kn_ref[:, c0:c0 + D_ATTN]
        for pair in range(N_SLABS):
            src = slice(c0 + pair * LANES, c0 + (pair + 1) * LANES)
            dst = slice(2 * pair * LANES, (2 * pair + 1) * LANES)
            vbuf[res, 0:BAND, dst] = vp_ref[:, src]
            vbuf[res, BAND:BAND + rows, dst] = vc_ref[:, src]
            vbuf[res, BAND + rows:, dst] = vn_ref[:, src]
            vbuf[res, :, (2 * pair + 1) * LANES:(2 * pair + 2) * LANES] = ones

    first_row = (pl.program_id(0) % (sub_len // rows)) * rows
    delta = (lax.broadcasted_iota(jnp.int32, (QB, KW), 1)
             - lax.broadcasted_iota(jnp.int32, (QB, KW), 0))
    band_bias = jnp.where((delta >= 0) & (delta <= 2 * BAND), 0.0, NEG_INF)
    key_col = lax.broadcasted_iota(jnp.int32, (1, KW), 1)
    lane = lax.broadcasted_iota(jnp.int32, (1, LANES), 1)
    low_head = lane < HEAD_DIM

    for j in range(rows // QB):
        row0 = j * QB
        key_pos = key_col + (first_row + row0 - BAND)
        bias = band_bias + jnp.where((key_pos >= 0) & (key_pos < sub_len), 0.0, NEG_INF)
        bias = jnp.concatenate([bias, bias], axis=0)
        for res in range(n_res):
            lse_tile = jnp.zeros((QB, LANES), F32)
            for pair in range(N_SLABS):
                cols = slice(res * D_ATTN + pair * LANES, res * D_ATTN + (pair + 1) * LANES)
                q2 = q_ref[row0:row0 + QB, cols]
                zero = jnp.zeros_like(q2)
                k2 = kbuf[res, row0:row0 + KW, pair * LANES:(pair + 1) * LANES]
                ps, ms = [], []
                for qh in (jnp.where(low_head, q2, zero), jnp.where(low_head, zero, q2)):
                    s = lax.dot_general(qh, k2, (((1,), (1,)), ((), ())),
                                        preferred_element_type=F32) + bias[:QB]
                    mh = jnp.max(s, axis=-1, keepdims=True)
                    ps.append(jnp.exp2(s - mh).astype(BF16))
                    ms.append(mh)
                p, m = jnp.concatenate(ps, axis=0), jnp.concatenate(ms, axis=0)
                pv = jnp.dot(p, vbuf[res, row0:row0 + KW, 2 * pair * LANES:(2 * pair + 2) * LANES],
                             preferred_element_type=F32)
                num = jnp.where(low_head, pv[:QB, :LANES], pv[QB:, :LANES])
                den = jnp.where(low_head, pv[:QB, LANES:], pv[QB:, LANES:])
                o_ref[row0:row0 + QB, cols] = (num / den).astype(BF16)
                lse_pair = jnp.where(low_head, m[:QB], m[QB:]) * LN_2 + jnp.log(den)
                lse_tile = jnp.where((lane == pair) | (lane == HEAD_DIM + pair), lse_pair, lse_tile)
            lse_ref[row0:row0 + QB, res * LANES:(res + 1) * LANES] = lse_tile


def _attention(q, k, v, dil, seq, cast_jobs=()):
    total_rows = q.shape[0]
    sub_len = seq // dil
    rows = min(MAX_ATT_ROWS, sub_len)
    n_res = min(MAX_ATT_ROWS // rows, dil)
    assert sub_len % rows == 0 and rows % QB == 0 and dil % n_res == 0
    halos_per_step = rows // BAND
    last_halo = total_rows // BAND - 1
    width = n_res * D_ATTN
    grid = (total_rows // rows, dil // n_res)
    cur = pl.BlockSpec((rows, width), lambda i, r: (i, r))
    prev = pl.BlockSpec((BAND, width), lambda i, r: (jnp.maximum(i * halos_per_step - 1, 0), r))
    nxt = pl.BlockSpec((BAND, width),
                       lambda i, r: (jnp.minimum((i + 1) * halos_per_step, last_halo), r))
    n_steps = grid[0] * grid[1]
    cast_in, cast_out, cast_shapes = [], [], []
    for w, w_layer in cast_jobs:
        _, w_rows, w_cols = w.shape
        chunk = w_rows // n_steps
        assert chunk * n_steps == w_rows and chunk % (2 * SUBLANES) == 0
        cast_in.append(pl.BlockSpec((None, chunk, w_cols),
                                    lambda i, r, w_layer=w_layer: (w_layer, i * grid[1] + r, 0)))
        cast_out.append(pl.BlockSpec((chunk, w_cols), lambda i, r: (i * grid[1] + r, 0)))
        cast_shapes.append(jax.ShapeDtypeStruct((w_rows, w_cols), BF16))
    outs = pl.pallas_call(
        functools.partial(_attn_kernel, sub_len=sub_len, rows=rows, n_res=n_res,
                          n_cast=len(cast_jobs)),
        grid=grid,
        in_specs=[cur, cur, prev, nxt, cur, prev, nxt] + cast_in,
        out_specs=[cur, pl.BlockSpec((rows, n_res * LANES), lambda i, r: (i, r))] + cast_out,
        out_shape=[jax.ShapeDtypeStruct(q.shape, BF16),
                   jax.ShapeDtypeStruct((total_rows, dil * LANES), F32)] + cast_shapes,
        scratch_shapes=[pltpu.VMEM((n_res, rows + 2 * BAND, D_ATTN), BF16),
                        pltpu.VMEM((n_res, rows + 2 * BAND, 2 * D_ATTN), BF16)],
        compiler_params=_params(2),
        name=f"attn_d{dil}",
    )(q, k, k, k, v, v, v, *[w for w, _ in cast_jobs])
    return outs[0], outs[1], list(outs[2:])


def _merge_work(o_refs, l_refs, sg_ref, ga_ref, gs_ref, norm_ref, stage_refs):
    def unview_slab(s):
        def work():
            produced = []
            for p, dil in enumerate(DILATIONS[1:]):
                for r in range(dil):
                    blk = o_refs[p + 1][:, r * D_ATTN + s * LANES:r * D_ATTN + (s + 1) * LANES]
                    blk = blk.astype(F32)
                    stage_refs[2 * p][s, pl.ds(r, TM // dil, stride=dil), :] = blk
                    produced.append(blk)
            return produced
        return work

    def unview_lse():
        produced = []
        for p, dil in enumerate(DILATIONS[1:]):
            for r in range(dil):
                blk = l_refs[p + 1][:, r * LANES:(r + 1) * LANES]
                stage_refs[2 * p + 1][pl.ds(r, TM // dil, stride=dil), :] = blk
                produced.append(blk)
        return produced

    def merge_rows(c):
        def work():
            rows = slice(c * CHUNK, (c + 1) * CHUNK)
            low_head = lax.broadcasted_iota(jnp.int32, (1, LANES), 1) < HEAD_DIM
            lses = [l_refs[0][rows, :]] + [stage_refs[2 * p + 1][rows, :] for p in range(N_PAT - 1)]
            m = functools.reduce(jnp.maximum, lses)
            es = [jnp.exp(l - m) for l in lses]
            inv = 1.0 / sum(es)
            slabs = []
            for s in range(N_SLABS):
                cols = slice(s * LANES, (s + 1) * LANES)
                outs = [o_refs[0][rows, cols].astype(F32)]
                outs += [stage_refs[2 * p][s, rows, :] for p in range(N_PAT - 1)]
                acc = None
                for e, o in zip(es, outs):
                    w = e * inv
                    w = jnp.where(low_head, w[:, s:s + 1], w[:, HEAD_DIM + s:HEAD_DIM + s + 1])
                    acc = w * o if acc is None else acc + w * o
                slabs.append(acc)
            na = _rms(jnp.concatenate(slabs, axis=1), ga_ref[...])
            ns = _rms(sg_ref[rows, :].astype(F32), gs_ref[...])
            norm_ref[rows, 0:D_ATTN] = na.astype(BF16)
            norm_ref[rows, D_ATTN:] = ns.astype(BF16)
            return [na, ns]
        return work

    return ([unview_slab(s) for s in range(N_SLABS)] + [unview_lse]
            + [merge_rows(c) for c in range(TM // CHUNK)])


def _merge_ffn_kernel(x_ref, *refs, n_tiles, final):
    o_refs, l_refs = refs[0:2 * N_PAT:2], refs[1:2 * N_PAT:2]
    sg_ref, ga_ref, gs_ref, wo_ref = refs[2 * N_PAT:2 * N_PAT + 4]
    ffn_refs = refs[2 * N_PAT + 4:2 * N_PAT + 8]
    fgain_ref, out_ref, act_ref, norm_ref = refs[2 * N_PAT + 8:2 * N_PAT + 12]
    stage_refs = refs[2 * N_PAT + 12:]
    step = pl.program_id(0)
    merge = _merge_work(o_refs, l_refs, sg_ref, ga_ref, gs_ref, norm_ref, stage_refs)

    def project_and_ffn(side_work):
        x = x_ref[...] + jnp.dot(norm_ref[...], wo_ref[...], preferred_element_type=F32)
        y = _ffn_apply(x, *ffn_refs, act_ref, side_work=side_work)
        out_ref[...] = _rms(y, fgain_ref[...]) if final else y

    @pl.when(step == 0)
    def _():
        for work in merge:
            work()

    @pl.when((step > 0) & (step < n_tiles))
    def _():
        project_and_ffn(merge)

    @pl.when(step == n_tiles)
    def _():
        project_and_ffn(())


def _merge_ffn(x, attn, sg, ga, gs, wo, ffn_w, fgain, layer, final):
    tokens = x.shape[0]
    n_tiles = tokens // TM
    lead = lambda s: (jnp.minimum(s, n_tiles - 1), 0)
    lag = lambda s: (jnp.maximum(s - 1, 0), 0)
    views = [pl.BlockSpec((TM // d, d * width), lead) for d in DILATIONS for width in (D_ATTN, LANES)]
    return pl.pallas_call(
        functools.partial(_merge_ffn_kernel, n_tiles=n_tiles, final=final),
        grid=(n_tiles + 1,),
        in_specs=[pl.BlockSpec((TM, D_MODEL), lag)] + views + [pl.BlockSpec((TM, D_SGU), lead)]
                 + [_resident((1, D_ATTN), layer), _resident((1, D_SGU), layer),
                    _resident_weight((D_ATTN + D_SGU, D_MODEL))]
                 + _ffn_specs(layer) + [pl.BlockSpec((1, D_MODEL), lambda s: (0, 0))],
        out_specs=pl.BlockSpec((TM, D_MODEL), lag),
        out_shape=jax.ShapeDtypeStruct((tokens, D_MODEL), F32),
        scratch_shapes=[pltpu.VMEM((TM, D_FF), BF16), pltpu.VMEM((TM, D_ATTN + D_SGU), BF16)]
                       + [pltpu.VMEM((N_SLABS, TM, LANES), F32), pltpu.VMEM((TM, LANES), F32)] * (N_PAT - 1),
        compiler_params=_params(1),
        name="merge_ffn",
    )(x, *attn, sg, ga, gs, wo, *ffn_w, fgain)


def kernel(x, norm_ffn1, ffn1_w_gate, ffn1_w_up, ffn1_w_down, norm_mix, w_in, sgu_ln_g, sgu_ln_b,
           sgu_w, sgu_b, out_norm_attn, out_norm_sgu, w_out, norm_ffn2, ffn2_w_gate, ffn2_w_up,
           ffn2_w_down, final_norm):
    batch, seq, _ = x.shape
    tokens = batch * seq
    assert x.shape[2] == D_MODEL and tokens % TM == 0 and seq % TM == 0 and TM % CHUNK == 0

    row = lambda p: p[:, None, :]
    n1, n2, nm = row(norm_ffn1), row(norm_ffn2), row(norm_mix)
    lng, lnb = row(sgu_ln_g), row(sgu_ln_b)
    ga, gs = row(out_norm_attn), row(out_norm_sgu)
    sgu_w_b = sgu_w.astype(BF16)
    sgu_b_col = sgu_b[..., None]
    fgain = final_norm[None, :]
    rope = _rope_tables(seq)

    stacked = dict(gate1=ffn1_w_gate, up1=ffn1_w_up, down1=ffn1_w_down, w_in=w_in,
                   gate2=ffn2_w_gate, up2=ffn2_w_up, down2=ffn2_w_down, w_out=w_out)
    early = ("gate1", "up1", "down1", "w_in")
    next_layer_jobs = (("gate1", "up1"), ("gate2", "up2"), ("down1", "down2", "w_in", "w_out"))
    same_layer_jobs = (("gate2",), ("up2",), ("down2", "w_out"))
    bf16 = {(name, 0): stacked[name][0].astype(BF16) for name in early}

    xt = x.reshape(tokens, D_MODEL)
    for layer in range(DEPTH):
        ffn1_w = [bf16[(name, layer)] for name in ("gate1", "up1", "down1")]
        xt, qs, ks, vs, sg = _ffn_inproj(xt, (n1, *ffn1_w), nm, bf16[("w_in", layer)], rope, lng, lnb,
                                         sgu_w_b, sgu_b_col, layer, seq)
        attn = []
        for p, (q, k, v, dil) in enumerate(zip(qs, ks, vs, DILATIONS)):
            jobs = [(name, layer + 1) for name in next_layer_jobs[p]] if layer + 1 < DEPTH else []
            jobs += [(name, layer) for name in same_layer_jobs[p] if (name, layer) not in bf16]
            o, lse, cast = _attention(q, k, v, dil, seq, [(stacked[n], l) for n, l in jobs])
            attn += [o, lse]
            bf16.update(zip(jobs, cast))
        ffn2_w = [bf16[(name, layer)] for name in ("gate2", "up2", "down2")]
        xt = _merge_ffn(xt, attn, sg, ga, gs, bf16[("w_out", layer)], (n2, *ffn2_w), fgain, layer,
                        layer == DEPTH - 1)
    return xt.reshape(batch, seq, D_MODEL)
```

```python
import functools
import math

import numpy as np
import jax
import jax.numpy as jnp
from jax import lax
from jax.experimental import pallas as pl
from jax.experimental.pallas import tpu as pltpu

F32 = jnp.float32
BF16 = jnp.bfloat16

D_MODEL = 1024
DEPTH = 4
HEAD_DIM = 64
D_ATTN = 512
D_SGU = 512
N_SGU_GROUPS = 4
CHUNK = 128
D_IN = 3 * D_ATTN + 2 * D_SGU
D_FF = 2816
DILATIONS = (1, 4, 16)
N_PAT = len(DILATIONS)
BAND = 64
ROPE_THETA = 500000.0
ROPE_DIM = HEAD_DIM // 4
EPS = 1e-6
NEG_INF = -1e30
LOG2_E = math.log2(math.e)
LN_2 = math.log(2.0)

LANES = 128
SUBLANES = 8
V7X_VMEM_BYTES = 64 * 1024 * 1024
VMEM_LIMIT = V7X_VMEM_BYTES * 7 // 8

TM = 512
FF_CHUNK = 256
N_FF_CHUNKS = D_FF // FF_CHUNK
N_SLABS = D_ATTN // LANES
MAX_ATT_ROWS = 2048
QB = 2 * BAND
KW = QB + 2 * BAND


def _rms(x, gain):
    return x * lax.rsqrt(jnp.mean(x * x, axis=-1, keepdims=True) + EPS) * gain


def _params(n_axes):
    return pltpu.CompilerParams(dimension_semantics=("arbitrary",) * n_axes,
                                vmem_limit_bytes=VMEM_LIMIT)


def _resident(shape, layer):
    nd = len(shape)
    return pl.BlockSpec((None,) + shape, lambda *_: (layer,) + (0,) * nd,
                        pipeline_mode=pl.Buffered(1))


def _resident_weight(shape):
    return pl.BlockSpec(shape, lambda *_: (0,) * len(shape), pipeline_mode=pl.Buffered(1))


def _view_shape(tokens, dil, dtype):
    return jax.ShapeDtypeStruct((tokens // dil, dil * D_ATTN), dtype)


def _ordering_zero(values):
    acc = None
    for v in values:
        bits = pltpu.bitcast(v[0:SUBLANES, 0:LANES], jnp.uint32)
        acc = bits if acc is None else acc | bits
    return ((acc >> 16) >> 16)[0:1, :].astype(F32)


def _ffn_apply(x, gain_ref, wg_ref, wu_ref, wd_ref, act_ref, side_work=()):
    h = _rms(x, gain_ref[...]).astype(BF16)
    for c in range(N_FF_CHUNKS):
        cols = slice(c * FF_CHUNK, (c + 1) * FF_CHUNK)
        g = jnp.dot(h, wg_ref[:, cols], preferred_element_type=F32)
        u = jnp.dot(h, wu_ref[:, cols], preferred_element_type=F32)
        a = g * jax.nn.sigmoid(g) * u
        todo = side_work[c * len(side_work) // N_FF_CHUNKS:(c + 1) * len(side_work) // N_FF_CHUNKS]
        if todo:
            zero = _ordering_zero([v for work in todo for v in work()])
            a = a + jnp.concatenate([zero] * (FF_CHUNK // LANES), axis=1)
        act_ref[:, cols] = a.astype(BF16)
    return x + 0.5 * jnp.dot(act_ref[...], wd_ref[...], preferred_element_type=F32)


def _ffn_specs(layer):
    return [_resident((1, D_MODEL), layer), _resident_weight((D_MODEL, D_FF)),
            _resident_weight((D_MODEL, D_FF)), _resident_weight((D_FF, D_MODEL))]


def _gelu(x):
    return 0.5 * x * (1.0 + lax.erf(x * (1.0 / math.sqrt(2.0))))


def _spatial_gating(u_raw, g_raw, lng_ref, lnb_ref, sw_ref, sb_ref):
    u = _gelu(u_raw)
    g = _gelu(g_raw)
    gc = g - jnp.mean(g, axis=-1, keepdims=True)
    gn = (gc * lax.rsqrt(jnp.mean(gc * gc, axis=-1, keepdims=True) + EPS) * lng_ref[...]
          + lnb_ref[...]).astype(BF16)
    group = D_SGU // N_SGU_GROUPS
    gated = []
    for gi in range(N_SGU_GROUPS):
        cols = slice(gi * group, (gi + 1) * group)
        mixed = jnp.dot(sw_ref[gi], gn[:, cols], preferred_element_type=F32) + sb_ref[gi]
        gated.append(u[:, cols] * mixed)
    return gated


def _inproj_finish_work(raw, cos_ref, sina_ref, sinb_ref, q_refs, k_refs, v_refs, ug_ref, stage_ref):
    def qkv_slab(col0, dst_refs, rotary, scale, s):
        def work():
            t = raw[col0 // D_ATTN][:, s * LANES:(s + 1) * LANES]
            if rotary:
                half = ROPE_DIM // 2
                t = (t * cos_ref[...] + pltpu.roll(t, half, 1) * sina_ref[...]
                     + pltpu.roll(t, LANES - half, 1) * sinb_ref[...])
            if scale != 1.0:
                t = t * scale
            dst_refs[0][:, s * LANES:(s + 1) * LANES] = t.astype(BF16)
            stage_ref[0, s] = t
            produced = [t]
            for k in range(1, N_PAT):
                d_prev, dil = DILATIONS[k - 1], DILATIONS[k]
                ratio, n_prev, n = dil // d_prev, TM // d_prev, TM // dil
                for e in range(d_prev):
                    for c in range(ratio):
                        r = c * d_prev + e
                        blk = stage_ref[k - 1, s, pl.ds(e * n_prev + c, n, stride=ratio), :]
                        dst_refs[k][:, r * D_ATTN + s * LANES:r * D_ATTN + (s + 1) * LANES] = (
                            blk.astype(BF16))
                        if k + 1 < N_PAT:
                            stage_ref[k, s, r * n:(r + 1) * n, :] = blk
                        produced.append(blk)
            return produced
        return work

    def gating_inputs():
        ug_ref[:, 0:D_SGU] = raw[3].astype(BF16)
        ug_ref[:, D_SGU:] = raw[4].astype(BF16)
        return [raw[3], raw[4]]

    q_scale = LOG2_E / math.sqrt(HEAD_DIM)
    pieces = [qkv_slab(0, q_refs, True, q_scale, s) for s in range(N_SLABS)]
    pieces += [qkv_slab(D_ATTN, k_refs, True, 1.0, s) for s in range(N_SLABS)]
    pieces += [qkv_slab(2 * D_ATTN, v_refs, False, 1.0, s) for s in range(N_SLABS)]
    pieces += [gating_inputs]
    return pieces


def _ffn_inproj_kernel(x_ref, *refs):
    ffn_refs, gain_ref, w_ref, table_refs = refs[:4], refs[4], refs[5], refs[6:9]
    xo_ref = refs[9]
    q_refs, k_refs, v_refs = (refs[10 + i * N_PAT:10 + (i + 1) * N_PAT] for i in range(3))
    ug_ref, act_ref, stage_ref = refs[10 + 3 * N_PAT:]
    x = _ffn_apply(x_ref[...], *ffn_refs, act_ref)
    xo_ref[...] = x
    h = _rms(x, gain_ref[...]).astype(BF16)
    raw = [jnp.dot(h, w_ref[:, c0:c0 + D_ATTN], preferred_element_type=F32)
           for c0 in range(0, D_IN, D_ATTN)]
    for work in _inproj_finish_work(raw, *table_refs, q_refs, k_refs, v_refs, ug_ref, stage_ref):
        work()


def _ffn_inproj(x, ffn_w, gain, w_in, rope, layer, seq):
    tokens = x.shape[0]
    rows = lambda s: (s, 0)
    table = pl.BlockSpec((TM, LANES), lambda s: (s % (seq // TM), 0))
    views = [pl.BlockSpec((TM // d, d * D_ATTN), rows) for d in DILATIONS]
    view_shapes = [_view_shape(tokens, d, BF16) for d in DILATIONS]
    outs = pl.pallas_call(
        _ffn_inproj_kernel,
        grid=(tokens // TM,),
        in_specs=[pl.BlockSpec((TM, D_MODEL), rows)] + _ffn_specs(layer)
                 + [_resident((1, D_MODEL), layer), _resident_weight((D_MODEL, D_IN)),
                    table, table, table],
        out_specs=[pl.BlockSpec((TM, D_MODEL), rows)] + views * 3
                  + [pl.BlockSpec((TM, 2 * D_SGU), rows)],
        out_shape=[jax.ShapeDtypeStruct((tokens, D_MODEL), F32)] + view_shapes * 3
                  + [jax.ShapeDtypeStruct((tokens, 2 * D_SGU), BF16)],
        scratch_shapes=[pltpu.VMEM((TM, D_FF), BF16), pltpu.VMEM((N_PAT - 1, N_SLABS, TM, LANES), F32)],
        compiler_params=_params(1),
        name="ffn_inproj",
    )(x, *ffn_w, gain, w_in, *rope)
    return (outs[0], outs[1:1 + N_PAT], outs[1 + N_PAT:1 + 2 * N_PAT],
            outs[1 + 2 * N_PAT:1 + 3 * N_PAT], outs[-1])


def _rope_tables(seq):
    half = ROPE_DIM // 2
    inv_freq = ROPE_THETA ** (-np.arange(0, ROPE_DIM, 2, dtype=np.float64) / ROPE_DIM)
    ang = np.arange(seq, dtype=np.float64)[:, None] * inv_freq[None, :]
    cos = np.ones((seq, LANES))
    sina = np.zeros((seq, LANES))
    sinb = np.zeros((seq, LANES))
    for head0 in range(0, LANES, HEAD_DIM):
        cos[:, head0:head0 + half] = np.cos(ang)
        cos[:, head0 + half:head0 + 2 * half] = np.cos(ang)
        sinb[:, head0:head0 + half] = -np.sin(ang)
        sina[:, head0 + half:head0 + 2 * half] = np.sin(ang)
    return tuple(jnp.asarray(t, F32) for t in (cos, sina, sinb))


def _attn_kernel(q_ref, kc_ref, kp_ref, kn_ref, vc_ref, vp_ref, vn_ref, *refs,
                 sub_len, rows, n_res, n_cast):
    cast_src, (o_ref, lse_ref) = refs[:n_cast], refs[n_cast:n_cast + 2]
    cast_dst, (kbuf, vbuf) = refs[n_cast + 2:2 * n_cast + 2], refs[2 * n_cast + 2:]
    for src, dst in zip(cast_src, cast_dst):
        dst[...] = src[...].astype(BF16)

    ones = jnp.ones((rows + 2 * BAND, LANES), BF16)
    for res in range(n_res):
        c0 = res * D_ATTN
        kbuf[res, 0:BAND] = kp_ref[:, c0:c0 + D_ATTN]
        kbuf[res, BAND:BAND + rows] = kc_ref[:, c0:c0 + D_ATTN]
        kbuf[res, BAND + rows:] = kn_ref[:, c0:c0 + D_ATTN]
        for pair in range(N_SLABS):
            src = slice(c0 + pair * LANES, c0 + (pair + 1) * LANES)
            dst = slice(2 * pair * LANES, (2 * pair + 1) * LANES)
            vbuf[res, 0:BAND, dst] = vp_ref[:, src]
            vbuf[res, BAND:BAND + rows, dst] = vc_ref[:, src]
            vbuf[res, BAND + rows:, dst] = vn_ref[:, src]
            vbuf[res, :, (2 * pair + 1) * LANES:(2 * pair + 2) * LANES] = ones

    first_row = (pl.program_id(0) % (sub_len // rows)) * rows
    delta = (lax.broadcasted_iota(jnp.int32, (QB, KW), 1)
             - lax.broadcasted_iota(jnp.int32, (QB, KW), 0))
    band_bias = jnp.where((delta >= 0) & (delta <= 2 * BAND), 0.0, NEG_INF)
    key_col = lax.broadcasted_iota(jnp.int32, (1, KW), 1)
    lane = lax.broadcasted_iota(jnp.int32, (1, LANES), 1)
    low_head = lane < HEAD_DIM

    for j in range(rows // QB):
        row0 = j * QB
        key_pos = key_col + (first_row + row0 - BAND)
        bias = band_bias + jnp.where((key_pos >= 0) & (key_pos < sub_len), 0.0, NEG_INF)
        bias = jnp.concatenate([bias, bias], axis=0)
        for res in range(n_res):
            lse_tile = jnp.zeros((QB, LANES), F32)
            for pair in range(N_SLABS):
                cols = slice(res * D_ATTN + pair * LANES, res * D_ATTN + (pair + 1) * LANES)
                q2 = q_ref[row0:row0 + QB, cols]
                zero = jnp.zeros_like(q2)
                qs = jnp.concatenate([jnp.where(low_head, q2, zero), jnp.where(low_head, zero, q2)],
                                     axis=0)
                k2 = kbuf[res, row0:row0 + KW, pair * LANES:(pair + 1) * LANES]
                s = lax.dot_general(qs, k2, (((1,), (1,)), ((), ())),
                                    preferred_element_type=F32) + bias
                m = jnp.max(s, axis=-1, keepdims=True)
                p = jnp.exp2(s - m).astype(BF16)
                pv = jnp.dot(p, vbuf[res, row0:row0 + KW, 2 * pair * LANES:(2 * pair + 2) * LANES],
                             preferred_element_type=F32)
                num = jnp.where(low_head, pv[:QB, :LANES], pv[QB:, :LANES])
                den = jnp.where(low_head, pv[:QB, LANES:], pv[QB:, LANES:])
                o_ref[row0:row0 + QB, cols] = (num / den).astype(BF16)
                lse_pair = jnp.where(low_head, m[:QB], m[QB:]) * LN_2 + jnp.log(den)
                lse_tile = jnp.where((lane == pair) | (lane == HEAD_DIM + pair), lse_pair, lse_tile)
            lse_ref[row0:row0 + QB, res * LANES:(res + 1) * LANES] = lse_tile


def _attention(q, k, v, dil, seq, cast_jobs=()):
    total_rows = q.shape[0]
    sub_len = seq // dil
    rows = min(MAX_ATT_ROWS, sub_len)
    n_res = min(MAX_ATT_ROWS // rows, dil)
    assert sub_len % rows == 0 and rows % QB == 0 and dil % n_res == 0
    halos_per_step = rows // BAND
    last_halo = total_rows // BAND - 1
    width = n_res * D_ATTN
    grid = (total_rows // rows, dil // n_res)
    cur = pl.BlockSpec((rows, width), lambda i, r: (i, r))
    prev = pl.BlockSpec((BAND, width), lambda i, r: (jnp.maximum(i * halos_per_step - 1, 0), r))
    nxt = pl.BlockSpec((BAND, width),
                       lambda i, r: (jnp.minimum((i + 1) * halos_per_step, last_halo), r))
    n_steps = grid[0] * grid[1]
    cast_in, cast_out, cast_shapes = [], [], []
    for w, w_layer in cast_jobs:
        _, w_rows, w_cols = w.shape
        chunk = w_rows // n_steps
        assert chunk * n_steps == w_rows and chunk % (2 * SUBLANES) == 0
        cast_in.append(pl.BlockSpec((None, chunk, w_cols),
                                    lambda i, r, w_layer=w_layer: (w_layer, i * grid[1] + r, 0)))
        cast_out.append(pl.BlockSpec((chunk, w_cols), lambda i, r: (i * grid[1] + r, 0)))
        cast_shapes.append(jax.ShapeDtypeStruct((w_rows, w_cols), BF16))
    outs = pl.pallas_call(
        functools.partial(_attn_kernel, sub_len=sub_len, rows=rows, n_res=n_res,
                          n_cast=len(cast_jobs)),
        grid=grid,
        in_specs=[cur, cur, prev, nxt, cur, prev, nxt] + cast_in,
        out_specs=[cur, pl.BlockSpec((rows, n_res * LANES), lambda i, r: (i, r))] + cast_out,
        out_shape=[jax.ShapeDtypeStruct(q.shape, BF16),
                   jax.ShapeDtypeStruct((total_rows, dil * LANES), F32)] + cast_shapes,
        scratch_shapes=[pltpu.VMEM((n_res, rows + 2 * BAND, D_ATTN), BF16),
                        pltpu.VMEM((n_res, rows + 2 * BAND, 2 * D_ATTN), BF16)],
        compiler_params=_params(2),
        name=f"attn_d{dil}",
    )(q, k, k, k, v, v, v, *[w for w, _ in cast_jobs])
    return outs[0], outs[1], list(outs[2:])


def _merge_work(o_refs, l_refs, ug_ref, sgu_refs, ga_ref, gs_ref, norm_ref, stage_refs):
    def unview_slab(s):
        def work():
            produced = []
            for p, dil in enumerate(DILATIONS[1:]):
                for r in range(dil):
                    blk = o_refs[p + 1][:, r * D_ATTN + s * LANES:r * D_ATTN + (s + 1) * LANES]
                    blk = blk.astype(F32)
                    stage_refs[2 * p][s, pl.ds(r, TM // dil, stride=dil), :] = blk
                    produced.append(blk)
            return produced
        return work

    def unview_lse():
        produced = []
        for p, dil in enumerate(DILATIONS[1:]):
            for r in range(dil):
                blk = l_refs[p + 1][:, r * LANES:(r + 1) * LANES]
                stage_refs[2 * p + 1][pl.ds(r, TM // dil, stride=dil), :] = blk
                produced.append(blk)
        return produced

    def merge_rows(c):
        def work():
            rows = slice(c * CHUNK, (c + 1) * CHUNK)
            low_head = lax.broadcasted_iota(jnp.int32, (1, LANES), 1) < HEAD_DIM
            lses = [l_refs[0][rows, :]] + [stage_refs[2 * p + 1][rows, :] for p in range(N_PAT - 1)]
            m = functools.reduce(jnp.maximum, lses)
            es = [jnp.exp(l - m) for l in lses]
            inv = 1.0 / sum(es)
            slabs = []
            for s in range(N_SLABS):
                cols = slice(s * LANES, (s + 1) * LANES)
                outs = [o_refs[0][rows, cols].astype(F32)]
                outs += [stage_refs[2 * p][s, rows, :] for p in range(N_PAT - 1)]
                acc = None
                for e, o in zip(es, outs):
                    w = e * inv
                    w = jnp.where(low_head, w[:, s:s + 1], w[:, HEAD_DIM + s:HEAD_DIM + s + 1])
                    acc = w * o if acc is None else acc + w * o
                slabs.append(acc)
            na = _rms(jnp.concatenate(slabs, axis=1), ga_ref[...])
            gated = _spatial_gating(ug_ref[rows, 0:D_SGU].astype(F32), ug_ref[rows, D_SGU:].astype(F32),
                                    *sgu_refs)
            ns = _rms(jnp.concatenate(gated, axis=1), gs_ref[...])
            norm_ref[rows, 0:D_ATTN] = na.astype(BF16)
            norm_ref[rows, D_ATTN:] = ns.astype(BF16)
            return [na, ns]
        return work

    return ([unview_slab(s) for s in range(N_SLABS)] + [unview_lse]
            + [merge_rows(c) for c in range(TM // CHUNK)])


def _merge_ffn_kernel(x_ref, *refs, n_tiles, final):
    o_refs, l_refs = refs[0:2 * N_PAT:2], refs[1:2 * N_PAT:2]
    ug_ref, sgu_refs = refs[2 * N_PAT], refs[2 * N_PAT + 1:2 * N_PAT + 5]
    ga_ref, gs_ref, wo_ref = refs[2 * N_PAT + 5:2 * N_PAT + 8]
    ffn_refs = refs[2 * N_PAT + 8:2 * N_PAT + 12]
    fgain_ref, out_ref, act_ref, norm_ref = refs[2 * N_PAT + 12:2 * N_PAT + 16]
    stage_refs = refs[2 * N_PAT + 16:]
    step = pl.program_id(0)
    merge = _merge_work(o_refs, l_refs, ug_ref, sgu_refs, ga_ref, gs_ref, norm_ref, stage_refs)

    def project_and_ffn(side_work):
        x = x_ref[...] + jnp.dot(norm_ref[...], wo_ref[...], preferred_element_type=F32)
        y = _ffn_apply(x, *ffn_refs, act_ref, side_work=side_work)
        out_ref[...] = _rms(y, fgain_ref[...]) if final else y

    @pl.when(step == 0)
    def _():
        for work in merge:
            work()

    @pl.when((step > 0) & (step < n_tiles))
    def _():
        project_and_ffn(merge)

    @pl.when(step == n_tiles)
    def _():
        project_and_ffn(())


def _merge_ffn(x, attn, ug, sgu, ga, gs, wo, ffn_w, fgain, layer, final):
    tokens = x.shape[0]
    n_tiles = tokens // TM
    lead = lambda s: (jnp.minimum(s, n_tiles - 1), 0)
    lag = lambda s: (jnp.maximum(s - 1, 0), 0)
    views = [pl.BlockSpec((TM // d, d * width), lead) for d in DILATIONS for width in (D_ATTN, LANES)]
    return pl.pallas_call(
        functools.partial(_merge_ffn_kernel, n_tiles=n_tiles, final=final),
        grid=(n_tiles + 1,),
        in_specs=[pl.BlockSpec((TM, D_MODEL), lag)] + views + [pl.BlockSpec((TM, 2 * D_SGU), lead)]
                 + [_resident((1, D_SGU), layer), _resident((1, D_SGU), layer),
                    _resident((N_SGU_GROUPS, CHUNK, CHUNK), layer),
                    _resident((N_SGU_GROUPS, CHUNK, 1), layer)]
                 + [_resident((1, D_ATTN), layer), _resident((1, D_SGU), layer),
                    _resident_weight((D_ATTN + D_SGU, D_MODEL))]
                 + _ffn_specs(layer) + [pl.BlockSpec((1, D_MODEL), lambda s: (0, 0))],
        out_specs=pl.BlockSpec((TM, D_MODEL), lag),
        out_shape=jax.ShapeDtypeStruct((tokens, D_MODEL), F32),
        scratch_shapes=[pltpu.VMEM((TM, D_FF), BF16), pltpu.VMEM((TM, D_ATTN + D_SGU), BF16)]
                       + [pltpu.VMEM((N_SLABS, TM, LANES), F32), pltpu.VMEM((TM, LANES), F32)] * (N_PAT - 1),
        compiler_params=_params(1),
        name="merge_ffn",
    )(x, *attn, ug, *sgu, ga, gs, wo, *ffn_w, fgain)


def kernel(x, norm_ffn1, ffn1_w_gate, ffn1_w_up, ffn1_w_down, norm_mix, w_in, sgu_ln_g, sgu_ln_b,
           sgu_w, sgu_b, out_norm_attn, out_norm_sgu, w_out, norm_ffn2, ffn2_w_gate, ffn2_w_up,
           ffn2_w_down, final_norm):
    batch, seq, _ = x.shape
    tokens = batch * seq
    assert x.shape[2] == D_MODEL and tokens % TM == 0 and seq % TM == 0 and TM % CHUNK == 0

    row = lambda p: p[:, None, :]
    n1, n2, nm = row(norm_ffn1), row(norm_ffn2), row(norm_mix)
    lng, lnb = row(sgu_ln_g), row(sgu_ln_b)
    ga, gs = row(out_norm_attn), row(out_norm_sgu)
    sgu_w_b = sgu_w.astype(BF16)
    sgu_b_col = sgu_b[..., None]
    fgain = final_norm[None, :]
    rope = _rope_tables(seq)

    stacked = dict(gate1=ffn1_w_gate, up1=ffn1_w_up, down1=ffn1_w_down, w_in=w_in,
                   gate2=ffn2_w_gate, up2=ffn2_w_up, down2=ffn2_w_down, w_out=w_out)
    early = ("gate1", "up1", "down1", "w_in")
    next_layer_jobs = (("gate1", "up1"), ("gate2", "up2"), ("down1", "down2", "w_in", "w_out"))
    same_layer_jobs = (("gate2",), ("up2",), ("down2", "w_out"))
    bf16 = {(name, 0): stacked[name][0].astype(BF16) for name in early}

    xt = x.reshape(tokens, D_MODEL)
    for layer in range(DEPTH):
        ffn1_w = [bf16[(name, layer)] for name in ("gate1", "up1", "down1")]
        xt, qs, ks, vs, ug = _ffn_inproj(xt, (n1, *ffn1_w), nm, bf16[("w_in", layer)], rope, layer, seq)
        attn = []
        for p, (q, k, v, dil) in enumerate(zip(qs, ks, vs, DILATIONS)):
            jobs = [(name, layer + 1) for name in next_layer_jobs[p]] if layer + 1 < DEPTH else []
            jobs += [(name, layer) for name in same_layer_jobs[p] if (name, layer) not in bf16]
            o, lse, cast = _attention(q, k, v, dil, seq, [(stacked[n], l) for n, l in jobs])
            attn += [o, lse]
            bf16.update(zip(jobs, cast))
        ffn2_w = [bf16[(name, layer)] for name in ("gate2", "up2", "down2")]
        xt = _merge_ffn(xt, attn, ug, (lng, lnb, sgu_w_b, sgu_b_col), ga, gs, bf16[("w_out", layer)],
                        (n2, *ffn2_w), fgain, layer, layer == DEPTH - 1)
    return xt.reshape(batch, seq, D_MODEL)
```

```python
import functools
import math

import numpy as np
import jax
import jax.numpy as jnp
from jax import lax
from jax.experimental import pallas as pl
from jax.experimental.pallas import tpu as pltpu

F32 = jnp.float32
BF16 = jnp.bfloat16

D_MODEL = 1024
DEPTH = 4
HEAD_DIM = 64
D_ATTN = 512
D_SGU = 512
N_SGU_GROUPS = 4
CHUNK = 128
D_IN = 3 * D_ATTN + 2 * D_SGU
D_FF = 2816
DILATIONS = (1, 4, 16)
N_PAT = len(DILATIONS)
BAND = 64
ROPE_THETA = 500000.0
ROPE_DIM = HEAD_DIM // 4
EPS = 1e-6
NEG_INF = -1e30
LOG2_E = math.log2(math.e)
LN_2 = math.log(2.0)

LANES = 128
SUBLANES = 8
V7X_VMEM_BYTES = 64 * 1024 * 1024
VMEM_LIMIT = V7X_VMEM_BYTES * 7 // 8

TM = 512
FF_CHUNK = 256
N_FF_CHUNKS = D_FF // FF_CHUNK
N_SLABS = D_ATTN // LANES
MAX_ATT_ROWS = 1024
QB = 2 * BAND
KW = QB + 2 * BAND


def _rms(x, gain):
    return x * lax.rsqrt(jnp.mean(x * x, axis=-1, keepdims=True) + EPS) * gain


def _params(n_axes):
    return pltpu.CompilerParams(dimension_semantics=("arbitrary",) * n_axes,
                                vmem_limit_bytes=VMEM_LIMIT)


def _resident(shape, layer):
    nd = len(shape)
    return pl.BlockSpec((None,) + shape, lambda *_: (layer,) + (0,) * nd,
                        pipeline_mode=pl.Buffered(1))


def _resident_weight(shape):
    return pl.BlockSpec(shape, lambda *_: (0,) * len(shape), pipeline_mode=pl.Buffered(1))


def _view_shape(tokens, dil, dtype):
    return jax.ShapeDtypeStruct((tokens // dil, dil * D_ATTN), dtype)


def _ordering_zero(values):
    acc = None
    for v in values:
        bits = pltpu.bitcast(v[0:SUBLANES, 0:LANES], jnp.uint32)
        acc = bits if acc is None else acc | bits
    return ((acc >> 16) >> 16)[0:1, :].astype(F32)


def _ffn_apply(x, gain_ref, wg_ref, wu_ref, wd_ref, act_ref, side_work=()):
    h = _rms(x, gain_ref[...]).astype(BF16)
    for c in range(N_FF_CHUNKS):
        cols = slice(c * FF_CHUNK, (c + 1) * FF_CHUNK)
        g = jnp.dot(h, wg_ref[:, cols], preferred_element_type=F32)
        u = jnp.dot(h, wu_ref[:, cols], preferred_element_type=F32)
        a = g * jax.nn.sigmoid(g) * u
        todo = side_work[c * len(side_work) // N_FF_CHUNKS:(c + 1) * len(side_work) // N_FF_CHUNKS]
        if todo:
            zero = _ordering_zero([v for work in todo for v in work()])
            a = a + jnp.concatenate([zero] * (FF_CHUNK // LANES), axis=1)
        act_ref[:, cols] = a.astype(BF16)
    return x + 0.5 * jnp.dot(act_ref[...], wd_ref[...], preferred_element_type=F32)


def _ffn_specs(layer):
    return [_resident((1, D_MODEL), layer), _resident_weight((D_MODEL, D_FF)),
            _resident_weight((D_MODEL, D_FF)), _resident_weight((D_FF, D_MODEL))]


def _gelu(x):
    return 0.5 * x * (1.0 + lax.erf(x * (1.0 / math.sqrt(2.0))))


def _inproj_finish_work(raw, cos_ref, sina_ref, sinb_ref, lng_ref, lnb_ref, sw_ref, sb_ref,
                        q_refs, k_refs, v_refs, sg_ref, stage_ref):
    def qkv_slab(col0, dst_refs, rotary, scale, s):
        def work():
            t = raw[col0 // D_ATTN][:, s * LANES:(s + 1) * LANES]
            if rotary:
                half = ROPE_DIM // 2
                t = (t * cos_ref[...] + pltpu.roll(t, half, 1) * sina_ref[...]
                     + pltpu.roll(t, LANES - half, 1) * sinb_ref[...])
            if scale != 1.0:
                t = t * scale
            dst_refs[0][:, s * LANES:(s + 1) * LANES] = t.astype(BF16)
            stage_ref[0, s] = t
            produced = [t]
            for k in range(1, N_PAT):
                d_prev, dil = DILATIONS[k - 1], DILATIONS[k]
                ratio, n_prev, n = dil // d_prev, TM // d_prev, TM // dil
                for e in range(d_prev):
                    for c in range(ratio):
                        r = c * d_prev + e
                        blk = stage_ref[k - 1, s, pl.ds(e * n_prev + c, n, stride=ratio), :]
                        dst_refs[k][:, r * D_ATTN + s * LANES:r * D_ATTN + (s + 1) * LANES] = (
                            blk.astype(BF16))
                        if k + 1 < N_PAT:
                            stage_ref[k, s, r * n:(r + 1) * n, :] = blk
                        produced.append(blk)
            return produced
        return work

    def sgu_chunk(c):
        def work():
            rows = slice(c * CHUNK, (c + 1) * CHUNK)
            u = _gelu(raw[3][rows, :])
            g = _gelu(raw[4][rows, :])
            gc = g - jnp.mean(g, axis=-1, keepdims=True)
            gn = (gc * lax.rsqrt(jnp.mean(gc * gc, axis=-1, keepdims=True) + EPS) * lng_ref[...]
                  + lnb_ref[...]).astype(BF16)
            group = D_SGU // N_SGU_GROUPS
            produced = []
            for gi in range(N_SGU_GROUPS):
                cols = slice(gi * group, (gi + 1) * group)
                mixed = jnp.dot(sw_ref[gi], gn[:, cols], preferred_element_type=F32) + sb_ref[gi]
                gated = u[:, cols] * mixed
                sg_ref[rows, cols] = gated.astype(BF16)
                produced.append(gated)
            return produced
        return work

    q_scale = LOG2_E / math.sqrt(HEAD_DIM)
    pieces = [qkv_slab(0, q_refs, True, q_scale, s) for s in range(N_SLABS)]
    pieces += [qkv_slab(D_ATTN, k_refs, True, 1.0, s) for s in range(N_SLABS)]
    pieces += [qkv_slab(2 * D_ATTN, v_refs, False, 1.0, s) for s in range(N_SLABS)]
    pieces += [sgu_chunk(c) for c in range(TM // CHUNK)]
    return pieces


def _ffn_inproj_kernel(x_ref, *refs):
    ffn_refs, gain_ref, w_ref, finish_refs = refs[:4], refs[4], refs[5], refs[6:13]
    xo_ref = refs[13]
    q_refs, k_refs, v_refs = (refs[14 + i * N_PAT:14 + (i + 1) * N_PAT] for i in range(3))
    sg_ref, act_ref, stage_ref = refs[14 + 3 * N_PAT:]
    x = _ffn_apply(x_ref[...], *ffn_refs, act_ref)
    xo_ref[...] = x
    h = _rms(x, gain_ref[...]).astype(BF16)
    raw = [jnp.dot(h, w_ref[:, c0:c0 + D_ATTN], preferred_element_type=F32)
           for c0 in range(0, D_IN, D_ATTN)]
    for work in _inproj_finish_work(raw, *finish_refs, q_refs, k_refs, v_refs, sg_ref, stage_ref):
        work()


def _ffn_inproj(x, ffn_w, gain, w_in, rope, lng, lnb, sw, sb, layer, seq):
    tokens = x.shape[0]
    rows = lambda s: (s, 0)
    table = pl.BlockSpec((TM, LANES), lambda s: (s % (seq // TM), 0))
    views = [pl.BlockSpec((TM // d, d * D_ATTN), rows) for d in DILATIONS]
    view_shapes = [_view_shape(tokens, d, BF16) for d in DILATIONS]
    outs = pl.pallas_call(
        _ffn_inproj_kernel,
        grid=(tokens // TM,),
        in_specs=[pl.BlockSpec((TM, D_MODEL), rows)] + _ffn_specs(layer)
                 + [_resident((1, D_MODEL), layer), _resident_weight((D_MODEL, D_IN)),
                    table, table, table,
                    _resident((1, D_SGU), layer), _resident((1, D_SGU), layer),
                    _resident((N_SGU_GROUPS, CHUNK, CHUNK), layer),
                    _resident((N_SGU_GROUPS, CHUNK, 1), layer)],
        out_specs=[pl.BlockSpec((TM, D_MODEL), rows)] + views * 3 + [pl.BlockSpec((TM, D_SGU), rows)],
        out_shape=[jax.ShapeDtypeStruct((tokens, D_MODEL), F32)] + view_shapes * 3
                  + [jax.ShapeDtypeStruct((tokens, D_SGU), BF16)],
        scratch_shapes=[pltpu.VMEM((TM, D_FF), BF16), pltpu.VMEM((N_PAT - 1, N_SLABS, TM, LANES), F32)],
        compiler_params=_params(1),
        name="ffn_inproj",
    )(x, *ffn_w, gain, w_in, *rope, lng, lnb, sw, sb)
    return (outs[0], outs[1:1 + N_PAT], outs[1 + N_PAT:1 + 2 * N_PAT],
            outs[1 + 2 * N_PAT:1 + 3 * N_PAT], outs[-1])


def _rope_tables(seq):
    half = ROPE_DIM // 2
    inv_freq = ROPE_THETA ** (-np.arange(0, ROPE_DIM, 2, dtype=np.float64) / ROPE_DIM)
    ang = np.arange(seq, dtype=np.float64)[:, None] * inv_freq[None, :]
    cos = np.ones((seq, LANES))
    sina = np.zeros((seq, LANES))
    sinb = np.zeros((seq, LANES))
    for head0 in range(0, LANES, HEAD_DIM):
        cos[:, head0:head0 + half] = np.cos(ang)
        cos[:, head0 + half:head0 + 2 * half] = np.cos(ang)
        sinb[:, head0:head0 + half] = -np.sin(ang)
        sina[:, head0 + half:head0 + 2 * half] = np.sin(ang)
    return tuple(jnp.asarray(t, F32) for t in (cos, sina, sinb))


def _attn_kernel(q_ref, kc_ref, kp_ref, kn_ref, vc_ref, vp_ref, vn_ref, *refs,
                 sub_len, rows, n_res, n_cast):
    cast_src, (o_ref, lse_ref) = refs[:n_cast], refs[n_cast:n_cast + 2]
    cast_dst, (kbuf, vbuf) = refs[n_cast + 2:2 * n_cast + 2], refs[2 * n_cast + 2:]
    for src, dst in zip(cast_src, cast_dst):
        dst[...] = src[...].astype(BF16)

    ones = jnp.ones((rows + 2 * BAND, LANES), BF16)
    for res in range(n_res):
        c0 = res * D_ATTN
        kbuf[res, 0:BAND] = kp_ref[:, c0:c0 + D_ATTN]
        kbuf[res, BAND:BAND + rows] = kc_ref[:, c0:c0 + D_ATTN]
        kbuf[res, BAND + rows:] = kn_ref[:, c0:c0 + D_ATTN]
        for pair in range(N_SLABS):
            src = slice(c0 + pair * LANES, c0 + (pair + 1) * LANES)
            dst = slice(2 * pair * LANES, (2 * pair + 1) * LANES)
            vbuf[res, 0:BAND, dst] = vp_ref[:, src]
            vbuf[res, BAND:BAND + rows, dst] = vc_ref[:, src]
            vbuf[res, BAND + rows:, dst] = vn_ref[:, src]
            vbuf[res, :, (2 * pair + 1) * LANES:(2 * pair + 2) * LANES] = ones

    first_row = (pl.program_id(0) % (sub_len // rows)) * rows
    delta = (lax.broadcasted_iota(jnp.int32, (QB, KW), 1)
             - lax.broadcasted_iota(jnp.int32, (QB, KW), 0))
    band_bias = jnp.where((delta >= 0) & (delta <= 2 * BAND), 0.0, NEG_INF)
    key_col = lax.broadcasted_iota(jnp.int32, (1, KW), 1)
    lane = lax.broadcasted_iota(jnp.int32, (1, LANES), 1)
    low_head = lane < HEAD_DIM

    for j in range(rows // QB):
        row0 = j * QB
        key_pos = key_col + (first_row + row0 - BAND)
        bias = band_bias + jnp.where((key_pos >= 0) & (key_pos < sub_len), 0.0, NEG_INF)
        bias = jnp.concatenate([bias, bias], axis=0)
        for res in range(n_res):
            lse_tile = jnp.zeros((QB, LANES), F32)
            for pair in range(N_SLABS):
                cols = slice(res * D_ATTN + pair * LANES, res * D_ATTN + (pair + 1) * LANES)
                q2 = q_ref[row0:row0 + QB, cols]
                zero = jnp.zeros_like(q2)
                qs = jnp.concatenate([jnp.where(low_head, q2, zero), jnp.where(low_head, zero, q2)],
                                     axis=0)
                k2 = kbuf[res, row0:row0 + KW, pair * LANES:(pair + 1) * LANES]
                s = lax.dot_general(qs, k2, (((1,), (1,)), ((), ())),
                                    preferred_element_type=F32) + bias
                m = jnp.max(s, axis=-1, keepdims=True)
                p = jnp.exp2(s - m).astype(BF16)
                pv = jnp.dot(p, vbuf[res, row0:row0 + KW, 2 * pair * LANES:(2 * pair + 2) * LANES],
                             preferred_element_type=F32)
                num = jnp.where(low_head, pv[:QB, :LANES], pv[QB:, :LANES])
                den = jnp.where(low_head, pv[:QB, LANES:], pv[QB:, LANES:])
                o_ref[row0:row0 + QB, cols] = (num / den).astype(BF16)
                lse_pair = jnp.where(low_head, m[:QB], m[QB:]) * LN_2 + jnp.log(den)
                lse_tile = jnp.where((lane == pair) | (lane == HEAD_DIM + pair), lse_pair, lse_tile)
            lse_ref[row0:row0 + QB, res * LANES:(res + 1) * LANES] = lse_tile


def _attention(q, k, v, dil, seq, cast_jobs=()):
    total_rows = q.shape[0]
    sub_len = seq // dil
    rows = min(MAX_ATT_ROWS, sub_len)
    n_res = min(MAX_ATT_ROWS // rows, dil)
    assert sub_len % rows == 0 and rows % QB == 0 and dil % n_res == 0
    halos_per_step = rows // BAND
    last_halo = total_rows // BAND - 1
    width = n_res * D_ATTN
    grid = (total_rows // rows, dil // n_res)
    cur = pl.BlockSpec((rows, width), lambda i, r: (i, r))
    prev = pl.BlockSpec((BAND, width), lambda i, r: (jnp.maximum(i * halos_per_step - 1, 0), r))
    nxt = pl.BlockSpec((BAND, width),
                       lambda i, r: (jnp.minimum((i + 1) * halos_per_step, last_halo), r))
    n_steps = grid[0] * grid[1]
    cast_in, cast_out, cast_shapes = [], [], []
    for w, w_layer in cast_jobs:
        _, w_rows, w_cols = w.shape
        chunk = w_rows // n_steps
        assert chunk * n_steps == w_rows and chunk % (2 * SUBLANES) == 0
        cast_in.append(pl.BlockSpec((None, chunk, w_cols),
                                    lambda i, r, w_layer=w_layer: (w_layer, i * grid[1] + r, 0)))
        cast_out.append(pl.BlockSpec((chunk, w_cols), lambda i, r: (i * grid[1] + r, 0)))
        cast_shapes.append(jax.ShapeDtypeStruct((w_rows, w_cols), BF16))
    outs = pl.pallas_call(
        functools.partial(_attn_kernel, sub_len=sub_len, rows=rows, n_res=n_res,
                          n_cast=len(cast_jobs)),
        grid=grid,
        in_specs=[cur, cur, prev, nxt, cur, prev, nxt] + cast_in,
        out_specs=[cur, pl.BlockSpec((rows, n_res * LANES), lambda i, r: (i, r))] + cast_out,
        out_shape=[jax.ShapeDtypeStruct(q.shape, BF16),
                   jax.ShapeDtypeStruct((total_rows, dil * LANES), F32)] + cast_shapes,
        scratch_shapes=[pltpu.VMEM((n_res, rows + 2 * BAND, D_ATTN), BF16),
                        pltpu.VMEM((n_res, rows + 2 * BAND, 2 * D_ATTN), BF16)],
        compiler_params=_params(2),
        name=f"attn_d{dil}",
    )(q, k, k, k, v, v, v, *[w for w, _ in cast_jobs])
    return outs[0], outs[1], list(outs[2:])


def _merge_work(o_refs, l_refs, sg_ref, ga_ref, gs_ref, norm_ref, stage_refs):
    def unview_slab(s):
        def work():
            produced = []
            for p, dil in enumerate(DILATIONS[1:]):
                for r in range(dil):
                    blk = o_refs[p + 1][:, r * D_ATTN + s * LANES:r * D_ATTN + (s + 1) * LANES]
                    blk = blk.astype(F32)
                    stage_refs[2 * p][s, pl.ds(r, TM // dil, stride=dil), :] = blk
                    produced.append(blk)
            return produced
        return work

    def unview_lse():
        produced = []
        for p, dil in enumerate(DILATIONS[1:]):
            for r in range(dil):
                blk = l_refs[p + 1][:, r * LANES:(r + 1) * LANES]
                stage_refs[2 * p + 1][pl.ds(r, TM // dil, stride=dil), :] = blk
                produced.append(blk)
        return produced

    def merge_rows(c):
        def work():
            rows = slice(c * CHUNK, (c + 1) * CHUNK)
            low_head = lax.broadcasted_iota(jnp.int32, (1, LANES), 1) < HEAD_DIM
            lses = [l_refs[0][rows, :]] + [stage_refs[2 * p + 1][rows, :] for p in range(N_PAT - 1)]
            m = functools.reduce(jnp.maximum, lses)
            es = [jnp.exp(l - m) for l in lses]
            inv = 1.0 / sum(es)
            slabs = []
            for s in range(N_SLABS):
                cols = slice(s * LANES, (s + 1) * LANES)
                outs = [o_refs[0][rows, cols].astype(F32)]
                outs += [stage_refs[2 * p][s, rows, :] for p in range(N_PAT - 1)]
                acc = None
                for e, o in zip(es, outs):
                    w = e * inv
                    w = jnp.where(low_head, w[:, s:s + 1], w[:, HEAD_DIM + s:HEAD_DIM + s + 1])
                    acc = w * o if acc is None else acc + w * o
                slabs.append(acc)
            na = _rms(jnp.concatenate(slabs, axis=1), ga_ref[...])
            ns = _rms(sg_ref[rows, :].astype(F32), gs_ref[...])
            norm_ref[rows, 0:D_ATTN] = na.astype(BF16)
            norm_ref[rows, D_ATTN:] = ns.astype(BF16)
            return [na, ns]
        return work

    return ([unview_slab(s) for s in range(N_SLABS)] + [unview_lse]
            + [merge_rows(c) for c in range(TM // CHUNK)])


def _merge_ffn_kernel(x_ref, *refs, n_tiles, final):
    o_refs, l_refs = refs[0:2 * N_PAT:2], refs[1:2 * N_PAT:2]
    sg_ref, ga_ref, gs_ref, wo_ref = refs[2 * N_PAT:2 * N_PAT + 4]
    ffn_refs = refs[2 * N_PAT + 4:2 * N_PAT + 8]
    fgain_ref, out_ref, act_ref, norm_ref = refs[2 * N_PAT + 8:2 * N_PAT + 12]
    stage_refs = refs[2 * N_PAT + 12:]
    step = pl.program_id(0)
    merge = _merge_work(o_refs, l_refs, sg_ref, ga_ref, gs_ref, norm_ref, stage_refs)

    def project_and_ffn(side_work):
        x = x_ref[...] + jnp.dot(norm_ref[...], wo_ref[...], preferred_element_type=F32)
        y = _ffn_apply(x, *ffn_refs, act_ref, side_work=side_work)
        out_ref[...] = _rms(y, fgain_ref[...]) if final else y

    @pl.when(step == 0)
    def _():
        for work in merge:
            work()

    @pl.when((step > 0) & (step < n_tiles))
    def _():
        project_and_ffn(merge)

    @pl.when(step == n_tiles)
    def _():
        project_and_ffn(())


def _merge_ffn(x, attn, sg, ga, gs, wo, ffn_w, fgain, layer, final):
    tokens = x.shape[0]
    n_tiles = tokens // TM
    lead = lambda s: (jnp.minimum(s, n_tiles - 1), 0)
    lag = lambda s: (jnp.maximum(s - 1, 0), 0)
    views = [pl.BlockSpec((TM // d, d * width), lead) for d in DILATIONS for width in (D_ATTN, LANES)]
    return pl.pallas_call(
        functools.partial(_merge_ffn_kernel, n_tiles=n_tiles, final=final),
        grid=(n_tiles + 1,),
        in_specs=[pl.BlockSpec((TM, D_MODEL), lag)] + views + [pl.BlockSpec((TM, D_SGU), lead)]
                 + [_resident((1, D_ATTN), layer), _resident((1, D_SGU), layer),
                    _resident_weight((D_ATTN + D_SGU, D_MODEL))]
                 + _ffn_specs(layer) + [pl.BlockSpec((1, D_MODEL), lambda s: (0, 0))],
        out_specs=pl.BlockSpec((TM, D_MODEL), lag),
        out_shape=jax.ShapeDtypeStruct((tokens, D_MODEL), F32),
        scratch_shapes=[pltpu.VMEM((TM, D_FF), BF16), pltpu.VMEM((TM, D_ATTN + D_SGU), BF16)]
                       + [pltpu.VMEM((N_SLABS, TM, LANES), F32), pltpu.VMEM((TM, LANES), F32)] * (N_PAT - 1),
        compiler_params=_params(1),
        name="merge_ffn",
    )(x, *attn, sg, ga, gs, wo, *ffn_w, fgain)


def kernel(x, norm_ffn1, ffn1_w_gate, ffn1_w_up, ffn1_w_down, norm_mix, w_in, sgu_ln_g, sgu_ln_b,
           sgu_w, sgu_b, out_norm_attn, out_norm_sgu, w_out, norm_ffn2, ffn2_w_gate, ffn2_w_up,
           ffn2_w_down, final_norm):
    batch, seq, _ = x.shape
    tokens = batch * seq
    assert x.shape[2] == D_MODEL and tokens % TM == 0 and seq % TM == 0 and TM % CHUNK == 0

    row = lambda p: p[:, None, :]
    n1, n2, nm = row(norm_ffn1), row(norm_ffn2), row(norm_mix)
    lng, lnb = row(sgu_ln_g), row(sgu_ln_b)
    ga, gs = row(out_norm_attn), row(out_norm_sgu)
    sgu_w_b = sgu_w.astype(BF16)
    sgu_b_col = sgu_b[..., None]
    fgain = final_norm[None, :]
    rope = _rope_tables(seq)

    stacked = dict(gate1=ffn1_w_gate, up1=ffn1_w_up, down1=ffn1_w_down, w_in=w_in,
                   gate2=ffn2_w_gate, up2=ffn2_w_up, down2=ffn2_w_down, w_out=w_out)
    early = ("gate1", "up1", "down1", "w_in")
    next_layer_jobs = (("gate1", "up1"), ("gate2", "up2"), ("down1", "down2", "w_in", "w_out"))
    same_layer_jobs = (("gate2",), ("up2",), ("down2", "w_out"))
    bf16 = {(name, 0): stacked[name][0].astype(BF16) for name in early}

    xt = x.reshape(tokens, D_MODEL)
    for layer in range(DEPTH):
        ffn1_w = [bf16[(name, layer)] for name in ("gate1", "up1", "down1")]
        xt, qs, ks, vs, sg = _ffn_inproj(xt, (n1, *ffn1_w), nm, bf16[("w_in", layer)], rope, lng, lnb,
                                         sgu_w_b, sgu_b_col, layer, seq)
        attn = []
        for p, (q, k, v, dil) in enumerate(zip(qs, ks, vs, DILATIONS)):
            jobs = [(name, layer + 1) for name in next_layer_jobs[p]] if layer + 1 < DEPTH else []
            jobs += [(name, layer) for name in same_layer_jobs[p] if (name, layer) not in bf16]
            o, lse, cast = _attention(q, k, v, dil, seq, [(stacked[n], l) for n, l in jobs])
            attn += [o, lse]
            bf16.update(zip(jobs, cast))
        ffn2_w = [bf16[(name, layer)] for name in ("gate2", "up2", "down2")]
        xt = _merge_ffn(xt, attn, sg, ga, gs, bf16[("w_out", layer)], (n2, *ffn2_w), fgain, layer,
                        layer == DEPTH - 1)
    return xt.reshape(batch, seq, D_MODEL)
```

```python
import functools
import math

import numpy as np
import jax
import jax.numpy as jnp
from jax import lax
from jax.experimental import pallas as pl
from jax.experimental.pallas import tpu as pltpu

F32 = jnp.float32
BF16 = jnp.bfloat16

D_MODEL = 1024
DEPTH = 4
HEAD_DIM = 64
D_ATTN = 512
D_SGU = 512
N_SGU_GROUPS = 4
CHUNK = 128
D_IN = 3 * D_ATTN + 2 * D_SGU
D_FF = 2816
DILATIONS = (1, 4, 16)
N_PAT = len(DILATIONS)
BAND = 64
ROPE_THETA = 500000.0
ROPE_DIM = HEAD_DIM // 4
EPS = 1e-6
NEG_INF = -1e30
LOG2_E = math.log2(math.e)
LN_2 = math.log(2.0)

LANES = 128
SUBLANES = 8
V7X_VMEM_BYTES = 64 * 1024 * 1024
VMEM_LIMIT = V7X_VMEM_BYTES * 7 // 8

TM = 512
FF_CHUNK = 256
N_FF_CHUNKS = D_FF // FF_CHUNK
N_SLABS = D_ATTN // LANES
MAX_ATT_ROWS = 2048
QB = 2 * BAND
KW = QB + 2 * BAND


def _rms(x, gain):
    return x * lax.rsqrt(jnp.mean(x * x, axis=-1, keepdims=True) + EPS) * gain


def _params(n_axes):
    return pltpu.CompilerParams(dimension_semantics=("arbitrary",) * n_axes,
                                vmem_limit_bytes=VMEM_LIMIT)


def _resident(shape, layer):
    nd = len(shape)
    return pl.BlockSpec((None,) + shape, lambda *_: (layer,) + (0,) * nd,
                        pipeline_mode=pl.Buffered(1))


def _resident_weight(shape):
    return pl.BlockSpec(shape, lambda *_: (0,) * len(shape), pipeline_mode=pl.Buffered(1))


def _view_shape(tokens, dil, dtype):
    return jax.ShapeDtypeStruct((tokens // dil, dil * D_ATTN), dtype)


def _ordering_zero(values):
    acc = None
    for v in values:
        bits = pltpu.bitcast(v[0:SUBLANES, 0:LANES], jnp.uint32)
        acc = bits if acc is None else acc | bits
    return ((acc >> 16) >> 16)[0:1, :].astype(F32)


def _ffn_apply(x, gain_ref, wg_ref, wu_ref, wd_ref, act_ref, side_work=()):
    h = _rms(x, gain_ref[...]).astype(BF16)
    for c in range(N_FF_CHUNKS):
        cols = slice(c * FF_CHUNK, (c + 1) * FF_CHUNK)
        g = jnp.dot(h, wg_ref[:, cols], preferred_element_type=F32)
        u = jnp.dot(h, wu_ref[:, cols], preferred_element_type=F32)
        a = (0.5 * g) * (1.0 + jnp.tanh(0.5 * g)) * u
        todo = side_work[c * len(side_work) // N_FF_CHUNKS:(c + 1) * len(side_work) // N_FF_CHUNKS]
        if todo:
            zero = _ordering_zero([v for work in todo for v in work()])
            a = a + jnp.concatenate([zero] * (FF_CHUNK // LANES), axis=1)
        act_ref[:, cols] = a.astype(BF16)
    return x + 0.5 * jnp.dot(act_ref[...], wd_ref[...], preferred_element_type=F32)


def _ffn_specs(layer):
    return [_resident((1, D_MODEL), layer), _resident_weight((D_MODEL, D_FF)),
            _resident_weight((D_MODEL, D_FF)), _resident_weight((D_FF, D_MODEL))]


def _gelu(x):
    return 0.5 * x * (1.0 + lax.erf(x * (1.0 / math.sqrt(2.0))))


def _inproj_finish_work(raw, cos_ref, sina_ref, sinb_ref, lng_ref, lnb_ref, sw_ref, sb_ref,
                        q_refs, k_refs, v_refs, sg_ref, stage_ref):
    def qkv_slab(col0, dst_refs, rotary, scale, s):
        def work():
            t = raw[col0 // D_ATTN][:, s * LANES:(s + 1) * LANES]
            if rotary:
                half = ROPE_DIM // 2
                t = (t * cos_ref[...] + pltpu.roll(t, half, 1) * sina_ref[...]
                     + pltpu.roll(t, LANES - half, 1) * sinb_ref[...])
            if scale != 1.0:
                t = t * scale
            dst_refs[0][:, s * LANES:(s + 1) * LANES] = t.astype(BF16)
            stage_ref[0, s] = t
            produced = [t]
            for k in range(1, N_PAT):
                d_prev, dil = DILATIONS[k - 1], DILATIONS[k]
                ratio, n_prev, n = dil // d_prev, TM // d_prev, TM // dil
                for e in range(d_prev):
                    for c in range(ratio):
                        r = c * d_prev + e
                        blk = stage_ref[k - 1, s, pl.ds(e * n_prev + c, n, stride=ratio), :]
                        dst_refs[k][:, r * D_ATTN + s * LANES:r * D_ATTN + (s + 1) * LANES] = (
                            blk.astype(BF16))
                        if k + 1 < N_PAT:
                            stage_ref[k, s, r * n:(r + 1) * n, :] = blk
                        produced.append(blk)
            return produced
        return work

    def sgu_chunk(c):
        def work():
            rows = slice(c * CHUNK, (c + 1) * CHUNK)
            u = _gelu(raw[3][rows, :])
            g = _gelu(raw[4][rows, :])
            gc = g - jnp.mean(g, axis=-1, keepdims=True)
            gn = (gc * lax.rsqrt(jnp.mean(gc * gc, axis=-1, keepdims=True) + EPS) * lng_ref[...]
                  + lnb_ref[...]).astype(BF16)
            group = D_SGU // N_SGU_GROUPS
            produced = []
            for gi in range(N_SGU_GROUPS):
                cols = slice(gi * group, (gi + 1) * group)
                mixed = jnp.dot(sw_ref[gi], gn[:, cols], preferred_element_type=F32) + sb_ref[gi]
                gated = u[:, cols] * mixed
                sg_ref[rows, cols] = gated.astype(BF16)
                produced.append(gated)
            return produced
        return work

    q_scale = LOG2_E / math.sqrt(HEAD_DIM)
    pieces = [qkv_slab(0, q_refs, True, q_scale, s) for s in range(N_SLABS)]
    pieces += [qkv_slab(D_ATTN, k_refs, True, 1.0, s) for s in range(N_SLABS)]
    pieces += [qkv_slab(2 * D_ATTN, v_refs, False, 1.0, s) for s in range(N_SLABS)]
    pieces += [sgu_chunk(c) for c in range(TM // CHUNK)]
    return pieces


def _ffn_inproj_kernel(x_ref, *refs):
    ffn_refs, gain_ref, w_ref, finish_refs = refs[:4], refs[4], refs[5], refs[6:13]
    xo_ref = refs[13]
    q_refs, k_refs, v_refs = (refs[14 + i * N_PAT:14 + (i + 1) * N_PAT] for i in range(3))
    sg_ref, act_ref, stage_ref = refs[14 + 3 * N_PAT:]
    x = _ffn_apply(x_ref[...], *ffn_refs, act_ref)
    xo_ref[...] = x
    h = _rms(x, gain_ref[...]).astype(BF16)
    raw = [jnp.dot(h, w_ref[:, c0:c0 + D_ATTN], preferred_element_type=F32)
           for c0 in range(0, D_IN, D_ATTN)]
    for work in _inproj_finish_work(raw, *finish_refs, q_refs, k_refs, v_refs, sg_ref, stage_ref):
        work()


def _ffn_inproj(x, ffn_w, gain, w_in, rope, lng, lnb, sw, sb, layer, seq):
    tokens = x.shape[0]
    rows = lambda s: (s, 0)
    table = pl.BlockSpec((TM, LANES), lambda s: (s % (seq // TM), 0))
    views = [pl.BlockSpec((TM // d, d * D_ATTN), rows) for d in DILATIONS]
    view_shapes = [_view_shape(tokens, d, BF16) for d in DILATIONS]
    outs = pl.pallas_call(
        _ffn_inproj_kernel,
        grid=(tokens // TM,),
        in_specs=[pl.BlockSpec((TM, D_MODEL), rows)] + _ffn_specs(layer)
                 + [_resident((1, D_MODEL), layer), _resident_weight((D_MODEL, D_IN)),
                    table, table, table,
                    _resident((1, D_SGU), layer), _resident((1, D_SGU), layer),
                    _resident((N_SGU_GROUPS, CHUNK, CHUNK), layer),
                    _resident((N_SGU_GROUPS, CHUNK, 1), layer)],
        out_specs=[pl.BlockSpec((TM, D_MODEL), rows)] + views * 3 + [pl.BlockSpec((TM, D_SGU), rows)],
        out_shape=[jax.ShapeDtypeStruct((tokens, D_MODEL), F32)] + view_shapes * 3
                  + [jax.ShapeDtypeStruct((tokens, D_SGU), BF16)],
        scratch_shapes=[pltpu.VMEM((TM, D_FF), BF16), pltpu.VMEM((N_PAT - 1, N_SLABS, TM, LANES), F32)],
        compiler_params=_params(1),
        name="ffn_inproj",
    )(x, *ffn_w, gain, w_in, *rope, lng, lnb, sw, sb)
    return (outs[0], outs[1:1 + N_PAT], outs[1 + N_PAT:1 + 2 * N_PAT],
            outs[1 + 2 * N_PAT:1 + 3 * N_PAT], outs[-1])


def _rope_tables(seq):
    half = ROPE_DIM // 2
    inv_freq = ROPE_THETA ** (-np.arange(0, ROPE_DIM, 2, dtype=np.float64) / ROPE_DIM)
    ang = np.arange(seq, dtype=np.float64)[:, None] * inv_freq[None, :]
    cos = np.ones((seq, LANES))
    sina = np.zeros((seq, LANES))
    sinb = np.zeros((seq, LANES))
    for head0 in range(0, LANES, HEAD_DIM):
        cos[:, head0:head0 + half] = np.cos(ang)
        cos[:, head0 + half:head0 + 2 * half] = np.cos(ang)
        sinb[:, head0:head0 + half] = -np.sin(ang)
        sina[:, head0 + half:head0 + 2 * half] = np.sin(ang)
    return tuple(jnp.asarray(t, F32) for t in (cos, sina, sinb))


def _attn_kernel(q_ref, kc_ref, kp_ref, kn_ref, vc_ref, vp_ref, vn_ref, *refs,
                 sub_len, rows, n_res, n_cast):
    cast_src, (o_ref, lse_ref) = refs[:n_cast], refs[n_cast:n_cast + 2]
    cast_dst, (kbuf, vbuf) = refs[n_cast + 2:2 * n_cast + 2], refs[2 * n_cast + 2:]
    for src, dst in zip(cast_src, cast_dst):
        dst[...] = src[...].astype(BF16)

    ones = jnp.ones((rows + 2 * BAND, LANES), BF16)
    for res in range(n_res):
        c0 = res * D_ATTN
        kbuf[res, 0:BAND] = kp_ref[:, c0:c0 + D_ATTN]
        kbuf[res, BAND:BAND + rows] = kc_ref[:, c0:c0 + D_ATTN]
        kbuf[res, BAND + rows:] = kn_ref[:, c0:c0 + D_ATTN]
        for pair in range(N_SLABS):
            src = slice(c0 + pair * LANES, c0 + (pair + 1) * LANES)
            dst = slice(2 * pair * LANES, (2 * pair + 1) * LANES)
            vbuf[res, 0:BAND, dst] = vp_ref[:, src]
            vbuf[res, BAND:BAND + rows, dst] = vc_ref[:, src]
            vbuf[res, BAND + rows:, dst] = vn_ref[:, src]
            vbuf[res, :, (2 * pair + 1) * LANES:(2 * pair + 2) * LANES] = ones

    first_row = (pl.program_id(0) % (sub_len // rows)) * rows
    delta = (lax.broadcasted_iota(jnp.int32, (QB, KW), 1)
             - lax.broadcasted_iota(jnp.int32, (QB, KW), 0))
    band_bias = jnp.where((delta >= 0) & (delta <= 2 * BAND), 0.0, NEG_INF)
    key_col = lax.broadcasted_iota(jnp.int32, (1, KW), 1)
    lane = lax.broadcasted_iota(jnp.int32, (1, LANES), 1)
    low_head = lane < HEAD_DIM

    for j in range(rows // QB):
        row0 = j * QB
        key_pos = key_col + (first_row + row0 - BAND)
        bias = band_bias + jnp.where((key_pos >= 0) & (key_pos < sub_len), 0.0, NEG_INF)
        bias = jnp.concatenate([bias, bias], axis=0)
        for res in range(n_res):
            lse_tile = jnp.zeros((QB, LANES), F32)
            for pair in range(N_SLABS):
                cols = slice(res * D_ATTN + pair * LANES, res * D_ATTN + (pair + 1) * LANES)
                q2 = q_ref[row0:row0 + QB, cols]
                zero = jnp.zeros_like(q2)
                qs = jnp.concatenate([jnp.where(low_head, q2, zero), jnp.where(low_head, zero, q2)],
                                     axis=0)
                k2 = kbuf[res, row0:row0 + KW, pair * LANES:(pair + 1) * LANES]
                s = lax.dot_general(qs, k2, (((1,), (1,)), ((), ())),
                                    preferred_element_type=F32) + bias
                m = jnp.max(s, axis=-1, keepdims=True)
                p = jnp.exp2(s - m).astype(BF16)
                pv = jnp.dot(p, vbuf[res, row0:row0 + KW, 2 * pair * LANES:(2 * pair + 2) * LANES],
                             preferred_element_type=F32)
                num = jnp.where(low_head, pv[:QB, :LANES], pv[QB:, :LANES])
                den = jnp.where(low_head, pv[:QB, LANES:], pv[QB:, LANES:])
                o_ref[row0:row0 + QB, cols] = (num / den).astype(BF16)
                lse_pair = jnp.where(low_head, m[:QB], m[QB:]) * LN_2 + jnp.log(den)
                lse_tile = jnp.where((lane == pair) | (lane == HEAD_DIM + pair), lse_pair, lse_tile)
            lse_ref[row0:row0 + QB, res * LANES:(res + 1) * LANES] = lse_tile


def _attention(q, k, v, dil, seq, cast_jobs=()):
    total_rows = q.shape[0]
    sub_len = seq // dil
    rows = min(MAX_ATT_ROWS, sub_len)
    n_res = min(MAX_ATT_ROWS // rows, dil)
    assert sub_len % rows == 0 and rows % QB == 0 and dil % n_res == 0
    halos_per_step = rows // BAND
    last_halo = total_rows // BAND - 1
    width = n_res * D_ATTN
    grid = (total_rows // rows, dil // n_res)
    cur = pl.BlockSpec((rows, width), lambda i, r: (i, r))
    prev = pl.BlockSpec((BAND, width), lambda i, r: (jnp.maximum(i * halos_per_step - 1, 0), r))
    nxt = pl.BlockSpec((BAND, width),
                       lambda i, r: (jnp.minimum((i + 1) * halos_per_step, last_halo), r))
    n_steps = grid[0] * grid[1]
    cast_in, cast_out, cast_shapes = [], [], []
    for w, w_layer in cast_jobs:
        _, w_rows, w_cols = w.shape
        chunk = w_rows // n_steps
        assert chunk * n_steps == w_rows and chunk % (2 * SUBLANES) == 0
        cast_in.append(pl.BlockSpec((None, chunk, w_cols),
                                    lambda i, r, w_layer=w_layer: (w_layer, i * grid[1] + r, 0)))
        cast_out.append(pl.BlockSpec((chunk, w_cols), lambda i, r: (i * grid[1] + r, 0)))
        cast_shapes.append(jax.ShapeDtypeStruct((w_rows, w_cols), BF16))
    outs = pl.pallas_call(
        functools.partial(_attn_kernel, sub_len=sub_len, rows=rows, n_res=n_res,
                          n_cast=len(cast_jobs)),
        grid=grid,
        in_specs=[cur, cur, prev, nxt, cur, prev, nxt] + cast_in,
        out_specs=[cur, pl.BlockSpec((rows, n_res * LANES), lambda i, r: (i, r))] + cast_out,
        out_shape=[jax.ShapeDtypeStruct(q.shape, BF16),
                   jax.ShapeDtypeStruct((total_rows, dil * LANES), F32)] + cast_shapes,
        scratch_shapes=[pltpu.VMEM((n_res, rows + 2 * BAND, D_ATTN), BF16),
                        pltpu.VMEM((n_res, rows + 2 * BAND, 2 * D_ATTN), BF16)],
        compiler_params=_params(2),
        name=f"attn_d{dil}",
    )(q, k, k, k, v, v, v, *[w for w, _ in cast_jobs])
    return outs[0], outs[1], list(outs[2:])


def _merge_work(o_refs, l_refs, sg_ref, ga_ref, gs_ref, norm_ref, stage_refs):
    def unview_slab(s):
        def work():
            produced = []
            for p, dil in enumerate(DILATIONS[1:]):
                for r in range(dil):
                    blk = o_refs[p + 1][:, r * D_ATTN + s * LANES:r * D_ATTN + (s + 1) * LANES]
                    blk = blk.astype(F32)
                    stage_refs[2 * p][s, pl.ds(r, TM // dil, stride=dil), :] = blk
                    produced.append(blk)
            return produced
        return work

    def unview_lse():
        produced = []
        for p, dil in enumerate(DILATIONS[1:]):
            for r in range(dil):
                blk = l_refs[p + 1][:, r * LANES:(r + 1) * LANES]
                stage_refs[2 * p + 1][pl.ds(r, TM // dil, stride=dil), :] = blk
                produced.append(blk)
        return produced

    def merge_rows(c):
        def work():
            rows = slice(c * CHUNK, (c + 1) * CHUNK)
            low_head = lax.broadcasted_iota(jnp.int32, (1, LANES), 1) < HEAD_DIM
            lses = [l_refs[0][rows, :]] + [stage_refs[2 * p + 1][rows, :] for p in range(N_PAT - 1)]
            m = functools.reduce(jnp.maximum, lses)
            es = [jnp.exp(l - m) for l in lses]
            inv = 1.0 / sum(es)
            slabs = []
            for s in range(N_SLABS):
                cols = slice(s * LANES, (s + 1) * LANES)
                outs = [o_refs[0][rows, cols].astype(F32)]
                outs += [stage_refs[2 * p][s, rows, :] for p in range(N_PAT - 1)]
                acc = None
                for e, o in zip(es, outs):
                    w = e * inv
                    w = jnp.where(low_head, w[:, s:s + 1], w[:, HEAD_DIM + s:HEAD_DIM + s + 1])
                    acc = w * o if acc is None else acc + w * o
                slabs.append(acc)
            na = _rms(jnp.concatenate(slabs, axis=1), ga_ref[...])
            ns = _rms(sg_ref[rows, :].astype(F32), gs_ref[...])
            norm_ref[rows, 0:D_ATTN] = na.astype(BF16)
            norm_ref[rows, D_ATTN:] = ns.astype(BF16)
            return [na, ns]
        return work

    return ([unview_slab(s) for s in range(N_SLABS)] + [unview_lse]
            + [merge_rows(c) for c in range(TM // CHUNK)])


def _merge_ffn_kernel(x_ref, *refs, n_tiles, final):
    o_refs, l_refs = refs[0:2 * N_PAT:2], refs[1:2 * N_PAT:2]
    sg_ref, ga_ref, gs_ref, wo_ref = refs[2 * N_PAT:2 * N_PAT + 4]
    ffn_refs = refs[2 * N_PAT + 4:2 * N_PAT + 8]
    fgain_ref, out_ref, act_ref, norm_ref = refs[2 * N_PAT + 8:2 * N_PAT + 12]
    stage_refs = refs[2 * N_PAT + 12:]
    step = pl.program_id(0)
    merge = _merge_work(o_refs, l_refs, sg_ref, ga_ref, gs_ref, norm_ref, stage_refs)

    def project_and_ffn(side_work):
        x = x_ref[...] + jnp.dot(norm_ref[...], wo_ref[...], preferred_element_type=F32)
        y = _ffn_apply(x, *ffn_refs, act_ref, side_work=side_work)
        out_ref[...] = _rms(y, fgain_ref[...]) if final else y

    @pl.when(step == 0)
    def _():
        for work in merge:
            work()

    @pl.when((step > 0) & (step < n_tiles))
    def _():
        project_and_ffn(merge)

    @pl.when(step == n_tiles)
    def _():
        project_and_ffn(())


def _merge_ffn(x, attn, sg, ga, gs, wo, ffn_w, fgain, layer, final):
    tokens = x.shape[0]
    n_tiles = tokens // TM
    lead = lambda s: (jnp.minimum(s, n_tiles - 1), 0)
    lag = lambda s: (jnp.maximum(s - 1, 0), 0)
    views = [pl.BlockSpec((TM // d, d * width), lead) for d in DILATIONS for width in (D_ATTN, LANES)]
    return pl.pallas_call(
        functools.partial(_merge_ffn_kernel, n_tiles=n_tiles, final=final),
        grid=(n_tiles + 1,),
        in_specs=[pl.BlockSpec((TM, D_MODEL), lag)] + views + [pl.BlockSpec((TM, D_SGU), lead)]
                 + [_resident((1, D_ATTN), layer), _resident((1, D_SGU), layer),
                    _resident_weight((D_ATTN + D_SGU, D_MODEL))]
                 + _ffn_specs(layer) + [pl.BlockSpec((1, D_MODEL), lambda s: (0, 0))],
        out_specs=pl.BlockSpec((TM, D_MODEL), lag),
        out_shape=jax.ShapeDtypeStruct((tokens, D_MODEL), F32),
        scratch_shapes=[pltpu.VMEM((TM, D_FF), BF16), pltpu.VMEM((TM, D_ATTN + D_SGU), BF16)]
                       + [pltpu.VMEM((N_SLABS, TM, LANES), F32), pltpu.VMEM((TM, LANES), F32)] * (N_PAT - 1),
        compiler_params=_params(1),
        name="merge_ffn",
    )(x, *attn, sg, ga, gs, wo, *ffn_w, fgain)


def kernel(x, norm_ffn1, ffn1_w_gate, ffn1_w_up, ffn1_w_down, norm_mix, w_in, sgu_ln_g, sgu_ln_b,
           sgu_w, sgu_b, out_norm_attn, out_norm_sgu, w_out, norm_ffn2, ffn2_w_gate, ffn2_w_up,
           ffn2_w_down, final_norm):
    batch, seq, _ = x.shape
    tokens = batch * seq
    assert x.shape[2] == D_MODEL and tokens % TM == 0 and seq % TM == 0 and TM % CHUNK == 0

    row = lambda p: p[:, None, :]
    n1, n2, nm = row(norm_ffn1), row(norm_ffn2), row(norm_mix)
    lng, lnb = row(sgu_ln_g), row(sgu_ln_b)
    ga, gs = row(out_norm_attn), row(out_norm_sgu)
    sgu_w_b = sgu_w.astype(BF16)
    sgu_b_col = sgu_b[..., None]
    fgain = final_norm[None, :]
    rope = _rope_tables(seq)

    stacked = dict(gate1=ffn1_w_gate, up1=ffn1_w_up, down1=ffn1_w_down, w_in=w_in,
                   gate2=ffn2_w_gate, up2=ffn2_w_up, down2=ffn2_w_down, w_out=w_out)
    early = ("gate1", "up1", "down1", "w_in")
    next_layer_jobs = (("gate1", "up1"), ("gate2", "up2"), ("down1", "down2", "w_in", "w_out"))
    same_layer_jobs = (("gate2",), ("up2",), ("down2", "w_out"))
    bf16 = {(name, 0): stacked[name][0].astype(BF16) for name in early}

    xt = x.reshape(tokens, D_MODEL)
    for layer in range(DEPTH):
        ffn1_w = [bf16[(name, layer)] for name in ("gate1", "up1", "down1")]
        xt, qs, ks, vs, sg = _ffn_inproj(xt, (n1, *ffn1_w), nm, bf16[("w_in", layer)], rope, lng, lnb,
                                         sgu_w_b, sgu_b_col, layer, seq)
        attn = []
        for p, (q, k, v, dil) in enumerate(zip(qs, ks, vs, DILATIONS)):
            jobs = [(name, layer + 1) for name in next_layer_jobs[p]] if layer + 1 < DEPTH else []
            jobs += [(name, layer) for name in same_layer_jobs[p] if (name, layer) not in bf16]
            o, lse, cast = _attention(q, k, v, dil, seq, [(stacked[n], l) for n, l in jobs])
            attn += [o, lse]
            bf16.update(zip(jobs, cast))
        ffn2_w = [bf16[(name, layer)] for name in ("gate2", "up2", "down2")]
        xt = _merge_ffn(xt, attn, sg, ga, gs, bf16[("w_out", layer)], (n2, *ffn2_w), fgain, layer,
                        layer == DEPTH - 1)
    return xt.reshape(batch, seq, D_MODEL)
```

```python
import functools
import math

import numpy as np
import jax
import jax.numpy as jnp
from jax import lax
from jax.experimental import pallas as pl
from jax.experimental.pallas import tpu as pltpu

F32 = jnp.float32
BF16 = jnp.bfloat16

D_MODEL = 1024
DEPTH = 4
HEAD_DIM = 64
D_ATTN = 512
D_SGU = 512
N_SGU_GROUPS = 4
CHUNK = 128
D_IN = 3 * D_ATTN + 2 * D_SGU
D_FF = 2816
DILATIONS = (1, 4, 16)
N_PAT = len(DILATIONS)
BAND = 64
ROPE_THETA = 500000.0
ROPE_DIM = HEAD_DIM // 4
EPS = 1e-6
NEG_INF = -1e30
LOG2_E = math.log2(math.e)
LN_2 = math.log(2.0)

LANES = 128
SUBLANES = 8
V7X_VMEM_BYTES = 64 * 1024 * 1024
VMEM_LIMIT = V7X_VMEM_BYTES * 7 // 8

TM = 512
FF_CHUNK = 256
N_FF_CHUNKS = D_FF // FF_CHUNK
N_SLABS = D_ATTN // LANES
MAX_ATT_ROWS = 2048
QB = 2 * BAND
KW = QB + 2 * BAND


def _rms(x, gain):
    return x * lax.rsqrt(jnp.mean(x * x, axis=-1, keepdims=True) + EPS) * gain


def _params(n_axes):
    return pltpu.CompilerParams(dimension_semantics=("arbitrary",) * n_axes,
                                vmem_limit_bytes=VMEM_LIMIT)


def _resident(shape, layer):
    nd = len(shape)
    return pl.BlockSpec((None,) + shape, lambda *_: (layer,) + (0,) * nd,
                        pipeline_mode=pl.Buffered(1))


def _resident_weight(shape):
    return pl.BlockSpec(shape, lambda *_: (0,) * len(shape), pipeline_mode=pl.Buffered(1))


def _view_shape(tokens, dil, dtype):
    return jax.ShapeDtypeStruct((tokens // dil, dil * D_ATTN), dtype)


def _ordering_zero(values):
    acc = None
    for v in values:
        bits = pltpu.bitcast(v[0:SUBLANES, 0:LANES], jnp.uint32)
        acc = bits if acc is None else acc | bits
    return ((acc >> 16) >> 16)[0:1, :].astype(F32)


def _ffn_apply(x, gain_ref, wg_ref, wu_ref, wd_ref, act_ref, side_work=()):
    h = _rms(x, gain_ref[...]).astype(BF16)
    for c in range(N_FF_CHUNKS):
        cols = slice(c * FF_CHUNK, (c + 1) * FF_CHUNK)
        g = jnp.dot(h, wg_ref[:, cols], preferred_element_type=F32)
        u = jnp.dot(h, wu_ref[:, cols], preferred_element_type=F32)
        a = g * jax.nn.sigmoid(g) * u
        todo = side_work[c * len(side_work) // N_FF_CHUNKS:(c + 1) * len(side_work) // N_FF_CHUNKS]
        if todo:
            zero = _ordering_zero([v for work in todo for v in work()])
            a = a + jnp.concatenate([zero] * (FF_CHUNK // LANES), axis=1)
        act_ref[:, cols] = a.astype(BF16)
    return x + 0.5 * jnp.dot(act_ref[...], wd_ref[...], preferred_element_type=F32)


def _ffn_specs(layer):
    return [_resident((1, D_MODEL), layer), _resident_weight((D_MODEL, D_FF)),
            _resident_weight((D_MODEL, D_FF)), _resident_weight((D_FF, D_MODEL))]


def _gelu(x):
    return 0.5 * x * (1.0 + lax.erf(x * (1.0 / math.sqrt(2.0))))


def _inproj_finish_work(raw, cos_ref, sina_ref, sinb_ref, lng_ref, lnb_ref, sw_ref, sb_ref,
                        qkv_refs, sg_ref, stage_ref):
    def qkv_slab(which, rotary, scale, s):
        def work():
            t = raw[which][:, s * LANES:(s + 1) * LANES]
            if rotary:
                half = ROPE_DIM // 2
                t = (t * cos_ref[...] + pltpu.roll(t, half, 1) * sina_ref[...]
                     + pltpu.roll(t, LANES - half, 1) * sinb_ref[...])
            if scale != 1.0:
                t = t * scale
            col0 = which * DILATIONS[0] * D_ATTN + s * LANES
            qkv_refs[0][:, col0:col0 + LANES] = t.astype(BF16)
            stage_ref[0, s] = t
            produced = [t]
            for k in range(1, N_PAT):
                d_prev, dil = DILATIONS[k - 1], DILATIONS[k]
                ratio, n_prev, n = dil // d_prev, TM // d_prev, TM // dil
                for e in range(d_prev):
                    for c in range(ratio):
                        r = c * d_prev + e
                        blk = stage_ref[k - 1, s, pl.ds(e * n_prev + c, n, stride=ratio), :]
                        col0 = (which * dil + r) * D_ATTN + s * LANES
                        qkv_refs[k][:, col0:col0 + LANES] = blk.astype(BF16)
                        if k + 1 < N_PAT:
                            stage_ref[k, s, r * n:(r + 1) * n, :] = blk
                        produced.append(blk)
            return produced
        return work

    def sgu_chunk(c):
        def work():
            rows = slice(c * CHUNK, (c + 1) * CHUNK)
            u = _gelu(raw[3][rows, :])
            g = _gelu(raw[4][rows, :])
            gc = g - jnp.mean(g, axis=-1, keepdims=True)
            gn = (gc * lax.rsqrt(jnp.mean(gc * gc, axis=-1, keepdims=True) + EPS) * lng_ref[...]
                  + lnb_ref[...]).astype(BF16)
            group = D_SGU // N_SGU_GROUPS
            produced = []
            for gi in range(N_SGU_GROUPS):
                cols = slice(gi * group, (gi + 1) * group)
                mixed = jnp.dot(sw_ref[gi], gn[:, cols], preferred_element_type=F32) + sb_ref[gi]
                gated = u[:, cols] * mixed
                sg_ref[rows, cols] = gated.astype(BF16)
                produced.append(gated)
            return produced
        return work

    q_scale = LOG2_E / math.sqrt(HEAD_DIM)
    pieces = [qkv_slab(0, True, q_scale, s) for s in range(N_SLABS)]
    pieces += [qkv_slab(1, True, 1.0, s) for s in range(N_SLABS)]
    pieces += [qkv_slab(2, False, 1.0, s) for s in range(N_SLABS)]
    pieces += [sgu_chunk(c) for c in range(TM // CHUNK)]
    return pieces


def _ffn_inproj_kernel(x_ref, *refs):
    ffn_refs, gain_ref, w_ref, finish_refs = refs[:4], refs[4], refs[5], refs[6:13]
    xo_ref, qkv_refs = refs[13], refs[14:14 + N_PAT]
    sg_ref, act_ref, stage_ref = refs[14 + N_PAT:]
    x = _ffn_apply(x_ref[...], *ffn_refs, act_ref)
    xo_ref[...] = x
    h = _rms(x, gain_ref[...]).astype(BF16)
    raw = [jnp.dot(h, w_ref[:, c0:c0 + D_ATTN], preferred_element_type=F32)
           for c0 in range(0, D_IN, D_ATTN)]
    for work in _inproj_finish_work(raw, *finish_refs, qkv_refs, sg_ref, stage_ref):
        work()


def _ffn_inproj(x, ffn_w, gain, w_in, rope, lng, lnb, sw, sb, layer, seq):
    tokens = x.shape[0]
    rows = lambda s: (s, 0)
    table = pl.BlockSpec((TM, LANES), lambda s: (s % (seq // TM), 0))
    views = [pl.BlockSpec((TM // d, 3 * d * D_ATTN), rows) for d in DILATIONS]
    view_shapes = [jax.ShapeDtypeStruct((tokens // d, 3 * d * D_ATTN), BF16) for d in DILATIONS]
    outs = pl.pallas_call(
        _ffn_inproj_kernel,
        grid=(tokens // TM,),
        in_specs=[pl.BlockSpec((TM, D_MODEL), rows)] + _ffn_specs(layer)
                 + [_resident((1, D_MODEL), layer), _resident_weight((D_MODEL, D_IN)),
                    table, table, table,
                    _resident((1, D_SGU), layer), _resident((1, D_SGU), layer),
                    _resident((N_SGU_GROUPS, CHUNK, CHUNK), layer),
                    _resident((N_SGU_GROUPS, CHUNK, 1), layer)],
        out_specs=[pl.BlockSpec((TM, D_MODEL), rows)] + views + [pl.BlockSpec((TM, D_SGU), rows)],
        out_shape=[jax.ShapeDtypeStruct((tokens, D_MODEL), F32)] + view_shapes
                  + [jax.ShapeDtypeStruct((tokens, D_SGU), BF16)],
        scratch_shapes=[pltpu.VMEM((TM, D_FF), BF16), pltpu.VMEM((N_PAT - 1, N_SLABS, TM, LANES), F32)],
        compiler_params=_params(1),
        name="ffn_inproj",
    )(x, *ffn_w, gain, w_in, *rope, lng, lnb, sw, sb)
    return outs[0], outs[1:1 + N_PAT], outs[-1]


def _rope_tables(seq):
    half = ROPE_DIM // 2
    inv_freq = ROPE_THETA ** (-np.arange(0, ROPE_DIM, 2, dtype=np.float64) / ROPE_DIM)
    ang = np.arange(seq, dtype=np.float64)[:, None] * inv_freq[None, :]
    cos = np.ones((seq, LANES))
    sina = np.zeros((seq, LANES))
    sinb = np.zeros((seq, LANES))
    for head0 in range(0, LANES, HEAD_DIM):
        cos[:, head0:head0 + half] = np.cos(ang)
        cos[:, head0 + half:head0 + 2 * half] = np.cos(ang)
        sinb[:, head0:head0 + half] = -np.sin(ang)
        sina[:, head0 + half:head0 + 2 * half] = np.sin(ang)
    return tuple(jnp.asarray(t, F32) for t in (cos, sina, sinb))


def _attn_kernel(q_ref, kc_ref, kp_ref, kn_ref, vc_ref, vp_ref, vn_ref, *refs,
                 sub_len, rows, n_res, n_cast):
    cast_src, (o_ref, lse_ref) = refs[:n_cast], refs[n_cast:n_cast + 2]
    cast_dst, (kbuf, vbuf) = refs[n_cast + 2:2 * n_cast + 2], refs[2 * n_cast + 2:]
    for src, dst in zip(cast_src, cast_dst):
        dst[...] = src[...].astype(BF16)

    ones = jnp.ones((rows + 2 * BAND, LANES), BF16)
    for res in range(n_res):
        c0 = res * D_ATTN
        kbuf[res, 0:BAND] = kp_ref[:, c0:c0 + D_ATTN]
        kbuf[res, BAND:BAND + rows] = kc_ref[:, c0:c0 + D_ATTN]
        kbuf[res, BAND + rows:] = kn_ref[:, c0:c0 + D_ATTN]
        for pair in range(N_SLABS):
            src = slice(c0 + pair * LANES, c0 + (pair + 1) * LANES)
            dst = slice(2 * pair * LANES, (2 * pair + 1) * LANES)
            vbuf[res, 0:BAND, dst] = vp_ref[:, src]
            vbuf[res, BAND:BAND + rows, dst] = vc_ref[:, src]
            vbuf[res, BAND + rows:, dst] = vn_ref[:, src]
            vbuf[res, :, (2 * pair + 1) * LANES:(2 * pair + 2) * LANES] = ones

    first_row = (pl.program_id(0) % (sub_len // rows)) * rows
    delta = (lax.broadcasted_iota(jnp.int32, (QB, KW), 1)
             - lax.broadcasted_iota(jnp.int32, (QB, KW), 0))
    band_bias = jnp.where((delta >= 0) & (delta <= 2 * BAND), 0.0, NEG_INF)
    key_col = lax.broadcasted_iota(jnp.int32, (1, KW), 1)
    lane = lax.broadcasted_iota(jnp.int32, (1, LANES), 1)
    low_head = lane < HEAD_DIM

    for j in range(rows // QB):
        row0 = j * QB
        key_pos = key_col + (first_row + row0 - BAND)
        bias = band_bias + jnp.where((key_pos >= 0) & (key_pos < sub_len), 0.0, NEG_INF)
        bias = jnp.concatenate([bias, bias], axis=0)
        for res in range(n_res):
            lse_tile = jnp.zeros((QB, LANES), F32)
            for pair in range(N_SLABS):
                cols = slice(res * D_ATTN + pair * LANES, res * D_ATTN + (pair + 1) * LANES)
                q2 = q_ref[row0:row0 + QB, cols]
                zero = jnp.zeros_like(q2)
                qs = jnp.concatenate([jnp.where(low_head, q2, zero), jnp.where(low_head, zero, q2)],
                                     axis=0)
                k2 = kbuf[res, row0:row0 + KW, pair * LANES:(pair + 1) * LANES]
                s = lax.dot_general(qs, k2, (((1,), (1,)), ((), ())),
                                    preferred_element_type=F32) + bias
                m = jnp.max(s, axis=-1, keepdims=True)
                p = jnp.exp2(s - m).astype(BF16)
                pv = jnp.dot(p, vbuf[res, row0:row0 + KW, 2 * pair * LANES:(2 * pair + 2) * LANES],
                             preferred_element_type=F32)
                num = jnp.where(low_head, pv[:QB, :LANES], pv[QB:, :LANES])
                den = jnp.where(low_head, pv[:QB, LANES:], pv[QB:, LANES:])
                o_ref[row0:row0 + QB, cols] = (num / den).astype(BF16)
                lse_pair = jnp.where(low_head, m[:QB], m[QB:]) * LN_2 + jnp.log(den)
                lse_tile = jnp.where((lane == pair) | (lane == HEAD_DIM + pair), lse_pair, lse_tile)
            lse_ref[row0:row0 + QB, res * LANES:(res + 1) * LANES] = lse_tile


def _attention(qkv, dil, seq, cast_jobs=()):
    total_rows = qkv.shape[0]
    sub_len = seq // dil
    rows = min(MAX_ATT_ROWS, sub_len)
    n_res = min(MAX_ATT_ROWS // rows, dil)
    assert sub_len % rows == 0 and rows % QB == 0 and dil % n_res == 0
    halos_per_step = rows // BAND
    last_halo = total_rows // BAND - 1
    width = n_res * D_ATTN
    grid = (total_rows // rows, dil // n_res)
    out_block = pl.BlockSpec((rows, width), lambda i, r: (i, r))

    def blocks(which):
        c0 = which * grid[1]
        return (pl.BlockSpec((rows, width), lambda i, r: (i, c0 + r)),
                pl.BlockSpec((BAND, width),
                             lambda i, r: (jnp.maximum(i * halos_per_step - 1, 0), c0 + r)),
                pl.BlockSpec((BAND, width),
                             lambda i, r: (jnp.minimum((i + 1) * halos_per_step, last_halo), c0 + r)))
    n_steps = grid[0] * grid[1]
    cast_in, cast_out, cast_shapes = [], [], []
    for w, w_layer in cast_jobs:
        _, w_rows, w_cols = w.shape
        chunk = w_rows // n_steps
        assert chunk * n_steps == w_rows and chunk % (2 * SUBLANES) == 0
        cast_in.append(pl.BlockSpec((None, chunk, w_cols),
                                    lambda i, r, w_layer=w_layer: (w_layer, i * grid[1] + r, 0)))
        cast_out.append(pl.BlockSpec((chunk, w_cols), lambda i, r: (i * grid[1] + r, 0)))
        cast_shapes.append(jax.ShapeDtypeStruct((w_rows, w_cols), BF16))
    outs = pl.pallas_call(
        functools.partial(_attn_kernel, sub_len=sub_len, rows=rows, n_res=n_res,
                          n_cast=len(cast_jobs)),
        grid=grid,
        in_specs=[blocks(0)[0], *blocks(1), *blocks(2)] + cast_in,
        out_specs=[out_block, pl.BlockSpec((rows, n_res * LANES), lambda i, r: (i, r))] + cast_out,
        out_shape=[jax.ShapeDtypeStruct((total_rows, dil * D_ATTN), BF16),
                   jax.ShapeDtypeStruct((total_rows, dil * LANES), F32)] + cast_shapes,
        scratch_shapes=[pltpu.VMEM((n_res, rows + 2 * BAND, D_ATTN), BF16),
                        pltpu.VMEM((n_res, rows + 2 * BAND, 2 * D_ATTN), BF16)],
        compiler_params=_params(2),
        name=f"attn_d{dil}",
    )(*[qkv] * 7, *[w for w, _ in cast_jobs])
    return outs[0], outs[1], list(outs[2:])


def _merge_work(o_refs, l_refs, sg_ref, ga_ref, gs_ref, norm_ref, stage_refs):
    def unview_slab(s):
        def work():
            produced = []
            for p, dil in enumerate(DILATIONS[1:]):
                for r in range(dil):
                    blk = o_refs[p + 1][:, r * D_ATTN + s * LANES:r * D_ATTN + (s + 1) * LANES]
                    blk = blk.astype(F32)
                    stage_refs[2 * p][s, pl.ds(r, TM // dil, stride=dil), :] = blk
                    produced.append(blk)
            return produced
        return work

    def unview_lse():
        produced = []
        for p, dil in enumerate(DILATIONS[1:]):
            for r in range(dil):
                blk = l_refs[p + 1][:, r * LANES:(r + 1) * LANES]
                stage_refs[2 * p + 1][pl.ds(r, TM // dil, stride=dil), :] = blk
                produced.append(blk)
        return produced

    def merge_rows(c):
        def work():
            rows = slice(c * CHUNK, (c + 1) * CHUNK)
            low_head = lax.broadcasted_iota(jnp.int32, (1, LANES), 1) < HEAD_DIM
            lses = [l_refs[0][rows, :]] + [stage_refs[2 * p + 1][rows, :] for p in range(N_PAT - 1)]
            m = functools.reduce(jnp.maximum, lses)
            es = [jnp.exp(l - m) for l in lses]
            inv = 1.0 / sum(es)
            slabs = []
            for s in range(N_SLABS):
                cols = slice(s * LANES, (s + 1) * LANES)
                outs = [o_refs[0][rows, cols].astype(F32)]
                outs += [stage_refs[2 * p][s, rows, :] for p in range(N_PAT - 1)]
                acc = None
                for e, o in zip(es, outs):
                    w = e * inv
                    w = jnp.where(low_head, w[:, s:s + 1], w[:, HEAD_DIM + s:HEAD_DIM + s + 1])
                    acc = w * o if acc is None else acc + w * o
                slabs.append(acc)
            na = _rms(jnp.concatenate(slabs, axis=1), ga_ref[...])
            ns = _rms(sg_ref[rows, :].astype(F32), gs_ref[...])
            norm_ref[rows, 0:D_ATTN] = na.astype(BF16)
            norm_ref[rows, D_ATTN:] = ns.astype(BF16)
            return [na, ns]
        return work

    return ([unview_slab(s) for s in range(N_SLABS)] + [unview_lse]
            + [merge_rows(c) for c in range(TM // CHUNK)])


def _merge_ffn_kernel(x_ref, *refs, n_tiles, final):
    o_refs, l_refs = refs[0:2 * N_PAT:2], refs[1:2 * N_PAT:2]
    sg_ref, ga_ref, gs_ref, wo_ref = refs[2 * N_PAT:2 * N_PAT + 4]
    ffn_refs = refs[2 * N_PAT + 4:2 * N_PAT + 8]
    fgain_ref, out_ref, act_ref, norm_ref = refs[2 * N_PAT + 8:2 * N_PAT + 12]
    stage_refs = refs[2 * N_PAT + 12:]
    step = pl.program_id(0)
    merge = _merge_work(o_refs, l_refs, sg_ref, ga_ref, gs_ref, norm_ref, stage_refs)

    def project_and_ffn(side_work):
        x = x_ref[...] + jnp.dot(norm_ref[...], wo_ref[...], preferred_element_type=F32)
        y = _ffn_apply(x, *ffn_refs, act_ref, side_work=side_work)
        out_ref[...] = _rms(y, fgain_ref[...]) if final else y

    @pl.when(step == 0)
    def _():
        for work in merge:
            work()

    @pl.when((step > 0) & (step < n_tiles))
    def _():
        project_and_ffn(merge)

    @pl.when(step == n_tiles)
    def _():
        project_and_ffn(())


def _merge_ffn(x, attn, sg, ga, gs, wo, ffn_w, fgain, layer, final):
    tokens = x.shape[0]
    n_tiles = tokens // TM
    lead = lambda s: (jnp.minimum(s, n_tiles - 1), 0)
    lag = lambda s: (jnp.maximum(s - 1, 0), 0)
    views = [pl.BlockSpec((TM // d, d * width), lead) for d in DILATIONS for width in (D_ATTN, LANES)]
    return pl.pallas_call(
        functools.partial(_merge_ffn_kernel, n_tiles=n_tiles, final=final),
        grid=(n_tiles + 1,),
        in_specs=[pl.BlockSpec((TM, D_MODEL), lag)] + views + [pl.BlockSpec((TM, D_SGU), lead)]
                 + [_resident((1, D_ATTN), layer), _resident((1, D_SGU), layer),
                    _resident_weight((D_ATTN + D_SGU, D_MODEL))]
                 + _ffn_specs(layer) + [pl.BlockSpec((1, D_MODEL), lambda s: (0, 0))],
        out_specs=pl.BlockSpec((TM, D_MODEL), lag),
        out_shape=jax.ShapeDtypeStruct((tokens, D_MODEL), F32),
        scratch_shapes=[pltpu.VMEM((TM, D_FF), BF16), pltpu.VMEM((TM, D_ATTN + D_SGU), BF16)]
                       + [pltpu.VMEM((N_SLABS, TM, LANES), F32), pltpu.VMEM((TM, LANES), F32)] * (N_PAT - 1),
        compiler_params=_params(1),
        name="merge_ffn",
    )(x, *attn, sg, ga, gs, wo, *ffn_w, fgain)


def kernel(x, norm_ffn1, ffn1_w_gate, ffn1_w_up, ffn1_w_down, norm_mix, w_in, sgu_ln_g, sgu_ln_b,
           sgu_w, sgu_b, out_norm_attn, out_norm_sgu, w_out, norm_ffn2, ffn2_w_gate, ffn2_w_up,
           ffn2_w_down, final_norm):
    batch, seq, _ = x.shape
    tokens = batch * seq
    assert x.shape[2] == D_MODEL and tokens % TM == 0 and seq % TM == 0 and TM % CHUNK == 0

    row = lambda p: p[:, None, :]
    n1, n2, nm = row(norm_ffn1), row(norm_ffn2), row(norm_mix)
    lng, lnb = row(sgu_ln_g), row(sgu_ln_b)
    ga, gs = row(out_norm_attn), row(out_norm_sgu)
    sgu_w_b = sgu_w.astype(BF16)
    sgu_b_col = sgu_b[..., None]
    fgain = final_norm[None, :]
    rope = _rope_tables(seq)

    stacked = dict(gate1=ffn1_w_gate, up1=ffn1_w_up, down1=ffn1_w_down, w_in=w_in,
                   gate2=ffn2_w_gate, up2=ffn2_w_up, down2=ffn2_w_down, w_out=w_out)
    early = ("gate1", "up1", "down1", "w_in")
    next_layer_jobs = (("gate1", "up1"), ("gate2", "up2"), ("down1", "down2", "w_in", "w_out"))
    same_layer_jobs = (("gate2",), ("up2",), ("down2", "w_out"))
    bf16 = {(name, 0): stacked[name][0].astype(BF16) for name in early}

    xt = x.reshape(tokens, D_MODEL)
    for layer in range(DEPTH):
        ffn1_w = [bf16[(name, layer)] for name in ("gate1", "up1", "down1")]
        xt, qkvs, sg = _ffn_inproj(xt, (n1, *ffn1_w), nm, bf16[("w_in", layer)], rope, lng, lnb,
                                         sgu_w_b, sgu_b_col, layer, seq)
        attn = []
        for p, (qkv, dil) in enumerate(zip(qkvs, DILATIONS)):
            jobs = [(name, layer + 1) for name in next_layer_jobs[p]] if layer + 1 < DEPTH else []
            jobs += [(name, layer) for name in same_layer_jobs[p] if (name, layer) not in bf16]
            o, lse, cast = _attention(qkv, dil, seq, [(stacked[n], l) for n, l in jobs])
            attn += [o, lse]
            bf16.update(zip(jobs, cast))
        ffn2_w = [bf16[(name, layer)] for name in ("gate2", "up2", "down2")]
        xt = _merge_ffn(xt, attn, sg, ga, gs, bf16[("w_out", layer)], (n2, *ffn2_w), fgain, layer,
                        layer == DEPTH - 1)
    return xt.reshape(batch, seq, D_MODEL)
```

```python
import functools
import math

import numpy as np
import jax
import jax.numpy as jnp
from jax import lax
from jax.experimental import pallas as pl
from jax.experimental.pallas import tpu as pltpu

F32 = jnp.float32
BF16 = jnp.bfloat16

D_MODEL = 1024
DEPTH = 4
HEAD_DIM = 64
D_ATTN = 512
D_SGU = 512
N_SGU_GROUPS = 4
CHUNK = 128
D_IN = 3 * D_ATTN + 2 * D_SGU
D_FF = 2816
DILATIONS = (1, 4, 16)
N_PAT = len(DILATIONS)
BAND = 64
ROPE_THETA = 500000.0
ROPE_DIM = HEAD_DIM // 4
EPS = 1e-6
NEG_INF = -1e30
LOG2_E = math.log2(math.e)
LN_2 = math.log(2.0)

LANES = 128
SUBLANES = 8
V7X_VMEM_BYTES = 64 * 1024 * 1024
VMEM_LIMIT = V7X_VMEM_BYTES * 7 // 8

TM = 512
FF_CHUNK = 256
N_FF_CHUNKS = D_FF // FF_CHUNK
N_SLABS = D_ATTN // LANES
MAX_ATT_ROWS = 2048
QB = 2 * BAND
KW = QB + 2 * BAND


def _rms(x, gain):
    return x * lax.rsqrt(jnp.mean(x * x, axis=-1, keepdims=True) + EPS) * gain


def _params(n_axes):
    return pltpu.CompilerParams(dimension_semantics=("arbitrary",) * n_axes,
                                vmem_limit_bytes=VMEM_LIMIT)


def _resident(shape, layer):
    nd = len(shape)
    return pl.BlockSpec((None,) + shape, lambda *_: (layer,) + (0,) * nd,
                        pipeline_mode=pl.Buffered(1))


def _resident_weight(shape):
    return pl.BlockSpec(shape, lambda *_: (0,) * len(shape), pipeline_mode=pl.Buffered(1))


def _view_shape(tokens, dil, dtype):
    return jax.ShapeDtypeStruct((tokens // dil, dil * D_ATTN), dtype)


def _ordering_zero(values):
    acc = None
    for v in values:
        bits = pltpu.bitcast(v[0:SUBLANES, 0:LANES], jnp.uint32)
        acc = bits if acc is None else acc | bits
    return ((acc >> 16) >> 16)[0:1, :].astype(F32)


def _ffn_apply(x, gain_ref, wgu_ref, wd_ref, act_ref, side_work=()):
    h = _rms(x, gain_ref[...]).astype(BF16)
    for c in range(N_FF_CHUNKS):
        cols = slice(c * FF_CHUNK, (c + 1) * FF_CHUNK)
        gu = jnp.dot(h, wgu_ref[:, 2 * c * FF_CHUNK:2 * (c + 1) * FF_CHUNK],
                     preferred_element_type=F32)
        g, u = gu[:, :FF_CHUNK], gu[:, FF_CHUNK:]
        a = g * jax.nn.sigmoid(g) * u
        todo = side_work[c * len(side_work) // N_FF_CHUNKS:(c + 1) * len(side_work) // N_FF_CHUNKS]
        if todo:
            zero = _ordering_zero([v for work in todo for v in work()])
            a = a + jnp.concatenate([zero] * (FF_CHUNK // LANES), axis=1)
        act_ref[:, cols] = a.astype(BF16)
    return x + 0.5 * jnp.dot(act_ref[...], wd_ref[...], preferred_element_type=F32)


def _ffn_specs(layer):
    return [_resident((1, D_MODEL), layer), _resident_weight((D_MODEL, 2 * D_FF)),
            _resident_weight((D_FF, D_MODEL))]


def _gelu(x):
    return 0.5 * x * (1.0 + lax.erf(x * (1.0 / math.sqrt(2.0))))


def _inproj_finish_work(raw, cos_ref, sina_ref, sinb_ref, lng_ref, lnb_ref, sw_ref, sb_ref,
                        q_refs, k_refs, v_refs, sg_ref, stage_ref):
    def qkv_slab(col0, dst_refs, rotary, scale, s):
        def work():
            t = raw[col0 // D_ATTN][:, s * LANES:(s + 1) * LANES]
            if rotary:
                half = ROPE_DIM // 2
                t = (t * cos_ref[...] + pltpu.roll(t, half, 1) * sina_ref[...]
                     + pltpu.roll(t, LANES - half, 1) * sinb_ref[...])
            if scale != 1.0:
                t = t * scale
            dst_refs[0][:, s * LANES:(s + 1) * LANES] = t.astype(BF16)
            stage_ref[0, s] = t
            produced = [t]
            for k in range(1, N_PAT):
                d_prev, dil = DILATIONS[k - 1], DILATIONS[k]
                ratio, n_prev, n = dil // d_prev, TM // d_prev, TM // dil
                for e in range(d_prev):
                    for c in range(ratio):
                        r = c * d_prev + e
                        blk = stage_ref[k - 1, s, pl.ds(e * n_prev + c, n, stride=ratio), :]
                        dst_refs[k][:, r * D_ATTN + s * LANES:r * D_ATTN + (s + 1) * LANES] = (
                            blk.astype(BF16))
                        if k + 1 < N_PAT:
                            stage_ref[k, s, r * n:(r + 1) * n, :] = blk
                        produced.append(blk)
            return produced
        return work

    def sgu_chunk(c):
        def work():
            rows = slice(c * CHUNK, (c + 1) * CHUNK)
            u = _gelu(raw[3][rows, :])
            g = _gelu(raw[4][rows, :])
            gc = g - jnp.mean(g, axis=-1, keepdims=True)
            gn = (gc * lax.rsqrt(jnp.mean(gc * gc, axis=-1, keepdims=True) + EPS) * lng_ref[...]
                  + lnb_ref[...]).astype(BF16)
            group = D_SGU // N_SGU_GROUPS
            produced = []
            for gi in range(N_SGU_GROUPS):
                cols = slice(gi * group, (gi + 1) * group)
                mixed = jnp.dot(sw_ref[gi], gn[:, cols], preferred_element_type=F32) + sb_ref[gi]
                gated = u[:, cols] * mixed
                sg_ref[rows, cols] = gated.astype(BF16)
                produced.append(gated)
            return produced
        return work

    q_scale = LOG2_E / math.sqrt(HEAD_DIM)
    pieces = [qkv_slab(0, q_refs, True, q_scale, s) for s in range(N_SLABS)]
    pieces += [qkv_slab(D_ATTN, k_refs, True, 1.0, s) for s in range(N_SLABS)]
    pieces += [qkv_slab(2 * D_ATTN, v_refs, False, 1.0, s) for s in range(N_SLABS)]
    pieces += [sgu_chunk(c) for c in range(TM // CHUNK)]
    return pieces


def _ffn_inproj_kernel(x_ref, *refs):
    ffn_refs, gain_ref, w_ref, finish_refs = refs[:3], refs[3], refs[4], refs[5:12]
    xo_ref = refs[12]
    q_refs, k_refs, v_refs = (refs[13 + i * N_PAT:13 + (i + 1) * N_PAT] for i in range(3))
    sg_ref, act_ref, stage_ref = refs[13 + 3 * N_PAT:]
    x = _ffn_apply(x_ref[...], *ffn_refs, act_ref)
    xo_ref[...] = x
    h = _rms(x, gain_ref[...]).astype(BF16)
    raw = [jnp.dot(h, w_ref[:, c0:c0 + D_ATTN], preferred_element_type=F32)
           for c0 in range(0, D_IN, D_ATTN)]
    for work in _inproj_finish_work(raw, *finish_refs, q_refs, k_refs, v_refs, sg_ref, stage_ref):
        work()


def _ffn_inproj(x, ffn_w, gain, w_in, rope, lng, lnb, sw, sb, layer, seq):
    tokens = x.shape[0]
    rows = lambda s: (s, 0)
    table = pl.BlockSpec((TM, LANES), lambda s: (s % (seq // TM), 0))
    views = [pl.BlockSpec((TM // d, d * D_ATTN), rows) for d in DILATIONS]
    view_shapes = [_view_shape(tokens, d, BF16) for d in DILATIONS]
    outs = pl.pallas_call(
        _ffn_inproj_kernel,
        grid=(tokens // TM,),
        in_specs=[pl.BlockSpec((TM, D_MODEL), rows)] + _ffn_specs(layer)
                 + [_resident((1, D_MODEL), layer), _resident_weight((D_MODEL, D_IN)),
                    table, table, table,
                    _resident((1, D_SGU), layer), _resident((1, D_SGU), layer),
                    _resident((N_SGU_GROUPS, CHUNK, CHUNK), layer),
                    _resident((N_SGU_GROUPS, CHUNK, 1), layer)],
        out_specs=[pl.BlockSpec((TM, D_MODEL), rows)] + views * 3 + [pl.BlockSpec((TM, D_SGU), rows)],
        out_shape=[jax.ShapeDtypeStruct((tokens, D_MODEL), F32)] + view_shapes * 3
                  + [jax.ShapeDtypeStruct((tokens, D_SGU), BF16)],
        scratch_shapes=[pltpu.VMEM((TM, D_FF), BF16), pltpu.VMEM((N_PAT - 1, N_SLABS, TM, LANES), F32)],
        compiler_params=_params(1),
        name="ffn_inproj",
    )(x, *ffn_w, gain, w_in, *rope, lng, lnb, sw, sb)
    return (outs[0], outs[1:1 + N_PAT], outs[1 + N_PAT:1 + 2 * N_PAT],
            outs[1 + 2 * N_PAT:1 + 3 * N_PAT], outs[-1])


def _rope_tables(seq):
    half = ROPE_DIM // 2
    inv_freq = ROPE_THETA ** (-np.arange(0, ROPE_DIM, 2, dtype=np.float64) / ROPE_DIM)
    ang = np.arange(seq, dtype=np.float64)[:, None] * inv_freq[None, :]
    cos = np.ones((seq, LANES))
    sina = np.zeros((seq, LANES))
    sinb = np.zeros((seq, LANES))
    for head0 in range(0, LANES, HEAD_DIM):
        cos[:, head0:head0 + half] = np.cos(ang)
        cos[:, head0 + half:head0 + 2 * half] = np.cos(ang)
        sinb[:, head0:head0 + half] = -np.sin(ang)
        sina[:, head0 + half:head0 + 2 * half] = np.sin(ang)
    return tuple(jnp.asarray(t, F32) for t in (cos, sina, sinb))


def _attn_kernel(q_ref, kc_ref, kp_ref, kn_ref, vc_ref, vp_ref, vn_ref, *refs,
                 sub_len, rows, n_res, cast_arity):
    n_src, n_dst = sum(cast_arity), len(cast_arity)
    cast_src, (o_ref, lse_ref) = refs[:n_src], refs[n_src:n_src + 2]
    cast_dst, (kbuf, vbuf) = refs[n_src + 2:n_src + 2 + n_dst], refs[n_src + 2 + n_dst:]
    first = 0
    for arity, dst in zip(cast_arity, cast_dst):
        srcs = cast_src[first:first + arity]
        first += arity
        if arity == 1:
            dst[...] = srcs[0][...].astype(BF16)
        else:
            for c in range(N_FF_CHUNKS):
                for i, src in enumerate(srcs):
                    col0 = (arity * c + i) * FF_CHUNK
                    dst[:, col0:col0 + FF_CHUNK] = src[:, c * FF_CHUNK:(c + 1) * FF_CHUNK].astype(BF16)

    ones = jnp.ones((rows + 2 * BAND, LANES), BF16)
    for res in range(n_res):
        c0 = res * D_ATTN
        kbuf[res, 0:BAND] = kp_ref[:, c0:c0 + D_ATTN]
        kbuf[res, BAND:BAND + rows] = kc_ref[:, c0:c0 + D_ATTN]
        kbuf[res, BAND + rows:] = kn_ref[:, c0:c0 + D_ATTN]
        for pair in range(N_SLABS):
            src = slice(c0 + pair * LANES, c0 + (pair + 1) * LANES)
            dst = slice(2 * pair * LANES, (2 * pair + 1) * LANES)
            vbuf[res, 0:BAND, dst] = vp_ref[:, src]
            vbuf[res, BAND:BAND + rows, dst] = vc_ref[:, src]
            vbuf[res, BAND + rows:, dst] = vn_ref[:, src]
            vbuf[res, :, (2 * pair + 1) * LANES:(2 * pair + 2) * LANES] = ones

    first_row = (pl.program_id(0) % (sub_len // rows)) * rows
    delta = (lax.broadcasted_iota(jnp.int32, (QB, KW), 1)
             - lax.broadcasted_iota(jnp.int32, (QB, KW), 0))
    band_bias = jnp.where((delta >= 0) & (delta <= 2 * BAND), 0.0, NEG_INF)
    key_col = lax.broadcasted_iota(jnp.int32, (1, KW), 1)
    lane = lax.broadcasted_iota(jnp.int32, (1, LANES), 1)
    low_head = lane < HEAD_DIM

    for j in range(rows // QB):
        row0 = j * QB
        key_pos = key_col + (first_row + row0 - BAND)
        bias = band_bias + jnp.where((key_pos >= 0) & (key_pos < sub_len), 0.0, NEG_INF)
        bias = jnp.concatenate([bias, bias], axis=0)
        for res in range(n_res):
            lse_tile = jnp.zeros((QB, LANES), F32)
            for pair in range(N_SLABS):
                cols = slice(res * D_ATTN + pair * LANES, res * D_ATTN + (pair + 1) * LANES)
                q2 = q_ref[row0:row0 + QB, cols]
                zero = jnp.zeros_like(q2)
                qs = jnp.concatenate([jnp.where(low_head, q2, zero), jnp.where(low_head, zero, q2)],
                                     axis=0)
                k2 = kbuf[res, row0:row0 + KW, pair * LANES:(pair + 1) * LANES]
                s = lax.dot_general(qs, k2, (((1,), (1,)), ((), ())),
                                    preferred_element_type=F32) + bias
                m = jnp.max(s, axis=-1, keepdims=True)
                p = jnp.exp2(s - m).astype(BF16)
                pv = jnp.dot(p, vbuf[res, row0:row0 + KW, 2 * pair * LANES:(2 * pair + 2) * LANES],
                             preferred_element_type=F32)
                num = jnp.where(low_head, pv[:QB, :LANES], pv[QB:, :LANES])
                den = jnp.where(low_head, pv[:QB, LANES:], pv[QB:, LANES:])
                o_ref[row0:row0 + QB, cols] = (num / den).astype(BF16)
                lse_pair = jnp.where(low_head, m[:QB], m[QB:]) * LN_2 + jnp.log(den)
                lse_tile = jnp.where((lane == pair) | (lane == HEAD_DIM + pair), lse_pair, lse_tile)
            lse_ref[row0:row0 + QB, res * LANES:(res + 1) * LANES] = lse_tile


def _attention(q, k, v, dil, seq, cast_jobs=()):
    total_rows = q.shape[0]
    sub_len = seq // dil
    rows = min(MAX_ATT_ROWS, sub_len)
    n_res = min(MAX_ATT_ROWS // rows, dil)
    assert sub_len % rows == 0 and rows % QB == 0 and dil % n_res == 0
    halos_per_step = rows // BAND
    last_halo = total_rows // BAND - 1
    width = n_res * D_ATTN
    grid = (total_rows // rows, dil // n_res)
    cur = pl.BlockSpec((rows, width), lambda i, r: (i, r))
    prev = pl.BlockSpec((BAND, width), lambda i, r: (jnp.maximum(i * halos_per_step - 1, 0), r))
    nxt = pl.BlockSpec((BAND, width),
                       lambda i, r: (jnp.minimum((i + 1) * halos_per_step, last_halo), r))
    n_steps = grid[0] * grid[1]
    cast_in, cast_out, cast_shapes = [], [], []
    for ws, w_layer in cast_jobs:
        _, w_rows, w_cols = ws[0].shape
        chunk = w_rows // n_steps
        assert chunk * n_steps == w_rows and chunk % (2 * SUBLANES) == 0
        cast_in += [pl.BlockSpec((None, chunk, w_cols),
                                 lambda i, r, w_layer=w_layer: (w_layer, i * grid[1] + r, 0))] * len(ws)
        cast_out.append(pl.BlockSpec((chunk, len(ws) * w_cols), lambda i, r: (i * grid[1] + r, 0)))
        cast_shapes.append(jax.ShapeDtypeStruct((w_rows, len(ws) * w_cols), BF16))
    outs = pl.pallas_call(
        functools.partial(_attn_kernel, sub_len=sub_len, rows=rows, n_res=n_res,
                          cast_arity=tuple(len(ws) for ws, _ in cast_jobs)),
        grid=grid,
        in_specs=[cur, cur, prev, nxt, cur, prev, nxt] + cast_in,
        out_specs=[cur, pl.BlockSpec((rows, n_res * LANES), lambda i, r: (i, r))] + cast_out,
        out_shape=[jax.ShapeDtypeStruct(q.shape, BF16),
                   jax.ShapeDtypeStruct((total_rows, dil * LANES), F32)] + cast_shapes,
        scratch_shapes=[pltpu.VMEM((n_res, rows + 2 * BAND, D_ATTN), BF16),
                        pltpu.VMEM((n_res, rows + 2 * BAND, 2 * D_ATTN), BF16)],
        compiler_params=_params(2),
        name=f"attn_d{dil}",
    )(q, k, k, k, v, v, v, *[w for ws, _ in cast_jobs for w in ws])
    return outs[0], outs[1], list(outs[2:])


def _merge_work(o_refs, l_refs, sg_ref, ga_ref, gs_ref, norm_ref, stage_refs):
    def unview_slab(s):
        def work():
            produced = []
            for p, dil in enumerate(DILATIONS[1:]):
                for r in range(dil):
                    blk = o_refs[p + 1][:, r * D_ATTN + s * LANES:r * D_ATTN + (s + 1) * LANES]
                    blk = blk.astype(F32)
                    stage_refs[2 * p][s, pl.ds(r, TM // dil, stride=dil), :] = blk
                    produced.append(blk)
            return produced
        return work

    def unview_lse():
        produced = []
        for p, dil in enumerate(DILATIONS[1:]):
            for r in range(dil):
                blk = l_refs[p + 1][:, r * LANES:(r + 1) * LANES]
                stage_refs[2 * p + 1][pl.ds(r, TM // dil, stride=dil), :] = blk
                produced.append(blk)
        return produced

    def merge_rows(c):
        def work():
            rows = slice(c * CHUNK, (c + 1) * CHUNK)
            low_head = lax.broadcasted_iota(jnp.int32, (1, LANES), 1) < HEAD_DIM
            lses = [l_refs[0][rows, :]] + [stage_refs[2 * p + 1][rows, :] for p in range(N_PAT - 1)]
            m = functools.reduce(jnp.maximum, lses)
            es = [jnp.exp(l - m) for l in lses]
            inv = 1.0 / sum(es)
            slabs = []
            for s in range(N_SLABS):
                cols = slice(s * LANES, (s + 1) * LANES)
                outs = [o_refs[0][rows, cols].astype(F32)]
                outs += [stage_refs[2 * p][s, rows, :] for p in range(N_PAT - 1)]
                acc = None
                for e, o in zip(es, outs):
                    w = e * inv
                    w = jnp.where(low_head, w[:, s:s + 1], w[:, HEAD_DIM + s:HEAD_DIM + s + 1])
                    acc = w * o if acc is None else acc + w * o
                slabs.append(acc)
            na = _rms(jnp.concatenate(slabs, axis=1), ga_ref[...])
            ns = _rms(sg_ref[rows, :].astype(F32), gs_ref[...])
            norm_ref[rows, 0:D_ATTN] = na.astype(BF16)
            norm_ref[rows, D_ATTN:] = ns.astype(BF16)
            return [na, ns]
        return work

    return ([unview_slab(s) for s in range(N_SLABS)] + [unview_lse]
            + [merge_rows(c) for c in range(TM // CHUNK)])


def _merge_ffn_kernel(x_ref, *refs, n_tiles, final):
    o_refs, l_refs = refs[0:2 * N_PAT:2], refs[1:2 * N_PAT:2]
    sg_ref, ga_ref, gs_ref, wo_ref = refs[2 * N_PAT:2 * N_PAT + 4]
    ffn_refs = refs[2 * N_PAT + 4:2 * N_PAT + 7]
    fgain_ref, out_ref, act_ref, norm_ref = refs[2 * N_PAT + 7:2 * N_PAT + 11]
    stage_refs = refs[2 * N_PAT + 11:]
    step = pl.program_id(0)
    merge = _merge_work(o_refs, l_refs, sg_ref, ga_ref, gs_ref, norm_ref, stage_refs)

    def project_and_ffn(side_work):
        x = x_ref[...] + jnp.dot(norm_ref[...], wo_ref[...], preferred_element_type=F32)
        y = _ffn_apply(x, *ffn_refs, act_ref, side_work=side_work)
        out_ref[...] = _rms(y, fgain_ref[...]) if final else y

    @pl.when(step == 0)
    def _():
        for work in merge:
            work()

    @pl.when((step > 0) & (step < n_tiles))
    def _():
        project_and_ffn(merge)

    @pl.when(step == n_tiles)
    def _():
        project_and_ffn(())


def _merge_ffn(x, attn, sg, ga, gs, wo, ffn_w, fgain, layer, final):
    tokens = x.shape[0]
    n_tiles = tokens // TM
    lead = lambda s: (jnp.minimum(s, n_tiles - 1), 0)
    lag = lambda s: (jnp.maximum(s - 1, 0), 0)
    views = [pl.BlockSpec((TM // d, d * width), lead) for d in DILATIONS for width in (D_ATTN, LANES)]
    return pl.pallas_call(
        functools.partial(_merge_ffn_kernel, n_tiles=n_tiles, final=final),
        grid=(n_tiles + 1,),
        in_specs=[pl.BlockSpec((TM, D_MODEL), lag)] + views + [pl.BlockSpec((TM, D_SGU), lead)]
                 + [_resident((1, D_ATTN), layer), _resident((1, D_SGU), layer),
                    _resident_weight((D_ATTN + D_SGU, D_MODEL))]
                 + _ffn_specs(layer) + [pl.BlockSpec((1, D_MODEL), lambda s: (0, 0))],
        out_specs=pl.BlockSpec((TM, D_MODEL), lag),
        out_shape=jax.ShapeDtypeStruct((tokens, D_MODEL), F32),
        scratch_shapes=[pltpu.VMEM((TM, D_FF), BF16), pltpu.VMEM((TM, D_ATTN + D_SGU), BF16)]
                       + [pltpu.VMEM((N_SLABS, TM, LANES), F32), pltpu.VMEM((TM, LANES), F32)] * (N_PAT - 1),
        compiler_params=_params(1),
        name="merge_ffn",
    )(x, *attn, sg, ga, gs, wo, *ffn_w, fgain)


def kernel(x, norm_ffn1, ffn1_w_gate, ffn1_w_up, ffn1_w_down, norm_mix, w_in, sgu_ln_g, sgu_ln_b,
           sgu_w, sgu_b, out_norm_attn, out_norm_sgu, w_out, norm_ffn2, ffn2_w_gate, ffn2_w_up,
           ffn2_w_down, final_norm):
    batch, seq, _ = x.shape
    tokens = batch * seq
    assert x.shape[2] == D_MODEL and tokens % TM == 0 and seq % TM == 0 and TM % CHUNK == 0

    row = lambda p: p[:, None, :]
    n1, n2, nm = row(norm_ffn1), row(norm_ffn2), row(norm_mix)
    lng, lnb = row(sgu_ln_g), row(sgu_ln_b)
    ga, gs = row(out_norm_attn), row(out_norm_sgu)
    sgu_w_b = sgu_w.astype(BF16)
    sgu_b_col = sgu_b[..., None]
    fgain = final_norm[None, :]
    rope = _rope_tables(seq)

    stacked = dict(gate_up1=(ffn1_w_gate, ffn1_w_up), down1=(ffn1_w_down,), w_in=(w_in,),
                   gate_up2=(ffn2_w_gate, ffn2_w_up), down2=(ffn2_w_down,), w_out=(w_out,))
    next_layer_jobs = (("gate_up1",), ("gate_up2",), ("down1", "down2", "w_in", "w_out"))
    same_layer_jobs = (("gate_up2",), (), ("down2", "w_out"))

    def cast_here(name):
        ws = [w[0].astype(BF16) for w in stacked[name]]
        if len(ws) == 1:
            return ws[0]
        chunks = [w[:, c * FF_CHUNK:(c + 1) * FF_CHUNK] for c in range(N_FF_CHUNKS) for w in ws]
        return jnp.concatenate(chunks, axis=1)

    bf16 = {(name, 0): cast_here(name) for name in ("gate_up1", "down1", "w_in")}

    xt = x.reshape(tokens, D_MODEL)
    for layer in range(DEPTH):
        ffn1_w = [bf16[(name, layer)] for name in ("gate_up1", "down1")]
        xt, qs, ks, vs, sg = _ffn_inproj(xt, (n1, *ffn1_w), nm, bf16[("w_in", layer)], rope, lng, lnb,
                                         sgu_w_b, sgu_b_col, layer, seq)
        attn = []
        for p, (q, k, v, dil) in enumerate(zip(qs, ks, vs, DILATIONS)):
            jobs = [(name, layer + 1) for name in next_layer_jobs[p]] if layer + 1 < DEPTH else []
            jobs += [(name, layer) for name in same_layer_jobs[p] if (name, layer) not in bf16]
            o, lse, cast = _attention(q, k, v, dil, seq, [(stacked[n], l) for n, l in jobs])
            attn += [o, lse]
            bf16.update(zip(jobs, cast))
        ffn2_w = [bf16[(name, layer)] for name in ("gate_up2", "down2")]
        xt = _merge_ffn(xt, attn, sg, ga, gs, bf16[("w_out", layer)], (n2, *ffn2_w), fgain, layer,
                        layer == DEPTH - 1)
    return xt.reshape(batch, seq, D_MODEL)
```

```python
import functools
import math

import numpy as np
import jax
import jax.numpy as jnp
from jax import lax
from jax.experimental import pallas as pl
from jax.experimental.pallas import tpu as pltpu

F32 = jnp.float32
BF16 = jnp.bfloat16

D_MODEL = 1024
DEPTH = 4
HEAD_DIM = 64
D_ATTN = 512
D_SGU = 512
N_SGU_GROUPS = 4
CHUNK = 128
D_IN = 3 * D_ATTN + 2 * D_SGU
D_FF = 2816
DILATIONS = (1, 4, 16)
N_PAT = len(DILATIONS)
BAND = 64
ROPE_THETA = 500000.0
ROPE_DIM = HEAD_DIM // 4
EPS = 1e-6
NEG_INF = -1e30
LOG2_E = math.log2(math.e)
LN_2 = math.log(2.0)

LANES = 128
SUBLANES = 8
V7X_VMEM_BYTES = 64 * 1024 * 1024
VMEM_LIMIT = V7X_VMEM_BYTES * 7 // 8

TM = 512
FF_CHUNK = 256
N_FF_CHUNKS = D_FF // FF_CHUNK
N_SLABS = D_ATTN // LANES
MAX_ATT_ROWS = 2048
QB = 2 * BAND
KW = QB + 2 * BAND


def _rms(x, gain):
    return x * lax.rsqrt(jnp.mean(x * x, axis=-1, keepdims=True) + EPS) * gain


def _params(n_axes):
    return pltpu.CompilerParams(dimension_semantics=("arbitrary",) * n_axes,
                                vmem_limit_bytes=VMEM_LIMIT)


def _resident(shape, layer):
    nd = len(shape)
    return pl.BlockSpec((None,) + shape, lambda *_: (layer,) + (0,) * nd,
                        pipeline_mode=pl.Buffered(1))


def _resident_weight(shape):
    return pl.BlockSpec(shape, lambda *_: (0,) * len(shape), pipeline_mode=pl.Buffered(1))


def _view_shape(tokens, dil, dtype):
    return jax.ShapeDtypeStruct((tokens // dil, dil * D_ATTN), dtype)


def _ordering_zero(values):
    acc = None
    for v in values:
        bits = pltpu.bitcast(v[0:SUBLANES, 0:LANES], jnp.uint32)
        acc = bits if acc is None else acc | bits
    return ((acc >> 16) >> 16)[0:1, :].astype(F32)


def _ffn_apply(x, gain_ref, wg_ref, wu_ref, wd_ref, act_ref, side_work=()):
    h = _rms(x, gain_ref[...]).astype(BF16)
    for c in range(N_FF_CHUNKS):
        cols = slice(c * FF_CHUNK, (c + 1) * FF_CHUNK)
        g = jnp.dot(h, wg_ref[:, cols], preferred_element_type=F32)
        u = jnp.dot(h, wu_ref[:, cols], preferred_element_type=F32)
        a = g * jax.nn.sigmoid(g) * u
        todo = side_work[c * len(side_work) // N_FF_CHUNKS:(c + 1) * len(side_work) // N_FF_CHUNKS]
        if todo:
            zero = _ordering_zero([v for work in todo for v in work()])
            a = a + jnp.concatenate([zero] * (FF_CHUNK // LANES), axis=1)
        act_ref[:, cols] = a.astype(BF16)
    return x + 0.5 * jnp.dot(act_ref[...], wd_ref[...], preferred_element_type=F32)


def _ffn_specs(layer):
    return [_resident((1, D_MODEL), layer), _resident_weight((D_MODEL, D_FF)),
            _resident_weight((D_MODEL, D_FF)), _resident_weight((D_FF, D_MODEL))]


def _gelu(x):
    return 0.5 * x * (1.0 + lax.erf(x * (1.0 / math.sqrt(2.0))))


def _inproj_finish_work(raw, cos_ref, sina_ref, sinb_ref, lng_ref, lnb_ref, sw_ref, sb_ref,
                        q_refs, k_refs, v_refs, sg_ref, stage_ref):
    def qkv_slab(col0, dst_refs, rotary, scale, s):
        def work():
            t = raw[col0 // D_ATTN][:, s * LANES:(s + 1) * LANES]
            if rotary:
                half = ROPE_DIM // 2
                t = (t * cos_ref[...] + pltpu.roll(t, half, 1) * sina_ref[...]
                     + pltpu.roll(t, LANES - half, 1) * sinb_ref[...])
            if scale != 1.0:
                t = t * scale
            dst_refs[0][:, s * LANES:(s + 1) * LANES] = t.astype(BF16)
            stage_ref[0, s] = t
            produced = [t]
            for k in range(1, N_PAT):
                d_prev, dil = DILATIONS[k - 1], DILATIONS[k]
                ratio, n_prev, n = dil // d_prev, TM // d_prev, TM // dil
                for e in range(d_prev):
                    for c in range(ratio):
                        r = c * d_prev + e
                        blk = stage_ref[k - 1, s, pl.ds(e * n_prev + c, n, stride=ratio), :]
                        dst_refs[k][:, r * D_ATTN + s * LANES:r * D_ATTN + (s + 1) * LANES] = (
                            blk.astype(BF16))
                        if k + 1 < N_PAT:
                            stage_ref[k, s, r * n:(r + 1) * n, :] = blk
                        produced.append(blk)
            return produced
        return work

    def sgu_chunk(c):
        def work():
            rows = slice(c * CHUNK, (c + 1) * CHUNK)
            u = _gelu(raw[3][rows, :])
            g = _gelu(raw[4][rows, :])
            gc = g - jnp.mean(g, axis=-1, keepdims=True)
            gn = (gc * lax.rsqrt(jnp.mean(gc * gc, axis=-1, keepdims=True) + EPS) * lng_ref[...]
                  + lnb_ref[...]).astype(BF16)
            group = D_SGU // N_SGU_GROUPS
            produced = []
            for gi in range(N_SGU_GROUPS):
                cols = slice(gi * group, (gi + 1) * group)
                mixed = jnp.dot(sw_ref[gi], gn[:, cols], preferred_element_type=F32) + sb_ref[gi]
                gated = u[:, cols] * mixed
                sg_ref[rows, cols] = gated.astype(BF16)
                produced.append(gated)
            return produced
        return work

    q_scale = LOG2_E / math.sqrt(HEAD_DIM)
    pieces = [qkv_slab(0, q_refs, True, q_scale, s) for s in range(N_SLABS)]
    pieces += [qkv_slab(D_ATTN, k_refs, True, 1.0, s) for s in range(N_SLABS)]
    pieces += [qkv_slab(2 * D_ATTN, v_refs, False, 1.0, s) for s in range(N_SLABS)]
    pieces += [sgu_chunk(c) for c in range(TM // CHUNK)]
    return pieces


def _ffn_inproj_kernel(x_ref, *refs):
    ffn_refs, gain_ref, w_ref, finish_refs = refs[:4], refs[4], refs[5], refs[6:13]
    xo_ref = refs[13]
    q_refs, k_refs, v_refs = (refs[14 + i * N_PAT:14 + (i + 1) * N_PAT] for i in range(3))
    sg_ref, act_ref, stage_ref = refs[14 + 3 * N_PAT:]
    x = _ffn_apply(x_ref[...], *ffn_refs, act_ref)
    xo_ref[...] = x
    h = _rms(x, gain_ref[...]).astype(BF16)
    raw = [jnp.dot(h, w_ref[:, c0:c0 + D_ATTN], preferred_element_type=F32)
           for c0 in range(0, D_IN, D_ATTN)]
    for work in _inproj_finish_work(raw, *finish_refs, q_refs, k_refs, v_refs, sg_ref, stage_ref):
        work()


def _ffn_inproj(x, ffn_w, gain, w_in, rope, lng, lnb, sw, sb, layer, seq):
    tokens = x.shape[0]
    rows = lambda s: (s, 0)
    table = pl.BlockSpec((TM, LANES), lambda s: (s % (seq // TM), 0))
    views = [pl.BlockSpec((TM // d, d * D_ATTN), rows) for d in DILATIONS]
    view_shapes = [_view_shape(tokens, d, BF16) for d in DILATIONS]
    outs = pl.pallas_call(
        _ffn_inproj_kernel,
        grid=(tokens // TM,),
        in_specs=[pl.BlockSpec((TM, D_MODEL), rows)] + _ffn_specs(layer)
                 + [_resident((1, D_MODEL), layer), _resident_weight((D_MODEL, D_IN)),
                    table, table, table,
                    _resident((1, D_SGU), layer), _resident((1, D_SGU), layer),
                    _resident((N_SGU_GROUPS, CHUNK, CHUNK), layer),
                    _resident((N_SGU_GROUPS, CHUNK, 1), layer)],
        out_specs=[pl.BlockSpec((TM, D_MODEL), rows)] + views * 3 + [pl.BlockSpec((TM, D_SGU), rows)],
        out_shape=[jax.ShapeDtypeStruct((tokens, D_MODEL), F32)] + view_shapes * 3
                  + [jax.ShapeDtypeStruct((tokens, D_SGU), BF16)],
        scratch_shapes=[pltpu.VMEM((TM, D_FF), BF16), pltpu.VMEM((N_PAT - 1, N_SLABS, TM, LANES), F32)],
        compiler_params=_params(1),
        name="ffn_inproj",
    )(x, *ffn_w, gain, w_in, *rope, lng, lnb, sw, sb)
    return (outs[0], outs[1:1 + N_PAT], outs[1 + N_PAT:1 + 2 * N_PAT],
            outs[1 + 2 * N_PAT:1 + 3 * N_PAT], outs[-1])


def _rope_tables(seq):
    half = ROPE_DIM // 2
    inv_freq = ROPE_THETA ** (-np.arange(0, ROPE_DIM, 2, dtype=np.float64) / ROPE_DIM)
    ang = np.arange(seq, dtype=np.float64)[:, None] * inv_freq[None, :]
    cos = np.ones((seq, LANES))
    sina = np.zeros((seq, LANES))
    sinb = np.zeros((seq, LANES))
    for head0 in range(0, LANES, HEAD_DIM):
        cos[:, head0:head0 + half] = np.cos(ang)
        cos[:, head0 + half:head0 + 2 * half] = np.cos(ang)
        sinb[:, head0:head0 + half] = -np.sin(ang)
        sina[:, head0 + half:head0 + 2 * half] = np.sin(ang)
    return tuple(jnp.asarray(t, F32) for t in (cos, sina, sinb))


def _attn_kernel(q_ref, kc_ref, kp_ref, kn_ref, vc_ref, vp_ref, vn_ref, *refs,
                 sub_len, rows, n_res, n_cast):
    cast_src, (o_ref, lse_ref) = refs[:n_cast], refs[n_cast:n_cast + 2]
    cast_dst, (vbuf,) = refs[n_cast + 2:2 * n_cast + 2], refs[2 * n_cast + 2:]
    for src, dst in zip(cast_src, cast_dst):
        dst[...] = src[...].astype(BF16)

    ones = jnp.ones((rows + 2 * BAND, LANES), BF16)
    for res in range(n_res):
        c0 = res * D_ATTN
        for pair in range(N_SLABS):
            src = slice(c0 + pair * LANES, c0 + (pair + 1) * LANES)
            dst = slice(2 * pair * LANES, (2 * pair + 1) * LANES)
            vbuf[res, 0:BAND, dst] = vp_ref[:, src]
            vbuf[res, BAND:BAND + rows, dst] = vc_ref[:, src]
            vbuf[res, BAND + rows:, dst] = vn_ref[:, src]
            vbuf[res, :, (2 * pair + 1) * LANES:(2 * pair + 2) * LANES] = ones

    first_row = (pl.program_id(0) % (sub_len // rows)) * rows
    delta = (lax.broadcasted_iota(jnp.int32, (QB, KW), 1)
             - lax.broadcasted_iota(jnp.int32, (QB, KW), 0))
    band_bias = jnp.where((delta >= 0) & (delta <= 2 * BAND), 0.0, NEG_INF)
    key_col = lax.broadcasted_iota(jnp.int32, (1, KW), 1)
    lane = lax.broadcasted_iota(jnp.int32, (1, LANES), 1)
    low_head = lane < HEAD_DIM

    for j in range(rows // QB):
        row0 = j * QB
        key_pos = key_col + (first_row + row0 - BAND)
        bias = band_bias + jnp.where((key_pos >= 0) & (key_pos < sub_len), 0.0, NEG_INF)
        bias = jnp.concatenate([bias, bias], axis=0)
        for res in range(n_res):
            lse_tile = jnp.zeros((QB, LANES), F32)
            for pair in range(N_SLABS):
                cols = slice(res * D_ATTN + pair * LANES, res * D_ATTN + (pair + 1) * LANES)
                q2 = q_ref[row0:row0 + QB, cols]
                zero = jnp.zeros_like(q2)
                qs = jnp.concatenate([jnp.where(low_head, q2, zero), jnp.where(low_head, zero, q2)],
                                     axis=0)
                lo, hi = row0 - BAND, row0 + QB + BAND
                k2 = kc_ref[max(lo, 0):min(hi, rows), cols]
                if lo < 0:
                    k2 = jnp.concatenate([kp_ref[:, cols], k2], axis=0)
                if hi > rows:
                    k2 = jnp.concatenate([k2, kn_ref[:, cols]], axis=0)
                s = lax.dot_general(qs, k2, (((1,), (1,)), ((), ())),
                                    preferred_element_type=F32) + bias
                m = jnp.max(s, axis=-1, keepdims=True)
                p = jnp.exp2(s - m).astype(BF16)
                pv = jnp.dot(p, vbuf[res, row0:row0 + KW, 2 * pair * LANES:(2 * pair + 2) * LANES],
                             preferred_element_type=F32)
                num = jnp.where(low_head, pv[:QB, :LANES], pv[QB:, :LANES])
                den = jnp.where(low_head, pv[:QB, LANES:], pv[QB:, LANES:])
                o_ref[row0:row0 + QB, cols] = (num / den).astype(BF16)
                lse_pair = jnp.where(low_head, m[:QB], m[QB:]) * LN_2 + jnp.log(den)
                lse_tile = jnp.where((lane == pair) | (lane == HEAD_DIM + pair), lse_pair, lse_tile)
            lse_ref[row0:row0 + QB, res * LANES:(res + 1) * LANES] = lse_tile


def _attention(q, k, v, dil, seq, cast_jobs=()):
    total_rows = q.shape[0]
    sub_len = seq // dil
    rows = min(MAX_ATT_ROWS, sub_len)
    n_res = min(MAX_ATT_ROWS // rows, dil)
    assert sub_len % rows == 0 and rows % QB == 0 and dil % n_res == 0
    halos_per_step = rows // BAND
    last_halo = total_rows // BAND - 1
    width = n_res * D_ATTN
    grid = (total_rows // rows, dil // n_res)
    cur = pl.BlockSpec((rows, width), lambda i, r: (i, r))
    prev = pl.BlockSpec((BAND, width), lambda i, r: (jnp.maximum(i * halos_per_step - 1, 0), r))
    nxt = pl.BlockSpec((BAND, width),
                       lambda i, r: (jnp.minimum((i + 1) * halos_per_step, last_halo), r))
    n_steps = grid[0] * grid[1]
    cast_in, cast_out, cast_shapes = [], [], []
    for w, w_layer in cast_jobs:
        _, w_rows, w_cols = w.shape
        chunk = w_rows // n_steps
        assert chunk * n_steps == w_rows and chunk % (2 * SUBLANES) == 0
        cast_in.append(pl.BlockSpec((None, chunk, w_cols),
                                    lambda i, r, w_layer=w_layer: (w_layer, i * grid[1] + r, 0)))
        cast_out.append(pl.BlockSpec((chunk, w_cols), lambda i, r: (i * grid[1] + r, 0)))
        cast_shapes.append(jax.ShapeDtypeStruct((w_rows, w_cols), BF16))
    outs = pl.pallas_call(
        functools.partial(_attn_kernel, sub_len=sub_len, rows=rows, n_res=n_res,
                          n_cast=len(cast_jobs)),
        grid=grid,
        in_specs=[cur, cur, prev, nxt, cur, prev, nxt] + cast_in,
        out_specs=[cur, pl.BlockSpec((rows, n_res * LANES), lambda i, r: (i, r))] + cast_out,
        out_shape=[jax.ShapeDtypeStruct(q.shape, BF16),
                   jax.ShapeDtypeStruct((total_rows, dil * LANES), F32)] + cast_shapes,
        scratch_shapes=[pltpu.VMEM((n_res, rows + 2 * BAND, 2 * D_ATTN), BF16)],
        compiler_params=_params(2),
        name=f"attn_d{dil}",
    )(q, k, k, k, v, v, v, *[w for w, _ in cast_jobs])
    return outs[0], outs[1], list(outs[2:])


def _merge_work(o_refs, l_refs, sg_ref, ga_ref, gs_ref, norm_ref, stage_refs):
    def unview_slab(s):
        def work():
            produced = []
            for p, dil in enumerate(DILATIONS[1:]):
                for r in range(dil):
                    blk = o_refs[p + 1][:, r * D_ATTN + s * LANES:r * D_ATTN + (s + 1) * LANES]
                    blk = blk.astype(F32)
                    stage_refs[2 * p][s, pl.ds(r, TM // dil, stride=dil), :] = blk
                    produced.append(blk)
            return produced
        return work

    def unview_lse():
        produced = []
        for p, dil in enumerate(DILATIONS[1:]):
            for r in range(dil):
                blk = l_refs[p + 1][:, r * LANES:(r + 1) * LANES]
                stage_refs[2 * p + 1][pl.ds(r, TM // dil, stride=dil), :] = blk
                produced.append(blk)
        return produced

    def merge_rows(c):
        def work():
            rows = slice(c * CHUNK, (c + 1) * CHUNK)
            low_head = lax.broadcasted_iota(jnp.int32, (1, LANES), 1) < HEAD_DIM
            lses = [l_refs[0][rows, :]] + [stage_refs[2 * p + 1][rows, :] for p in range(N_PAT - 1)]
            m = functools.reduce(jnp.maximum, lses)
            es = [jnp.exp(l - m) for l in lses]
            inv = 1.0 / sum(es)
            slabs = []
            for s in range(N_SLABS):
                cols = slice(s * LANES, (s + 1) * LANES)
                outs = [o_refs[0][rows, cols].astype(F32)]
                outs += [stage_refs[2 * p][s, rows, :] for p in range(N_PAT - 1)]
                acc = None
                for e, o in zip(es, outs):
                    w = e * inv
                    w = jnp.where(low_head, w[:, s:s + 1], w[:, HEAD_DIM + s:HEAD_DIM + s + 1])
                    acc = w * o if acc is None else acc + w * o
                slabs.append(acc)
            na = _rms(jnp.concatenate(slabs, axis=1), ga_ref[...])
            ns = _rms(sg_ref[rows, :].astype(F32), gs_ref[...])
            norm_ref[rows, 0:D_ATTN] = na.astype(BF16)
            norm_ref[rows, D_ATTN:] = ns.astype(BF16)
            return [na, ns]
        return work

    return ([unview_slab(s) for s in range(N_SLABS)] + [unview_lse]
            + [merge_rows(c) for c in range(TM // CHUNK)])


def _merge_ffn_kernel(x_ref, *refs, n_tiles, final):
    o_refs, l_refs = refs[0:2 * N_PAT:2], refs[1:2 * N_PAT:2]
    sg_ref, ga_ref, gs_ref, wo_ref = refs[2 * N_PAT:2 * N_PAT + 4]
    ffn_refs = refs[2 * N_PAT + 4:2 * N_PAT + 8]
    fgain_ref, out_ref, act_ref, norm_ref = refs[2 * N_PAT + 8:2 * N_PAT + 12]
    stage_refs = refs[2 * N_PAT + 12:]
    step = pl.program_id(0)
    merge = _merge_work(o_refs, l_refs, sg_ref, ga_ref, gs_ref, norm_ref, stage_refs)

    def project_and_ffn(side_work):
        x = x_ref[...] + jnp.dot(norm_ref[...], wo_ref[...], preferred_element_type=F32)
        y = _ffn_apply(x, *ffn_refs, act_ref, side_work=side_work)
        out_ref[...] = _rms(y, fgain_ref[...]) if final else y

    @pl.when(step == 0)
    def _():
        for work in merge:
            work()

    @pl.when((step > 0) & (step < n_tiles))
    def _():
        project_and_ffn(merge)

    @pl.when(step == n_tiles)
    def _():
        project_and_ffn(())


def _merge_ffn(x, attn, sg, ga, gs, wo, ffn_w, fgain, layer, final):
    tokens = x.shape[0]
    n_tiles = tokens // TM
    lead = lambda s: (jnp.minimum(s, n_tiles - 1), 0)
    lag = lambda s: (jnp.maximum(s - 1, 0), 0)
    views = [pl.BlockSpec((TM // d, d * width), lead) for d in DILATIONS for width in (D_ATTN, LANES)]
    return pl.pallas_call(
        functools.partial(_merge_ffn_kernel, n_tiles=n_tiles, final=final),
        grid=(n_tiles + 1,),
        in_specs=[pl.BlockSpec((TM, D_MODEL), lag)] + views + [pl.BlockSpec((TM, D_SGU), lead)]
                 + [_resident((1, D_ATTN), layer), _resident((1, D_SGU), layer),
                    _resident_weight((D_ATTN + D_SGU, D_MODEL))]
                 + _ffn_specs(layer) + [pl.BlockSpec((1, D_MODEL), lambda s: (0, 0))],
        out_specs=pl.BlockSpec((TM, D_MODEL), lag),
        out_shape=jax.ShapeDtypeStruct((tokens, D_MODEL), F32),
        scratch_shapes=[pltpu.VMEM((TM, D_FF), BF16), pltpu.VMEM((TM, D_ATTN + D_SGU), BF16)]
                       + [pltpu.VMEM((N_SLABS, TM, LANES), F32), pltpu.VMEM((TM, LANES), F32)] * (N_PAT - 1),
        compiler_params=_params(1),
        name="merge_ffn",
    )(x, *attn, sg, ga, gs, wo, *ffn_w, fgain)


def kernel(x, norm_ffn1, ffn1_w_gate, ffn1_w_up, ffn1_w_down, norm_mix, w_in, sgu_ln_g, sgu_ln_b,
           sgu_w, sgu_b, out_norm_attn, out_norm_sgu, w_out, norm_ffn2, ffn2_w_gate, ffn2_w_up,
           ffn2_w_down, final_norm):
    batch, seq, _ = x.shape
    tokens = batch * seq
    assert x.shape[2] == D_MODEL and tokens % TM == 0 and seq % TM == 0 and TM % CHUNK == 0

    row = lambda p: p[:, None, :]
    n1, n2, nm = row(norm_ffn1), row(norm_ffn2), row(norm_mix)
    lng, lnb = row(sgu_ln_g), row(sgu_ln_b)
    ga, gs = row(out_norm_attn), row(out_norm_sgu)
    sgu_w_b = sgu_w.astype(BF16)
    sgu_b_col = sgu_b[..., None]
    fgain = final_norm[None, :]
    rope = _rope_tables(seq)

    stacked = dict(gate1=ffn1_w_gate, up1=ffn1_w_up, down1=ffn1_w_down, w_in=w_in,
                   gate2=ffn2_w_gate, up2=ffn2_w_up, down2=ffn2_w_down, w_out=w_out)
    early = ("gate1", "up1", "down1", "w_in")
    next_layer_jobs = (("gate1", "up1"), ("gate2", "up2"), ("down1", "down2", "w_in", "w_out"))
    same_layer_jobs = (("gate2",), ("up2",), ("down2", "w_out"))
    bf16 = {(name, 0): stacked[name][0].astype(BF16) for name in early}

    xt = x.reshape(tokens, D_MODEL)
    for layer in range(DEPTH):
        ffn1_w = [bf16[(name, layer)] for name in ("gate1", "up1", "down1")]
        xt, qs, ks, vs, sg = _ffn_inproj(xt, (n1, *ffn1_w), nm, bf16[("w_in", layer)], rope, lng, lnb,
                                         sgu_w_b, sgu_b_col, layer, seq)
        attn = []
        for p, (q, k, v, dil) in enumerate(zip(qs, ks, vs, DILATIONS)):
            jobs = [(name, layer + 1) for name in next_layer_jobs[p]] if layer + 1 < DEPTH else []
            jobs += [(name, layer) for name in same_layer_jobs[p] if (name, layer) not in bf16]
            o, lse, cast = _attention(q, k, v, dil, seq, [(stacked[n], l) for n, l in jobs])
            attn += [o, lse]
            bf16.update(zip(jobs, cast))
        ffn2_w = [bf16[(name, layer)] for name in ("gate2", "up2", "down2")]
        xt = _merge_ffn(xt, attn, sg, ga, gs, bf16[("w_out", layer)], (n2, *ffn2_w), fgain, layer,
                        layer == DEPTH - 1)
    return xt.reshape(batch, seq, D_MODEL)
```

```python
import functools
import math

import numpy as np
import jax
import jax.numpy as jnp
from jax import lax
from jax.experimental import pallas as pl
from jax.experimental.pallas import tpu as pltpu

F32 = jnp.float32
BF16 = jnp.bfloat16

D_MODEL = 1024
DEPTH = 4
HEAD_DIM = 64
D_ATTN = 512
D_SGU = 512
N_SGU_GROUPS = 4
CHUNK = 128
D_IN = 3 * D_ATTN + 2 * D_SGU
D_FF = 2816
DILATIONS = (1, 4, 16)
N_PAT = len(DILATIONS)
BAND = 64
ROPE_THETA = 500000.0
ROPE_DIM = HEAD_DIM // 4
EPS = 1e-6
NEG_INF = -1e30
LOG2_E = math.log2(math.e)
LN_2 = math.log(2.0)

LANES = 128
SUBLANES = 8
V7X_VMEM_BYTES = 64 * 1024 * 1024
VMEM_LIMIT = V7X_VMEM_BYTES * 7 // 8

TM = 512
FF_CHUNK = 256
N_FF_CHUNKS = D_FF // FF_CHUNK
N_SLABS = D_ATTN // LANES
MAX_ATT_ROWS = 2048
QB = 2 * BAND
KW = QB + 2 * BAND


def _rms(x, gain):
    return x * lax.rsqrt(jnp.mean(x * x, axis=-1, keepdims=True) + EPS) * gain


def _params(n_axes):
    return pltpu.CompilerParams(dimension_semantics=("arbitrary",) * n_axes,
                                vmem_limit_bytes=VMEM_LIMIT)


def _resident(shape, layer):
    nd = len(shape)
    return pl.BlockSpec((None,) + shape, lambda *_: (layer,) + (0,) * nd,
                        pipeline_mode=pl.Buffered(1))


def _resident_weight(shape):
    return pl.BlockSpec(shape, lambda *_: (0,) * len(shape), pipeline_mode=pl.Buffered(1))


def _view_shape(tokens, dil, dtype):
    return jax.ShapeDtypeStruct((tokens // dil, dil * D_ATTN), dtype)


def _ordering_zero(values):
    acc = None
    for v in values:
        bits = pltpu.bitcast(v[0:SUBLANES, 0:LANES], jnp.uint32)
        acc = bits if acc is None else acc | bits
    return ((acc >> 16) >> 16)[0:1, :].astype(F32)


def _ffn_apply(x, gain_ref, wg_ref, wu_ref, wd_ref, act_ref, side_work=()):
    h = _rms(x, gain_ref[...]).astype(BF16)
    for c in range(N_FF_CHUNKS):
        cols = slice(c * FF_CHUNK, (c + 1) * FF_CHUNK)
        g = jnp.dot(h, wg_ref[:, cols], preferred_element_type=F32)
        u = jnp.dot(h, wu_ref[:, cols], preferred_element_type=F32)
        a = g * jax.nn.sigmoid(g) * u
        todo = side_work[c * len(side_work) // N_FF_CHUNKS:(c + 1) * len(side_work) // N_FF_CHUNKS]
        if todo:
            zero = _ordering_zero([v for work in todo for v in work()])
            a = a + jnp.concatenate([zero] * (FF_CHUNK // LANES), axis=1)
        act_ref[:, cols] = a.astype(BF16)
    return x + 0.5 * jnp.dot(act_ref[...], wd_ref[...], preferred_element_type=F32)


def _ffn_specs(layer):
    return [_resident((1, D_MODEL), layer), _resident_weight((D_MODEL, D_FF)),
            _resident_weight((D_MODEL, D_FF)), _resident_weight((D_FF, D_MODEL))]


def _gelu(x):
    return 0.5 * x * (1.0 + lax.erf(x * (1.0 / math.sqrt(2.0))))


def _inproj_finish_work(raw, cos_ref, sina_ref, sinb_ref, lng_ref, lnb_ref, sw_ref, sb_ref,
                        q_refs, k_refs, v_refs, sg_ref, stage_ref):
    def qkv_slab(col0, dst_refs, rotary, scale, s):
        def work():
            t = raw[col0 // D_ATTN][:, s * LANES:(s + 1) * LANES]
            if rotary:
                half = ROPE_DIM // 2
                t = (t * cos_ref[...] + pltpu.roll(t, half, 1) * sina_ref[...]
                     + pltpu.roll(t, LANES - half, 1) * sinb_ref[...])
            if scale != 1.0:
                t = t * scale
            dst_refs[0][:, s * LANES:(s + 1) * LANES] = t.astype(BF16)
            stage_ref[0, s] = t
            produced = [t]
            for k in range(1, N_PAT):
                d_prev, dil = DILATIONS[k - 1], DILATIONS[k]
                ratio, n_prev, n = dil // d_prev, TM // d_prev, TM // dil
                for e in range(d_prev):
                    for c in range(ratio):
                        r = c * d_prev + e
                        blk = stage_ref[k - 1, s, pl.ds(e * n_prev + c, n, stride=ratio), :]
                        dst_refs[k][:, r * D_ATTN + s * LANES:r * D_ATTN + (s + 1) * LANES] = (
                            blk.astype(BF16))
                        if k + 1 < N_PAT:
                            stage_ref[k, s, r * n:(r + 1) * n, :] = blk
                        produced.append(blk)
            return produced
        return work

    def sgu_chunk(c):
        def work():
            rows = slice(c * CHUNK, (c + 1) * CHUNK)
            u = _gelu(raw[3][rows, :])
            g = _gelu(raw[4][rows, :])
            gc = g - jnp.mean(g, axis=-1, keepdims=True)
            gn = (gc * lax.rsqrt(jnp.mean(gc * gc, axis=-1, keepdims=True) + EPS) * lng_ref[...]
                  + lnb_ref[...]).astype(BF16)
            group = D_SGU // N_SGU_GROUPS
            produced = []
            for gi in range(N_SGU_GROUPS):
                cols = slice(gi * group, (gi + 1) * group)
                mixed = jnp.dot(sw_ref[gi], gn[:, cols], preferred_element_type=F32) + sb_ref[gi]
                gated = u[:, cols] * mixed
                sg_ref[rows, cols] = gated.astype(BF16)
                produced.append(gated)
            return produced
        return work

    q_scale = LOG2_E / math.sqrt(HEAD_DIM)
    pieces = [qkv_slab(0, q_refs, True, q_scale, s) for s in range(N_SLABS)]
    pieces += [qkv_slab(D_ATTN, k_refs, True, 1.0, s) for s in range(N_SLABS)]
    pieces += [qkv_slab(2 * D_ATTN, v_refs, False, 1.0, s) for s in range(N_SLABS)]
    pieces += [sgu_chunk(c) for c in range(TM // CHUNK)]
    return pieces


def _ffn_inproj_kernel(x_ref, *refs):
    ffn_refs, gain_ref, w_ref, finish_refs = refs[:4], refs[4], refs[5], refs[6:13]
    xo_ref = refs[13]
    q_refs, k_refs, v_refs = (refs[14 + i * N_PAT:14 + (i + 1) * N_PAT] for i in range(3))
    sg_ref, act_ref, stage_ref = refs[14 + 3 * N_PAT:]
    x = _ffn_apply(x_ref[...], *ffn_refs, act_ref)
    xo_ref[...] = x
    h = _rms(x, gain_ref[...]).astype(BF16)
    raw = [jnp.dot(h, w_ref[:, c0:c0 + D_ATTN], preferred_element_type=F32)
           for c0 in range(0, D_IN, D_ATTN)]
    for work in _inproj_finish_work(raw, *finish_refs, q_refs, k_refs, v_refs, sg_ref, stage_ref):
        work()


def _ffn_inproj(x, ffn_w, gain, w_in, rope, lng, lnb, sw, sb, layer, seq):
    tokens = x.shape[0]
    rows = lambda s: (s, 0)
    table = pl.BlockSpec((TM, LANES), lambda s: (s % (seq // TM), 0))
    views = [pl.BlockSpec((TM // d, d * D_ATTN), rows) for d in DILATIONS]
    view_shapes = [_view_shape(tokens, d, BF16) for d in DILATIONS]
    outs = pl.pallas_call(
        _ffn_inproj_kernel,
        grid=(tokens // TM,),
        in_specs=[pl.BlockSpec((TM, D_MODEL), rows)] + _ffn_specs(layer)
                 + [_resident((1, D_MODEL), layer), _resident_weight((D_MODEL, D_IN)),
                    table, table, table,
                    _resident((1, D_SGU), layer), _resident((1, D_SGU), layer),
                    _resident((N_SGU_GROUPS, CHUNK, CHUNK), layer),
                    _resident((N_SGU_GROUPS, CHUNK, 1), layer)],
        out_specs=[pl.BlockSpec((TM, D_MODEL), rows)] + views * 3 + [pl.BlockSpec((TM, D_SGU), rows)],
        out_shape=[jax.ShapeDtypeStruct((tokens, D_MODEL), F32)] + view_shapes * 3
                  + [jax.ShapeDtypeStruct((tokens, D_SGU), BF16)],
        scratch_shapes=[pltpu.VMEM((TM, D_FF), BF16), pltpu.VMEM((N_PAT - 1, N_SLABS, TM, LANES), F32)],
        compiler_params=_params(1),
        name="ffn_inproj",
    )(x, *ffn_w, gain, w_in, *rope, lng, lnb, sw, sb)
    return (outs[0], outs[1:1 + N_PAT], outs[1 + N_PAT:1 + 2 * N_PAT],
            outs[1 + 2 * N_PAT:1 + 3 * N_PAT], outs[-1])


def _rope_tables(seq):
    half = ROPE_DIM // 2
    inv_freq = ROPE_THETA ** (-np.arange(0, ROPE_DIM, 2, dtype=np.float64) / ROPE_DIM)
    ang = np.arange(seq, dtype=np.float64)[:, None] * inv_freq[None, :]
    cos = np.ones((seq, LANES))
    sina = np.zeros((seq, LANES))
    sinb = np.zeros((seq, LANES))
    for head0 in range(0, LANES, HEAD_DIM):
        cos[:, head0:head0 + half] = np.cos(ang)
        cos[:, head0 + half:head0 + 2 * half] = np.cos(ang)
        sinb[:, head0:head0 + half] = -np.sin(ang)
        sina[:, head0 + half:head0 + 2 * half] = np.sin(ang)
    return tuple(jnp.asarray(t, F32) for t in (cos, sina, sinb))


def _attn_kernel(q_ref, kc_ref, kp_ref, kn_ref, vc_ref, vp_ref, vn_ref, *refs,
                 sub_len, rows, n_res, n_cast):
    cast_src, (o_ref, lse_ref) = refs[:n_cast], refs[n_cast:n_cast + 2]
    cast_dst, (vbuf,) = refs[n_cast + 2:2 * n_cast + 2], refs[2 * n_cast + 2:]
    for src, dst in zip(cast_src, cast_dst):
        dst[...] = src[...].astype(BF16)

    ones = jnp.ones((rows + 2 * BAND, LANES), BF16)
    for res in range(n_res):
        c0 = res * D_ATTN
        for pair in range(N_SLABS):
            src = slice(c0 + pair * LANES, c0 + (pair + 1) * LANES)
            dst = slice(2 * pair * LANES, (2 * pair + 1) * LANES)
            vbuf[res, 0:BAND, dst] = vp_ref[:, src]
            vbuf[res, BAND:BAND + rows, dst] = vc_ref[:, src]
            vbuf[res, BAND + rows:, dst] = vn_ref[:, src]

    @pl.when((pl.program_id(0) == 0) & (pl.program_id(1) == 0))
    def _():
        for res in range(n_res):
            for pair in range(N_SLABS):
                vbuf[res, :, (2 * pair + 1) * LANES:(2 * pair + 2) * LANES] = ones

    first_row = (pl.program_id(0) % (sub_len // rows)) * rows
    delta = (lax.broadcasted_iota(jnp.int32, (QB, KW), 1)
             - lax.broadcasted_iota(jnp.int32, (QB, KW), 0))
    band_bias = jnp.where((delta >= 0) & (delta <= 2 * BAND), 0.0, NEG_INF)
    key_col = lax.broadcasted_iota(jnp.int32, (1, KW), 1)
    lane = lax.broadcasted_iota(jnp.int32, (1, LANES), 1)
    low_head = lane < HEAD_DIM

    for j in range(rows // QB):
        row0 = j * QB
        key_pos = key_col + (first_row + row0 - BAND)
        bias = band_bias + jnp.where((key_pos >= 0) & (key_pos < sub_len), 0.0, NEG_INF)
        bias = jnp.concatenate([bias, bias], axis=0)
        for res in range(n_res):
            lse_tile = jnp.zeros((QB, LANES), F32)
            for pair in range(N_SLABS):
                cols = slice(res * D_ATTN + pair * LANES, res * D_ATTN + (pair + 1) * LANES)
                q2 = q_ref[row0:row0 + QB, cols]
                zero = jnp.zeros_like(q2)
                qs = jnp.concatenate([jnp.where(low_head, q2, zero), jnp.where(low_head, zero, q2)],
                                     axis=0)
                lo, hi = row0 - BAND, row0 + QB + BAND
                k2 = kc_ref[max(lo, 0):min(hi, rows), cols]
                if lo < 0:
                    k2 = jnp.concatenate([kp_ref[:, cols], k2], axis=0)
                if hi > rows:
                    k2 = jnp.concatenate([k2, kn_ref[:, cols]], axis=0)
                s = lax.dot_general(qs, k2, (((1,), (1,)), ((), ())),
                                    preferred_element_type=F32) + bias
                m = jnp.max(s, axis=-1, keepdims=True)
                p = jnp.exp2(s - m).astype(BF16)
                pv = jnp.dot(p, vbuf[res, row0:row0 + KW, 2 * pair * LANES:(2 * pair + 2) * LANES],
                             preferred_element_type=F32)
                num = jnp.where(low_head, pv[:QB, :LANES], pv[QB:, :LANES])
                den = jnp.where(low_head, pv[:QB, LANES:], pv[QB:, LANES:])
                o_ref[row0:row0 + QB, cols] = (num / den).astype(BF16)
                lse_pair = jnp.where(low_head, m[:QB], m[QB:]) * LN_2 + jnp.log(den)
                lse_tile = jnp.where((lane == pair) | (lane == HEAD_DIM + pair), lse_pair, lse_tile)
            lse_ref[row0:row0 + QB, res * LANES:(res + 1) * LANES] = lse_tile


def _attention(q, k, v, dil, seq, cast_jobs=()):
    total_rows = q.shape[0]
    sub_len = seq // dil
    rows = min(MAX_ATT_ROWS, sub_len)
    n_res = min(MAX_ATT_ROWS // rows, dil)
    assert sub_len % rows == 0 and rows % QB == 0 and dil % n_res == 0
    halos_per_step = rows // BAND
    last_halo = total_rows // BAND - 1
    width = n_res * D_ATTN
    grid = (total_rows // rows, dil // n_res)
    cur = pl.BlockSpec((rows, width), lambda i, r: (i, r))
    prev = pl.BlockSpec((BAND, width), lambda i, r: (jnp.maximum(i * halos_per_step - 1, 0), r))
    nxt = pl.BlockSpec((BAND, width),
                       lambda i, r: (jnp.minimum((i + 1) * halos_per_step, last_halo), r))
    n_steps = grid[0] * grid[1]
    cast_in, cast_out, cast_shapes = [], [], []
    for w, w_layer in cast_jobs:
        _, w_rows, w_cols = w.shape
        chunk = w_rows // n_steps
        assert chunk * n_steps == w_rows and chunk % (2 * SUBLANES) == 0
        cast_in.append(pl.BlockSpec((None, chunk, w_cols),
                                    lambda i, r, w_layer=w_layer: (w_layer, i * grid[1] + r, 0)))
        cast_out.append(pl.BlockSpec((chunk, w_cols), lambda i, r: (i * grid[1] + r, 0)))
        cast_shapes.append(jax.ShapeDtypeStruct((w_rows, w_cols), BF16))
    outs = pl.pallas_call(
        functools.partial(_attn_kernel, sub_len=sub_len, rows=rows, n_res=n_res,
                          n_cast=len(cast_jobs)),
        grid=grid,
        in_specs=[cur, cur, prev, nxt, cur, prev, nxt] + cast_in,
        out_specs=[cur, pl.BlockSpec((rows, n_res * LANES), lambda i, r: (i, r))] + cast_out,
        out_shape=[jax.ShapeDtypeStruct(q.shape, BF16),
                   jax.ShapeDtypeStruct((total_rows, dil * LANES), F32)] + cast_shapes,
        scratch_shapes=[pltpu.VMEM((n_res, rows + 2 * BAND, 2 * D_ATTN), BF16)],
        compiler_params=_params(2),
        name=f"attn_d{dil}",
    )(q, k, k, k, v, v, v, *[w for w, _ in cast_jobs])
    return outs[0], outs[1], list(outs[2:])


def _merge_work(o_refs, l_refs, sg_ref, ga_ref, gs_ref, norm_ref, stage_refs):
    def unview_slab(s):
        def work():
            produced = []
            for p, dil in enumerate(DILATIONS[1:]):
                for r in range(dil):
                    blk = o_refs[p + 1][:, r * D_ATTN + s * LANES:r * D_ATTN + (s + 1) * LANES]
                    blk = blk.astype(F32)
                    stage_refs[2 * p][s, pl.ds(r, TM // dil, stride=dil), :] = blk
                    produced.append(blk)
            return produced
        return work

    def unview_lse():
        produced = []
        for p, dil in enumerate(DILATIONS[1:]):
            for r in range(dil):
                blk = l_refs[p + 1][:, r * LANES:(r + 1) * LANES]
                stage_refs[2 * p + 1][pl.ds(r, TM // dil, stride=dil), :] = blk
                produced.append(blk)
        return produced

    def merge_rows(c):
        def work():
            rows = slice(c * CHUNK, (c + 1) * CHUNK)
            low_head = lax.broadcasted_iota(jnp.int32, (1, LANES), 1) < HEAD_DIM
            lses = [l_refs[0][rows, :]] + [stage_refs[2 * p + 1][rows, :] for p in range(N_PAT - 1)]
            m = functools.reduce(jnp.maximum, lses)
            es = [jnp.exp(l - m) for l in lses]
            inv = 1.0 / sum(es)
            slabs = []
            for s in range(N_SLABS):
                cols = slice(s * LANES, (s + 1) * LANES)
                outs = [o_refs[0][rows, cols].astype(F32)]
                outs += [stage_refs[2 * p][s, rows, :] for p in range(N_PAT - 1)]
                acc = None
                for e, o in zip(es, outs):
                    w = e * inv
                    w = jnp.where(low_head, w[:, s:s + 1], w[:, HEAD_DIM + s:HEAD_DIM + s + 1])
                    acc = w * o if acc is None else acc + w * o
                slabs.append(acc)
            na = _rms(jnp.concatenate(slabs, axis=1), ga_ref[...])
            ns = _rms(sg_ref[rows, :].astype(F32), gs_ref[...])
            norm_ref[rows, 0:D_ATTN] = na.astype(BF16)
            norm_ref[rows, D_ATTN:] = ns.astype(BF16)
            return [na, ns]
        return work

    return ([unview_slab(s) for s in range(N_SLABS)] + [unview_lse]
            + [merge_rows(c) for c in range(TM // CHUNK)])


def _merge_ffn_kernel(x_ref, *refs, n_tiles, final):
    o_refs, l_refs = refs[0:2 * N_PAT:2], refs[1:2 * N_PAT:2]
    sg_ref, ga_ref, gs_ref, wo_ref = refs[2 * N_PAT:2 * N_PAT + 4]
    ffn_refs = refs[2 * N_PAT + 4:2 * N_PAT + 8]
    fgain_ref, out_ref, act_ref, norm_ref = refs[2 * N_PAT + 8:2 * N_PAT + 12]
    stage_refs = refs[2 * N_PAT + 12:]
    step = pl.program_id(0)
    merge = _merge_work(o_refs, l_refs, sg_ref, ga_ref, gs_ref, norm_ref, stage_refs)

    def project_and_ffn(side_work):
        x = x_ref[...] + jnp.dot(norm_ref[...], wo_ref[...], preferred_element_type=F32)
        y = _ffn_apply(x, *ffn_refs, act_ref, side_work=side_work)
        out_ref[...] = _rms(y, fgain_ref[...]) if final else y

    @pl.when(step == 0)
    def _():
        for work in merge:
            work()

    @pl.when((step > 0) & (step < n_tiles))
    def _():
        project_and_ffn(merge)

    @pl.when(step == n_tiles)
    def _():
        project_and_ffn(())


def _merge_ffn(x, attn, sg, ga, gs, wo, ffn_w, fgain, layer, final):
    tokens = x.shape[0]
    n_tiles = tokens // TM
    lead = lambda s: (jnp.minimum(s, n_tiles - 1), 0)
    lag = lambda s: (jnp.maximum(s - 1, 0), 0)
    views = [pl.BlockSpec((TM // d, d * width), lead) for d in DILATIONS for width in (D_ATTN, LANES)]
    return pl.pallas_call(
        functools.partial(_merge_ffn_kernel, n_tiles=n_tiles, final=final),
        grid=(n_tiles + 1,),
        in_specs=[pl.BlockSpec((TM, D_MODEL), lag)] + views + [pl.BlockSpec((TM, D_SGU), lead)]
                 + [_resident((1, D_ATTN), layer), _resident((1, D_SGU), layer),
                    _resident_weight((D_ATTN + D_SGU, D_MODEL))]
                 + _ffn_specs(layer) + [pl.BlockSpec((1, D_MODEL), lambda s: (0, 0))],
        out_specs=pl.BlockSpec((TM, D_MODEL), lag),
        out_shape=jax.ShapeDtypeStruct((tokens, D_MODEL), F32),
        scratch_shapes=[pltpu.VMEM((TM, D_FF), BF16), pltpu.VMEM((TM, D_ATTN + D_SGU), BF16)]
                       + [pltpu.VMEM((N_SLABS, TM, LANES), F32), pltpu.VMEM((TM, LANES), F32)] * (N_PAT - 1),
        compiler_params=_params(1),
        name="merge_ffn",
    )(x, *attn, sg, ga, gs, wo, *ffn_w, fgain)


def kernel(x, norm_ffn1, ffn1_w_gate, ffn1_w_up, ffn1_w_down, norm_mix, w_in, sgu_ln_g, sgu_ln_b,
           sgu_w, sgu_b, out_norm_attn, out_norm_sgu, w_out, norm_ffn2, ffn2_w_gate, ffn2_w_up,
           ffn2_w_down, final_norm):
    batch, seq, _ = x.shape
    tokens = batch * seq
    assert x.shape[2] == D_MODEL and tokens % TM == 0 and seq % TM == 0 and TM % CHUNK == 0

    row = lambda p: p[:, None, :]
    n1, n2, nm = row(norm_ffn1), row(norm_ffn2), row(norm_mix)
    lng, lnb = row(sgu_ln_g), row(sgu_ln_b)
    ga, gs = row(out_norm_attn), row(out_norm_sgu)
    sgu_w_b = sgu_w.astype(BF16)
    sgu_b_col = sgu_b[..., None]
    fgain = final_norm[None, :]
    rope = _rope_tables(seq)

    stacked = dict(gate1=ffn1_w_gate, up1=ffn1_w_up, down1=ffn1_w_down, w_in=w_in,
                   gate2=ffn2_w_gate, up2=ffn2_w_up, down2=ffn2_w_down, w_out=w_out)
    early = ("gate1", "up1", "down1", "w_in")
    next_layer_jobs = (("gate1", "up1"), ("gate2", "up2"), ("down1", "down2", "w_in", "w_out"))
    same_layer_jobs = (("gate2",), ("up2",), ("down2", "w_out"))
    bf16 = {(name, 0): stacked[name][0].astype(BF16) for name in early}

    xt = x.reshape(tokens, D_MODEL)
    for layer in range(DEPTH):
        ffn1_w = [bf16[(name, layer)] for name in ("gate1", "up1", "down1")]
        xt, qs, ks, vs, sg = _ffn_inproj(xt, (n1, *ffn1_w), nm, bf16[("w_in", layer)], rope, lng, lnb,
                                         sgu_w_b, sgu_b_col, layer, seq)
        attn = []
        for p, (q, k, v, dil) in enumerate(zip(qs, ks, vs, DILATIONS)):
            jobs = [(name, layer + 1) for name in next_layer_jobs[p]] if layer + 1 < DEPTH else []
            jobs += [(name, layer) for name in same_layer_jobs[p] if (name, layer) not in bf16]
            o, lse, cast = _attention(q, k, v, dil, seq, [(stacked[n], l) for n, l in jobs])
            attn += [o, lse]
            bf16.update(zip(jobs, cast))
        ffn2_w = [bf16[(name, layer)] for name in ("gate2", "up2", "down2")]
        xt = _merge_ffn(xt, attn, sg, ga, gs, bf16[("w_out", layer)], (n2, *ffn2_w), fgain, layer,
                        layer == DEPTH - 1)
    return xt.reshape(batch, seq, D_MODEL)
```
